```python
import jax, jax.numpy as jnp
from jax import lax
import numpy as np

D_MODEL = 2048
BATCH = 2
SEQ = 4096
DEPTH = 1

D_MIX = D_MODEL
HG_DK = 128
HG_DV = 128
HG_WIDTH = D_MIX // 2
HG_HEADS = HG_WIDTH // HG_DV
HG_QK = HG_HEADS * HG_DK
GDN_DK = 128
GDN_DV = 128
GDN_WIDTH = D_MIX - HG_WIDTH
GDN_HEADS = GDN_WIDTH // GDN_DV
GDN_QK = GDN_HEADS * GDN_DK
CONV_K = 4
CHUNK = 64
D_FF = 4 * D_MODEL
N_MOD = 6
EPS = 1e-6

HG_COLS = 2 * HG_QK + 2 * HG_WIDTH
GDN_CONV_CH = 2 * GDN_QK + GDN_WIDTH
GDN_COLS = GDN_CONV_CH + GDN_WIDTH + 2 * GDN_HEADS
IN_COLS = HG_COLS + GDN_COLS

kernel_name = "hybrid_hgrn2_gdn_parallel_heads_adaln"


def rmsnorm(x, w):
    x32 = x.astype(jnp.float32)
    y = x32 * lax.rsqrt(jnp.mean(x32 * x32, axis=-1, keepdims=True) + EPS)
    return (y * w.astype(jnp.float32)).astype(x.dtype)


def l2norm(x):
    return x * lax.rsqrt(jnp.sum(x * x, axis=-1, keepdims=True) + EPS)


def to_chunks(x):
    B, T, H, D = x.shape
    return x.reshape(B, T // CHUNK, CHUNK, H, D).transpose(1, 0, 3, 2, 4)


def from_chunks(x):
    N, B, H, C, D = x.shape
    return x.transpose(1, 0, 3, 2, 4).reshape(B, N * C, H, D)


def causal_conv(u, w):
    ch = u.shape[-1]
    return lax.conv_general_dilated(
        u, w[:, None, :].astype(u.dtype), window_strides=(1,), padding=[(CONV_K - 1, 0)],
        dimension_numbers=('NWC', 'WIO', 'NWC'), feature_group_count=ch)


def hgrn2_chunked(q, log_f, k, v):
    B, T, H, DK = q.shape
    DV = v.shape[-1]
    causal = jnp.tril(jnp.ones((CHUNK, CHUNK), dtype=bool))

    def step(S, xs):
        qc, lfc, kc, vc = xs
        b = jnp.cumsum(lfc, axis=-2)
        diff = b[:, :, :, None, :] - b[:, :, None, :, :]
        decay = jnp.exp(jnp.where(causal[None, None, :, :, None], diff, -jnp.inf))
        attn = jnp.einsum('bhtk,bhsk,bhtsk->bhts', qc, kc, decay)
        o = (jnp.einsum('bhts,bhsv->bhtv', attn, vc)
             + jnp.einsum('bhtk,bhkv->bhtv', qc * jnp.exp(b), S))
        b_last = b[:, :, -1:, :]
        S_new = (S * jnp.exp(b_last[:, :, 0, :, None])
                 + jnp.einsum('bhsk,bhsv->bhkv', kc * jnp.exp(b_last - b), vc))
        return S_new, o

    S0 = jnp.zeros((B, H, DK, DV), jnp.float32)
    _, o = lax.scan(step, S0, (to_chunks(q), to_chunks(log_f), to_chunks(k), to_chunks(v)))
    return from_chunks(o)


def gated_delta_chunked(q, k, v, log_a, beta):
    B, T, H, DK = q.shape
    DV = v.shape[-1]
    qc, kc, vc = to_chunks(q), to_chunks(k), to_chunks(v)
    g = jnp.cumsum(to_chunks(log_a[..., None])[..., 0], axis=-1)
    bc = to_chunks(beta[..., None])[..., 0]
    incl = jnp.tril(jnp.ones((CHUNK, CHUNK), dtype=bool))
    strict = jnp.tril(jnp.ones((CHUNK, CHUNK), dtype=bool), -1)
    gamma = jnp.exp(jnp.where(incl, g[..., :, None] - g[..., None, :], -jnp.inf))
    kk = jnp.einsum('nbhtk,nbhsk->nbhts', kc, kc)
    m = jnp.where(strict, bc[..., :, None] * kk * gamma, 0.0)
    a_mat = jnp.eye(CHUNK, dtype=jnp.float32) + m
    rhs = jnp.concatenate([vc * bc[..., None], kc * (bc * jnp.exp(g))[..., None]], axis=-1)
    sol = lax.linalg.triangular_solve(a_mat, rhs, left_side=True, lower=True, unit_diagonal=True)
    u, w = sol[..., :DV], sol[..., DV:]
    qk = jnp.einsum('nbhtk,nbhsk->nbhts', qc, kc) * gamma
    q_dec = qc * jnp.exp(g)[..., None]
    k_tail = kc * jnp.exp(g[..., -1:] - g)[..., None]
    tail = jnp.exp(g[..., -1])

    def step(S, xs):
        u_c, w_c, qk_c, qd_c, kt_c, tl_c = xs
        v_new = u_c - jnp.einsum('bhck,bhkv->bhcv', w_c, S)
        o = jnp.einsum('bhck,bhkv->bhcv', qd_c, S) + jnp.einsum('bhts,bhsv->bhtv', qk_c, v_new)
        S = S * tl_c[..., None, None] + jnp.einsum('bhck,bhcv->bhkv', kt_c, v_new)
        return S, o

    S0 = jnp.zeros((B, H, DK, DV), jnp.float32)
    _, o = lax.scan(step, S0, (u, w, qk, q_dec, k_tail, tail))
    return from_chunks(o)


def hgrn2_group(p, lb, norm_w):
    B, T, _ = p.shape
    dt = p.dtype
    p32 = p.astype(jnp.float32)
    q = p32[..., :HG_QK].reshape(B, T, HG_HEADS, HG_DK)
    f_logit = p32[..., HG_QK:2 * HG_QK].reshape(B, T, HG_HEADS, HG_DK)
    i_in = p32[..., 2 * HG_QK:2 * HG_QK + HG_WIDTH].reshape(B, T, HG_HEADS, HG_DV)
    g_out = p[..., 2 * HG_QK + HG_WIDTH:].reshape(B, T, HG_HEADS, HG_DV)
    f = lb + (1.0 - lb) * jax.nn.sigmoid(f_logit)
    o = hgrn2_chunked(q, jnp.log(f), 1.0 - f, i_in).astype(dt)
    o = rmsnorm(o, norm_w) * jax.nn.silu(g_out)
    return o.reshape(B, T, HG_WIDTH)


def gdn_group(p, conv_w, a_log, dt_bias, norm_w):
    B, T, _ = p.shape
    dt = p.dtype
    qkv = jax.nn.silu(causal_conv(p[..., :GDN_CONV_CH], conv_w)).astype(jnp.float32)
    q = l2norm(qkv[..., :GDN_QK].reshape(B, T, GDN_HEADS, GDN_DK)) * (GDN_DK ** -0.5)
    k = l2norm(qkv[..., GDN_QK:2 * GDN_QK].reshape(B, T, GDN_HEADS, GDN_DK))
    v = qkv[..., 2 * GDN_QK:].reshape(B, T, GDN_HEADS, GDN_DV)
    off = GDN_CONV_CH
    g_out = p[..., off:off + GDN_WIDTH].reshape(B, T, GDN_HEADS, GDN_DV)
    a = p[..., off + GDN_WIDTH:off + GDN_WIDTH + GDN_HEADS].astype(jnp.float32)
    b = p[..., off + GDN_WIDTH + GDN_HEADS:].astype(jnp.float32)
    log_a = -jnp.exp(a_log.astype(jnp.float32)) * jax.nn.softplus(a + dt_bias.astype(jnp.float32))
    beta = jax.nn.sigmoid(b)
    o = gated_delta_chunked(q, k, v, log_a, beta).astype(dt)
    o = rmsnorm(o, norm_w) * jax.nn.silu(g_out)
    return o.reshape(B, T, GDN_WIDTH)


def setup_inputs(seed: int = 0) -> dict:
    key = jax.random.key(seed)
    ks = jax.random.split(key, 20)
    f32 = jnp.float32
    nrm = lambda k, s, sc: jax.random.normal(k, s, f32) * sc
    gain = lambda k, s: 1.0 + 0.05 * jax.random.normal(k, s, f32)
    dtv = jnp.exp(jax.random.uniform(ks[13], (DEPTH, GDN_HEADS), f32, np.log(1e-3), np.log(1e-1)))
    return {
        "x": nrm(ks[0], (BATCH, SEQ, D_MODEL), 1.0),
        "c": nrm(ks[1], (BATCH, D_MODEL), 1.0),
        "w_ada": nrm(ks[2], (DEPTH, D_MODEL, N_MOD * D_MODEL), 0.5 * D_MODEL ** -0.5),
        "b_ada": nrm(ks[3], (DEPTH, N_MOD * D_MODEL), 0.02),
        "pre_mix_norm": gain(ks[4], (DEPTH, D_MODEL)),
        "post_mix_norm": gain(ks[5], (DEPTH, D_MODEL)),
        "pre_ffn_norm": gain(ks[6], (DEPTH, D_MODEL)),
        "post_ffn_norm": gain(ks[7], (DEPTH, D_MODEL)),
        "w_in": nrm(ks[8], (DEPTH, D_MODEL, IN_COLS), D_MODEL ** -0.5),
        "hg_lb_logits": nrm(ks[9], (DEPTH + 1, HG_HEADS, HG_DK), 0.5),
        "hg_norm": gain(ks[10], (DEPTH, HG_DV)),
        "gdn_conv_w": nrm(ks[11], (DEPTH, CONV_K, GDN_CONV_CH), CONV_K ** -0.5),
        "gdn_a_log": jnp.log(jax.random.uniform(ks[12], (DEPTH, GDN_HEADS), f32, 1.0, 16.0)),
        "gdn_dt_bias": dtv + jnp.log(-jnp.expm1(-dtv)),
        "gdn_norm": gain(ks[14], (DEPTH, GDN_DV)),
        "w_out": nrm(ks[15], (DEPTH, D_MIX, D_MODEL), D_MIX ** -0.5),
        "w_ff1": nrm(ks[16], (DEPTH, D_MODEL, D_FF), D_MODEL ** -0.5),
        "w_ff2": nrm(ks[17], (DEPTH, D_FF, D_MODEL), D_FF ** -0.5),
    }


def reference(x, c, w_ada, b_ada, pre_mix_norm, post_mix_norm, pre_ffn_norm, post_ffn_norm,
              w_in, hg_lb_logits, hg_norm, gdn_conv_w, gdn_a_log, gdn_dt_bias, gdn_norm,
              w_out, w_ff1, w_ff2):
    lb_all = jnp.cumsum(jax.nn.softmax(hg_lb_logits.astype(jnp.float32), axis=0), axis=0)
    c_act = jax.nn.silu(c)
    for l in range(DEPTH):
        mod = c_act @ w_ada[l] + b_ada[l]
        sh_m, sc_m, gt_m, sh_f, sc_f, gt_f = jnp.split(mod[:, None, :], N_MOD, axis=-1)
        h = rmsnorm(x, pre_mix_norm[l]) * (1.0 + sc_m) + sh_m
        proj = h @ w_in[l]
        o_hg = hgrn2_group(proj[..., :HG_COLS], lb_all[l], hg_norm[l])
        o_gdn = gdn_group(proj[..., HG_COLS:], gdn_conv_w[l], gdn_a_log[l], gdn_dt_bias[l], gdn_norm[l])
        y = jnp.concatenate([o_hg, o_gdn], axis=-1) @ w_out[l]
        x = x + gt_m * rmsnorm(y, post_mix_norm[l])
        h = rmsnorm(x, pre_ffn_norm[l]) * (1.0 + sc_f) + sh_f
        y = jnp.square(jax.nn.relu(h @ w_ff1[l])) @ w_ff2[l]
        x = x + gt_f * rmsnorm(y, post_ffn_norm[l])
    return x
```

```python
import functools

import jax
import jax.numpy as jnp
from jax import lax
from jax.experimental import pallas as pl
from jax.experimental.pallas import tpu as pltpu

F32 = jnp.float32
BF16 = jnp.bfloat16
EPS = 1e-6

LANES = 128
HEAD_DIM = 128
CHUNK = 64
SUB = 16
CONV_K = 4
HALO = 8
VMEM_LIMIT = 56 * 1024 * 1024


def _cparams(sem):
    return pltpu.CompilerParams(dimension_semantics=sem, vmem_limit_bytes=VMEM_LIMIT)


def _dot(a, b):
    return jnp.dot(a.astype(BF16), b.astype(BF16), preferred_element_type=F32)


def _dot_nt(a, b):
    return lax.dot_general(a.astype(BF16), b.astype(BF16), (((1,), (1,)), ((), ())),
                           preferred_element_type=F32)


def _dot_tn(a, b):
    return lax.dot_general(a.astype(BF16), b.astype(BF16), (((0,), (0,)), ((), ())),
                           preferred_element_type=F32)


def _split3(x):
    hi = x.astype(BF16)
    r = x - hi.astype(F32)
    mid = r.astype(BF16)
    lo = (r - mid.astype(F32)).astype(BF16)
    return hi, mid, lo


def _dot_exact_lhs(m_bf16, x):
    hi, mid, lo = _split3(x)
    d = lambda p: jnp.dot(m_bf16, p, preferred_element_type=F32)
    return d(hi) + d(mid) + d(lo)


def _dot_exact_rhs(x, m_bf16):
    hi, mid, lo = _split3(x)
    d = lambda p: jnp.dot(p, m_bf16, preferred_element_type=F32)
    return d(hi) + d(mid) + d(lo)


def _sigmoid(x):
    return 1.0 / (1.0 + jnp.exp(-x))


def _silu(x):
    return x * _sigmoid(x)


def _iota2(shape, dim):
    return lax.broadcasted_iota(jnp.int32, shape, dim)


def _ada_kernel(c_ref, w_ref, b_ref, o_ref):
    c = c_ref[...]
    o_ref[...] = _dot(_silu(c), w_ref[...]) + b_ref[...]


def _ada(c_pad, w_ada, b_ada, bn=1024):
    rows, d = c_pad.shape
    n = w_ada.shape[1]
    return pl.pallas_call(
        _ada_kernel,
        grid=(n // bn,),
        in_specs=[pl.BlockSpec((rows, d), lambda j: (0, 0)),
                  pl.BlockSpec((d, bn), lambda j: (0, j)),
                  pl.BlockSpec((1, bn), lambda j: (0, j))],
        out_specs=pl.BlockSpec((rows, bn), lambda j: (0, j)),
        out_shape=jax.ShapeDtypeStruct((rows, n), F32),
        compiler_params=_cparams(("arbitrary",)),
        name="ada",
    )(c_pad, w_ada, b_ada)


def _inproj_kernel(x_ref, nw_ref, sc_ref, sh_ref, w_ref, wab_ref, o_ref, ab_ref, h_ref):
    @pl.when(pl.program_id(1) == 0)
    def _():
        x = x_ref[...]
        y = x * lax.rsqrt(jnp.mean(x * x, axis=-1, keepdims=True) + EPS)
        h = (y * nw_ref[...]) * (1.0 + sc_ref[0]) + sh_ref[0]
        hb = h.astype(BF16)
        h_ref[...] = hb
        ab_ref[...] = jnp.dot(hb, wab_ref[...], preferred_element_type=F32)

    o_ref[...] = jnp.dot(h_ref[...], w_ref[...], preferred_element_type=F32)


def _inproj(x2, nw, sc, sh, w_main, w_ab, seq, bm=1024, bn=1024):
    m, d = x2.shape
    n = w_main.shape[1]
    per_b = seq // bm
    return pl.pallas_call(
        _inproj_kernel,
        grid=(m // bm, n // bn),
        in_specs=[pl.BlockSpec((bm, d), lambda i, j: (i, 0)),
                  pl.BlockSpec((1, d), lambda i, j: (0, 0)),
                  pl.BlockSpec((1, 1, d), lambda i, j: (i // per_b, 0, 0)),
                  pl.BlockSpec((1, 1, d), lambda i, j: (i // per_b, 0, 0)),
                  pl.BlockSpec((d, bn), lambda i, j: (0, j)),
                  pl.BlockSpec((d, LANES), lambda i, j: (0, 0))],
        out_specs=[pl.BlockSpec((bm, bn), lambda i, j: (i, j)),
                   pl.BlockSpec((bm, LANES), lambda i, j: (i, 0))],
        out_shape=[jax.ShapeDtypeStruct((m, n), F32),
                   jax.ShapeDtypeStruct((m, LANES), F32)],
        scratch_shapes=[pltpu.VMEM((bm, d), BF16)],
        compiler_params=_cparams(("arbitrary", "arbitrary")),
        name="inproj",
    )(x2, nw, sc, sh, w_main, w_ab)


def _gated_norm(o, nw, g):
    y = o * lax.rsqrt(jnp.mean(o * o, axis=-1, keepdims=True) + EPS)
    return (y * nw) * _silu(g)


def _hgrn_kernel(lbl_ref, q_ref, f_ref, i_ref, g_ref, nw_ref, o_ref, st_ref, *, n_chunks):
    @pl.when(pl.program_id(2) == 0)
    def _():
        st_ref[...] = jnp.zeros_like(st_ref)

    l0 = lbl_ref[0]
    l1 = lbl_ref[1]
    mx = jnp.maximum(l0, l1)
    e0 = jnp.exp(l0 - mx)
    lb = e0 / (e0 + jnp.exp(l1 - mx))

    tri = (_iota2((CHUNK, CHUNK), 1) <= _iota2((CHUNK, CHUNK), 0)).astype(BF16)
    nw = nw_ref[...]
    st = st_ref[...]
    for c in range(n_chunks):
        rows = pl.ds(c * CHUNK, CHUNK)
        q = q_ref[rows, :]
        v = i_ref[rows, :]
        f = lb + (1.0 - lb) * _sigmoid(f_ref[rows, :])
        kk = 1.0 - f
        b = _dot_exact_lhs(tri, jnp.log(f))
        vb = v.astype(BF16)

        parts = []
        for blk in range(CHUNK // SUB):
            r0 = blk * SUB
            n = r0 + SUB
            p = b[r0 + SUB // 2:r0 + SUB // 2 + 1, :]
            qt = q[r0:n] * jnp.exp(b[r0:n] - p)
            kt = kk[0:n] * jnp.exp(p - b[0:n])
            a = _dot_nt(qt, kt)
            keep = _iota2((SUB, n), 1) <= _iota2((SUB, n), 0) + r0
            a = jnp.where(keep, a, 0.0)
            parts.append(jnp.dot(a.astype(BF16), vb[0:n], preferred_element_type=F32))
        o = jnp.concatenate(parts, axis=0)

        o = o + _dot_nt(q * jnp.exp(b), st)
        b_last = b[CHUNK - 1:CHUNK, :]
        k_tail = kk * jnp.exp(b_last - b)
        st = st * jnp.exp(b_last) + _dot_tn(v, k_tail)

        o_ref[rows, :] = _gated_norm(o, nw, g_ref[rows, :]).astype(o_ref.dtype)
    st_ref[...] = st


def _hgrn(proj, lb_logits4, norm_w, batch, seq, heads, col0, tb=512):
    per_b = seq // tb

    def col(group):
        return pl.BlockSpec((tb, HEAD_DIM),
                            lambda b, h, j, g=group: (b * per_b + j, col0 + g * heads + h))

    return pl.pallas_call(
        functools.partial(_hgrn_kernel, n_chunks=tb // CHUNK),
        grid=(batch, heads, per_b),
        in_specs=[pl.BlockSpec((lb_logits4.shape[0], None, 1, HEAD_DIM), lambda b, h, j: (0, h, 0, 0)),
                  col(0), col(1), col(2), col(3),
                  pl.BlockSpec((1, HEAD_DIM), lambda b, h, j: (0, 0))],
        out_specs=pl.BlockSpec((tb, HEAD_DIM), lambda b, h, j: (b * per_b + j, h)),
        out_shape=jax.ShapeDtypeStruct((batch * seq, heads * HEAD_DIM), BF16),
        scratch_shapes=[pltpu.VMEM((HEAD_DIM, HEAD_DIM), F32)],
        compiler_params=_cparams(("arbitrary", "arbitrary", "arbitrary")),
        name="hgrn",
    )(lb_logits4, proj, proj, proj, proj, norm_w)


def _softplus(x):
    return jnp.maximum(x, 0.0) + jnp.log(1.0 + jnp.exp(-jnp.abs(x)))


def _conv_silu(u, halo_ref, w):
    tb = u.shape[0]
    ext = jnp.concatenate([halo_ref[...], u], axis=0)
    halo_ref[...] = u[tb - HALO:, :]
    acc = w[CONV_K - 1:CONV_K, :] * u
    for j in range(CONV_K - 1):
        s = HALO - (CONV_K - 1) + j
        acc = acc + w[j:j + 1, :] * ext[s:s + tb, :]
    return _silu(acc)


def _l2norm(x):
    return x * lax.rsqrt(jnp.sum(x * x, axis=-1, keepdims=True) + EPS)


def _gdn_kernel(q_ref, k_ref, v_ref, g_ref, ab_ref, wq_ref, wk_ref, wv_ref, alog_ref, dtb_ref,
                nw_ref, o_ref, s_ref, hq_ref, hk_ref, hv_ref, qs_ref, ks_ref, vs_ref, la_ref, be_ref,
                *, n_chunks, heads):
    h = pl.program_id(1)

    @pl.when(pl.program_id(2) == 0)
    def _():
        s_ref[...] = jnp.zeros_like(s_ref)
        hq_ref[...] = jnp.zeros_like(hq_ref)
        hk_ref[...] = jnp.zeros_like(hk_ref)
        hv_ref[...] = jnp.zeros_like(hv_ref)

    qs_ref[...] = _l2norm(_conv_silu(q_ref[...], hq_ref, wq_ref[...])) * (HEAD_DIM ** -0.5)
    ks_ref[...] = _l2norm(_conv_silu(k_ref[...], hk_ref, wk_ref[...]))
    vs_ref[...] = _conv_silu(v_ref[...], hv_ref, wv_ref[...])

    lane = _iota2((LANES, LANES), 0)
    sel_a = (lane == h).astype(BF16)
    sel_b = (lane == h + heads).astype(BF16)
    ab = ab_ref[...]
    a_rep = _dot_exact_rhs(ab, sel_a)
    b_rep = _dot_exact_rhs(ab, sel_b)
    a_log = _dot_exact_rhs(alog_ref[...], sel_a)
    dtb = _dot_exact_rhs(dtb_ref[...], sel_a)
    la_ref[...] = -jnp.exp(a_log) * _softplus(a_rep + dtb)
    be_ref[...] = _sigmoid(b_rep)

    row = _iota2((CHUNK, CHUNK), 0)
    colm = _iota2((CHUNK, CHUNK), 1)
    tri = (colm <= row).astype(BF16)
    incl = colm <= row
    strict = colm < row
    sub_shift = SUB.bit_length() - 1
    same_blk = (row >> sub_shift) == (colm >> sub_shift)
    eye = (row == colm).astype(F32)
    nw = nw_ref[...]
    s = s_ref[...]
    for c in range(n_chunks):
        rows = pl.ds(c * CHUNK, CHUNK)
        q = qs_ref[rows, :]
        k = ks_ref[rows, :]
        v = vs_ref[rows, :]
        beta = be_ref[rows, :]
        g = _dot_exact_lhs(tri, la_ref[rows, :])
        g_sq = g[:, :CHUNK]
        g_row = jnp.sum(g_sq * eye, axis=0, keepdims=True)
        gamma = jnp.exp(jnp.where(incl, g_sq - g_row, -jnp.inf))
        beta_sq = beta[:, :CHUNK]

        kkt = _dot_nt(k, k)
        mm = jnp.where(strict, beta_sq * kkt * gamma, 0.0)
        nd = jnp.where(same_blk, -mm, 0.0)
        off = jnp.where(same_blk, 0.0, mm)
        dinv = eye + nd
        pw = nd
        for _ in range(3):
            pw = _dot(pw, pw)
            dinv = dinv + _dot(dinv, pw)
        fm = -_dot(dinv, off)
        f2 = _dot(fm, fm)
        ipf = eye + fm
        tinv = _dot(ipf + _dot(ipf, f2), dinv)

        eg = jnp.exp(g)
        rhs = jnp.concatenate([v * beta, k * (beta * eg)], axis=1)
        uw = _dot(tinv, rhs)
        u = uw[:, :HEAD_DIM]
        w = uw[:, HEAD_DIM:]

        qk = _dot_nt(q, k) * gamma
        g_last = g[CHUNK - 1:CHUNK, :]
        v_new = u - _dot(w, s)
        o = _dot(q * eg, s) + _dot(qk, v_new)
        s = s * jnp.exp(g_last) + _dot_tn(k * jnp.exp(g_last - g), v_new)

        o_ref[rows, :] = _gated_norm(o, nw, g_ref[rows, :]).astype(o_ref.dtype)
    s_ref[...] = s


def _gdn(proj, ab, conv_w, a_log_pad, dt_bias_pad, norm_w, batch, seq, heads, col0, tb=512):
    per_b = seq // tb

    def col(group):
        return pl.BlockSpec((tb, HEAD_DIM),
                            lambda b, h, j, g=group: (b * per_b + j, col0 + g * heads + h))

    def wcol(group):
        return pl.BlockSpec((CONV_K, HEAD_DIM), lambda b, h, j, g=group: (0, g * heads + h))

    small = pl.BlockSpec((1, LANES), lambda b, h, j: (0, 0))
    blk = lambda dt: pltpu.VMEM((tb, HEAD_DIM), dt)
    halo = pltpu.VMEM((HALO, HEAD_DIM), F32)
    return pl.pallas_call(
        functools.partial(_gdn_kernel, n_chunks=tb // CHUNK, heads=heads),
        grid=(batch, heads, per_b),
        in_specs=[col(0), col(1), col(2), col(3),
                  pl.BlockSpec((tb, LANES), lambda b, h, j: (b * per_b + j, 0)),
                  wcol(0), wcol(1), wcol(2), small, small, small],
        out_specs=pl.BlockSpec((tb, HEAD_DIM), lambda b, h, j: (b * per_b + j, h)),
        out_shape=jax.ShapeDtypeStruct((batch * seq, heads * HEAD_DIM), BF16),
        scratch_shapes=[pltpu.VMEM((HEAD_DIM, HEAD_DIM), F32), halo, halo, halo,
                        blk(F32), blk(F32), blk(F32), blk(F32), blk(F32)],
        compiler_params=_cparams(("arbitrary", "arbitrary", "arbitrary")),
        name="gdn",
    )(proj, proj, proj, proj, ab, conv_w, conv_w, conv_w, a_log_pad, dt_bias_pad, norm_w)


def _rms(y, w):
    return (y * lax.rsqrt(jnp.mean(y * y, axis=-1, keepdims=True) + EPS)) * w


def _outproj_kernel(oh_ref, og_ref, wh_ref, wg_ref, x_ref, gt_ref, pw_ref, fw_ref, sc_ref, sh_ref,
                    x1_ref, h2_ref):
    y = (jnp.dot(oh_ref[...], wh_ref[...], preferred_element_type=F32)
         + jnp.dot(og_ref[...], wg_ref[...], preferred_element_type=F32))
    x1 = x_ref[...] + gt_ref[0] * _rms(y, pw_ref[...])
    x1_ref[...] = x1
    h2_ref[...] = (_rms(x1, fw_ref[...]) * (1.0 + sc_ref[0]) + sh_ref[0]).astype(BF16)


def _outproj(o_hg, o_gdn, w_hg, w_gdn, x2, gt, post_w, ffn_w, sc, sh, seq, bm=512):
    m, d = x2.shape
    kh = o_hg.shape[1]
    per_b = seq // bm
    vec = pl.BlockSpec((1, d), lambda i: (0, 0))
    mod = pl.BlockSpec((1, 1, d), lambda i: (i // per_b, 0, 0))
    return pl.pallas_call(
        _outproj_kernel,
        grid=(m // bm,),
        in_specs=[pl.BlockSpec((bm, kh), lambda i: (i, 0)),
                  pl.BlockSpec((bm, kh), lambda i: (i, 0)),
                  pl.BlockSpec((kh, d), lambda i: (0, 0)),
                  pl.BlockSpec((kh, d), lambda i: (0, 0)),
                  pl.BlockSpec((bm, d), lambda i: (i, 0)),
                  mod, vec, vec, mod, mod],
        out_specs=[pl.BlockSpec((bm, d), lambda i: (i, 0)),
                   pl.BlockSpec((bm, d), lambda i: (i, 0))],
        out_shape=[jax.ShapeDtypeStruct((m, d), F32),
                   jax.ShapeDtypeStruct((m, d), BF16)],
        compiler_params=_cparams(("arbitrary",)),
        name="outproj",
    )(o_hg, o_gdn, w_hg, w_gdn, x2, gt, post_w, ffn_w, sc, sh)


def _ffn_kernel(h_ref, w1_ref, w2_ref, x1_ref, gt_ref, pw_ref, o_ref, acc_ref):
    f = pl.program_id(1)

    @pl.when(f == 0)
    def _():
        acc_ref[...] = jnp.zeros_like(acc_ref)

    a = jnp.maximum(jnp.dot(h_ref[...], w1_ref[...], preferred_element_type=F32), 0.0)
    acc_ref[...] += jnp.dot((a * a).astype(BF16), w2_ref[...], preferred_element_type=F32)

    @pl.when(f == pl.num_programs(1) - 1)
    def _():
        o_ref[...] = x1_ref[...] + gt_ref[0] * _rms(acc_ref[...], pw_ref[...])


def _ffn(h2, w1, w2, x1, gt, post_w, seq, bm=512, bf=1024):
    m, d = h2.shape
    dff = w1.shape[1]
    per_b = seq // bm
    return pl.pallas_call(
        _ffn_kernel,
        grid=(m // bm, dff // bf),
        in_specs=[pl.BlockSpec((bm, d), lambda i, f: (i, 0)),
                  pl.BlockSpec((d, bf), lambda i, f: (0, f)),
                  pl.BlockSpec((bf, d), lambda i, f: (f, 0)),
                  pl.BlockSpec((bm, d), lambda i, f: (i, 0)),
                  pl.BlockSpec((1, 1, d), lambda i, f: (i // per_b, 0, 0)),
                  pl.BlockSpec((1, d), lambda i, f: (0, 0))],
        out_specs=pl.BlockSpec((bm, d), lambda i, f: (i, 0)),
        out_shape=jax.ShapeDtypeStruct((m, d), F32),
        scratch_shapes=[pltpu.VMEM((bm, d), F32)],
        compiler_params=_cparams(("arbitrary", "arbitrary")),
        name="ffn",
    )(h2, w1, w2, x1, gt, post_w)


def _layer(x, mod, pre_mix_w, post_mix_w, pre_ffn_w, post_ffn_w, w_in, lb_logits, hg_norm_w,
           conv_w, a_log, dt_bias, gdn_norm_w, w_out, w_ff1, w_ff2):
    batch, seq, d = x.shape
    hg_heads = lb_logits.shape[1]
    gdn_heads = a_log.shape[0]
    hg_cols = 4 * hg_heads * HEAD_DIM
    n_main = hg_cols + 4 * gdn_heads * HEAD_DIM

    x2 = x.reshape(batch * seq, d)
    sh_m, sc_m, gt_m, sh_f, sc_f, gt_f = [mod[:batch, None, i * d:(i + 1) * d] for i in range(6)]
    row = lambda v: v.reshape(1, -1)
    pad_lanes = lambda v: jnp.pad(v.reshape(1, -1), ((0, 0), (0, LANES - v.shape[-1])))

    w_main = w_in[:, :n_main].astype(BF16)
    w_ab = jnp.pad(w_in[:, n_main:], ((0, 0), (0, LANES - (w_in.shape[1] - n_main)))).astype(BF16)
    proj, ab = _inproj(x2, row(pre_mix_w), sc_m, sh_m, w_main, w_ab, seq)

    o_hg = _hgrn(proj, lb_logits[:, :, None, :], row(hg_norm_w), batch, seq, hg_heads, 0)
    o_gdn = _gdn(proj, ab, conv_w, pad_lanes(a_log), pad_lanes(dt_bias), row(gdn_norm_w),
                 batch, seq, gdn_heads, hg_cols // HEAD_DIM)

    kh = hg_heads * HEAD_DIM
    w_out_b = w_out.astype(BF16)
    x1, h2 = _outproj(o_hg, o_gdn, w_out_b[:kh], w_out_b[kh:], x2, gt_m, row(post_mix_w),
                      row(pre_ffn_w), sc_f, sh_f, seq)
    out = _ffn(h2, w_ff1.astype(BF16), w_ff2.astype(BF16), x1, gt_f, row(post_ffn_w), seq)
    return out.reshape(batch, seq, d)


def kernel(x, c, w_ada, b_ada, pre_mix_norm, post_mix_norm, pre_ffn_norm, post_ffn_norm, w_in,
           hg_lb_logits, hg_norm, gdn_conv_w, gdn_a_log, gdn_dt_bias, gdn_norm, w_out, w_ff1, w_ff2):
    depth = w_ada.shape[0]
    assert depth == 1 and hg_lb_logits.shape[0] == 2, "single-layer configuration only"
    batch = c.shape[0]
    c_pad = jnp.pad(c, ((0, 8 - batch), (0, 0)))
    for l in range(depth):
        mod = _ada(c_pad, w_ada[l], b_ada[l][None, :])
        x = _layer(x, mod, pre_mix_norm[l], post_mix_norm[l], pre_ffn_norm[l], post_ffn_norm[l],
                   w_in[l], hg_lb_logits, hg_norm[l], gdn_conv_w[l], gdn_a_log[l], gdn_dt_bias[l],
                   gdn_norm[l], w_out[l], w_ff1[l], w_ff2[l])
    return x
```

```python
import functools

import jax
import jax.numpy as jnp
from jax import lax
from jax.experimental import pallas as pl
from jax.experimental.pallas import tpu as pltpu

F32 = jnp.float32
BF16 = jnp.bfloat16
EPS = 1e-6

LANES = 128
HEAD_DIM = 128
CHUNK = 64
SUB = 16
CONV_K = 4
HALO = 8
VMEM_LIMIT = 56 * 1024 * 1024


def _cparams(sem):
    return pltpu.CompilerParams(dimension_semantics=sem, vmem_limit_bytes=VMEM_LIMIT)


def _dot(a, b):
    return jnp.dot(a.astype(BF16), b.astype(BF16), preferred_element_type=F32)


def _dot_nt(a, b):
    return lax.dot_general(a.astype(BF16), b.astype(BF16), (((1,), (1,)), ((), ())),
                           preferred_element_type=F32)


def _dot_tn(a, b):
    return lax.dot_general(a.astype(BF16), b.astype(BF16), (((0,), (0,)), ((), ())),
                           preferred_element_type=F32)


def _split3(x):
    hi = x.astype(BF16)
    r = x - hi.astype(F32)
    mid = r.astype(BF16)
    lo = (r - mid.astype(F32)).astype(BF16)
    return hi, mid, lo


def _dot_exact_lhs(m_bf16, x):
    hi, mid, lo = _split3(x)
    d = lambda p: jnp.dot(m_bf16, p, preferred_element_type=F32)
    return d(hi) + d(mid) + d(lo)


def _dot_exact_rhs(x, m_bf16):
    hi, mid, lo = _split3(x)
    d = lambda p: jnp.dot(p, m_bf16, preferred_element_type=F32)
    return d(hi) + d(mid) + d(lo)


def _sigmoid(x):
    return 1.0 / (1.0 + jnp.exp(-x))


def _silu(x):
    return x * _sigmoid(x)


def _iota2(shape, dim):
    return lax.broadcasted_iota(jnp.int32, shape, dim)


def _ada_kernel(c_ref, w_ref, b_ref, o_ref):
    c = c_ref[...]
    o_ref[...] = _dot(_silu(c), w_ref[...]) + b_ref[...]


def _ada(c_pad, w_ada, b_ada, bn=1024):
    rows, d = c_pad.shape
    n = w_ada.shape[1]
    return pl.pallas_call(
        _ada_kernel,
        grid=(n // bn,),
        in_specs=[pl.BlockSpec((rows, d), lambda j: (0, 0)),
                  pl.BlockSpec((d, bn), lambda j: (0, j)),
                  pl.BlockSpec((1, bn), lambda j: (0, j))],
        out_specs=pl.BlockSpec((rows, bn), lambda j: (0, j)),
        out_shape=jax.ShapeDtypeStruct((rows, n), F32),
        compiler_params=_cparams(("arbitrary",)),
        name="ada",
    )(c_pad, w_ada, b_ada)


def _inproj_kernel(x_ref, nw_ref, sc_ref, sh_ref, w_ref, wab_ref, o_ref, ab_ref, h_ref):
    @pl.when(pl.program_id(1) == 0)
    def _():
        x = x_ref[...]
        y = x * lax.rsqrt(jnp.mean(x * x, axis=-1, keepdims=True) + EPS)
        h = (y * nw_ref[...]) * (1.0 + sc_ref[0]) + sh_ref[0]
        hb = h.astype(BF16)
        h_ref[...] = hb
        ab_ref[...] = jnp.dot(hb, wab_ref[...], preferred_element_type=F32)

    o_ref[...] = jnp.dot(h_ref[...], w_ref[...], preferred_element_type=F32)


def _inproj(x2, nw, sc, sh, w_main, w_ab, seq, bm=1024, bn=1024):
    m, d = x2.shape
    n = w_main.shape[1]
    per_b = seq // bm
    return pl.pallas_call(
        _inproj_kernel,
        grid=(m // bm, n // bn),
        in_specs=[pl.BlockSpec((bm, d), lambda i, j: (i, 0)),
                  pl.BlockSpec((1, d), lambda i, j: (0, 0)),
                  pl.BlockSpec((1, 1, d), lambda i, j: (i // per_b, 0, 0)),
                  pl.BlockSpec((1, 1, d), lambda i, j: (i // per_b, 0, 0)),
                  pl.BlockSpec((d, bn), lambda i, j: (0, j)),
                  pl.BlockSpec((d, LANES), lambda i, j: (0, 0))],
        out_specs=[pl.BlockSpec((bm, bn), lambda i, j: (i, j)),
                   pl.BlockSpec((bm, LANES), lambda i, j: (i, 0))],
        out_shape=[jax.ShapeDtypeStruct((m, n), F32),
                   jax.ShapeDtypeStruct((m, LANES), F32)],
        scratch_shapes=[pltpu.VMEM((bm, d), BF16)],
        compiler_params=_cparams(("arbitrary", "arbitrary")),
        name="inproj",
    )(x2, nw, sc, sh, w_main, w_ab)


def _gated_norm(o, nw, g):
    y = o * lax.rsqrt(jnp.mean(o * o, axis=-1, keepdims=True) + EPS)
    return (y * nw) * _silu(g)


HGRN_ROWS = 2 * CHUNK


def _hgrn_kernel(lbl_ref, q_ref, f_ref, i_ref, g_ref, nw_ref, o_ref, st_ref, oi_ref, qd_ref, kv_ref,
                 dec_ref, *, heads, tb):
    n_chunks = tb // CHUNK
    cpi = HGRN_ROWS // CHUNK

    @pl.when(pl.program_id(1) == 0)
    def _():
        st_ref[...] = jnp.zeros_like(st_ref)

    l0 = lbl_ref[0:1, :]
    l1 = lbl_ref[1:2, :]
    mx = jnp.maximum(l0, l1)
    e0 = jnp.exp(l0 - mx)
    lb = e0 / (e0 + jnp.exp(l1 - mx))

    tri = (_iota2((CHUNK, CHUNK), 1) <= _iota2((CHUNK, CHUNK), 0)).astype(BF16)

    def phase1(i, carry):
        rows = pl.ds(pl.multiple_of(i * HGRN_ROWS, HGRN_ROWS), HGRN_ROWS)
        f = lb + (1.0 - lb) * _sigmoid(f_ref[rows, :])
        logf = jnp.log(f)
        bc = jnp.concatenate([_dot_exact_lhs(tri, logf[c * CHUNK:(c + 1) * CHUNK]) for c in range(cpi)],
                             axis=0)
        q3 = _problems(q_ref[rows, :], heads, cpi)
        k3 = _problems(1.0 - f, heads, cpi)
        b3 = _problems(bc, heads, cpi)
        v3 = _problems(i_ref[rows, :], heads, cpi).astype(BF16)

        parts = []
        for blk in range(CHUNK // SUB):
            r0 = blk * SUB
            n = r0 + SUB
            p = b3[:, r0 + SUB // 2:r0 + SUB // 2 + 1, :]
            qt = q3[:, r0:n] * jnp.exp(b3[:, r0:n] - p)
            kt = k3[:, 0:n] * jnp.exp(p - b3[:, 0:n])
            a = _bmm_nt(qt, kt)
            keep = _iota2((SUB, n), 1) <= _iota2((SUB, n), 0) + r0
            parts.append(_bmm(jnp.where(keep, a, 0.0), v3[:, 0:n]))
        b_last = b3[:, CHUNK - 1:CHUNK, :]
        ps = pl.ds(pl.multiple_of(i * (cpi * heads), cpi * heads), cpi * heads)
        oi_ref[ps] = jnp.concatenate(parts, axis=1)
        qd_ref[ps] = (q3 * jnp.exp(b3)).astype(BF16)
        kv_ref[ps] = _bmm_tn(v3, k3 * jnp.exp(b_last - b3))
        dec_ref[ps] = jnp.exp(b_last)
        return carry

    lax.fori_loop(0, n_chunks // cpi, phase1, 0)

    nw = nw_ref[...]

    def phase2(c, carry):
        ps = pl.ds(pl.multiple_of(c * heads, heads), heads)
        rows = pl.ds(pl.multiple_of(c * CHUNK, CHUNK), CHUNK)
        st = st_ref[...]
        o = oi_ref[ps] + _bmm_nt(qd_ref[ps], st)
        st_ref[...] = st * dec_ref[ps] + kv_ref[ps]
        for h in range(heads):
            cols = slice(h * HEAD_DIM, (h + 1) * HEAD_DIM)
            o_ref[rows, cols] = _gated_norm(o[h], nw, g_ref[rows, cols]).astype(o_ref.dtype)
        return carry

    lax.fori_loop(0, n_chunks, phase2, 0)


def _hgrn(proj, lb_logits2, norm_w, batch, seq, heads, col0, tb=512):
    per_b = seq // tb
    width = heads * HEAD_DIM
    cb0 = col0 // heads

    def col(group):
        return pl.BlockSpec((tb, width), lambda b, j, g=group: (b * per_b + j, cb0 + g))

    n_prob = (tb // CHUNK) * heads
    return pl.pallas_call(
        functools.partial(_hgrn_kernel, heads=heads, tb=tb),
        grid=(batch, per_b),
        in_specs=[pl.BlockSpec(lb_logits2.shape, lambda b, j: (0, 0)),
                  col(0), col(1), col(2), col(3),
                  pl.BlockSpec((1, HEAD_DIM), lambda b, j: (0, 0))],
        out_specs=pl.BlockSpec((tb, width), lambda b, j: (b * per_b + j, 0)),
        out_shape=jax.ShapeDtypeStruct((batch * seq, width), BF16),
        scratch_shapes=[pltpu.VMEM((heads, HEAD_DIM, HEAD_DIM), F32),
                        pltpu.VMEM((n_prob, CHUNK, HEAD_DIM), F32),
                        pltpu.VMEM((n_prob, CHUNK, HEAD_DIM), BF16),
                        pltpu.VMEM((n_prob, HEAD_DIM, HEAD_DIM), F32),
                        pltpu.VMEM((n_prob, 1, HEAD_DIM), F32)],
        compiler_params=_cparams(("arbitrary", "arbitrary")),
        name="hgrn",
    )(lb_logits2, proj, proj, proj, proj, norm_w)


def _softplus(x):
    return jnp.maximum(x, 0.0) + jnp.log(1.0 + jnp.exp(-jnp.abs(x)))


def _l2norm(x):
    return x * lax.rsqrt(jnp.sum(x * x, axis=-1, keepdims=True) + EPS)


def _bmm(a, b):
    return lax.dot_general(a.astype(BF16), b.astype(BF16), (((2,), (1,)), ((0,), (0,))),
                           preferred_element_type=F32)


def _bmm_nt(a, b):
    return lax.dot_general(a.astype(BF16), b.astype(BF16), (((2,), (2,)), ((0,), (0,))),
                           preferred_element_type=F32)


def _bmm_tn(a, b):
    return lax.dot_general(a.astype(BF16), b.astype(BF16), (((1,), (1,)), ((0,), (0,))),
                           preferred_element_type=F32)


def _conv_silu_tile(x_ref, halo_ref, w, r0, rb):
    cur = x_ref[pl.ds(r0, rb), :]
    prev = x_ref[pl.ds(pl.multiple_of(jnp.maximum(r0 - HALO, 0), HALO), HALO), :]
    prev = jnp.where(r0 == 0, halo_ref[...], prev)
    ext = jnp.concatenate([prev, cur], axis=0)
    acc = w[CONV_K - 1:CONV_K, :] * cur
    for j in range(CONV_K - 1):
        s = HALO - (CONV_K - 1) + j
        acc = acc + w[j:j + 1, :] * ext[s:s + rb, :]
    return _silu(acc)


def _problems(x, heads, chunks):
    return jnp.stack([x[c * CHUNK:(c + 1) * CHUNK, h * HEAD_DIM:(h + 1) * HEAD_DIM]
                      for c in range(chunks) for h in range(heads)], axis=0)


GDN_ROWS = 2 * CHUNK


def _gdn_kernel(q_ref, k_ref, v_ref, g_ref, ab_ref, wq_ref, wk_ref, wv_ref, alog_ref, dtb_ref,
                nw_ref, o_ref, s_ref, hq_ref, hk_ref, hv_ref, u_ref, w_ref, qd_ref, kt_ref, qk_ref,
                tail_ref, *, heads, tb):
    n_chunks = tb // CHUNK
    cpi = GDN_ROWS // CHUNK

    @pl.when(pl.program_id(1) == 0)
    def _():
        s_ref[...] = jnp.zeros_like(s_ref)
        hq_ref[...] = jnp.zeros_like(hq_ref)
        hk_ref[...] = jnp.zeros_like(hk_ref)
        hv_ref[...] = jnp.zeros_like(hv_ref)

    width = heads * HEAD_DIM
    row = _iota2((CHUNK, CHUNK), 0)
    colm = _iota2((CHUNK, CHUNK), 1)
    tri = (colm <= row).astype(BF16)
    incl = colm <= row
    strict = colm < row
    sub_shift = SUB.bit_length() - 1
    same_blk = (row >> sub_shift) == (colm >> sub_shift)
    eye = (row == colm).astype(F32)
    src = _iota2((LANES, width), 0)
    dst_head = _iota2((LANES, width), 1) >> (HEAD_DIM.bit_length() - 1)
    sel_a = (src == dst_head).astype(BF16)
    sel_b = (src == dst_head + heads).astype(BF16)

    def phase1(i, carry):
        r0 = pl.multiple_of(i * GDN_ROWS, GDN_ROWS)
        qa = _conv_silu_tile(q_ref, hq_ref, wq_ref[...], r0, GDN_ROWS)
        ka = _conv_silu_tile(k_ref, hk_ref, wk_ref[...], r0, GDN_ROWS)
        va = _conv_silu_tile(v_ref, hv_ref, wv_ref[...], r0, GDN_ROWS)
        qn = jnp.concatenate([_l2norm(qa[:, h * HEAD_DIM:(h + 1) * HEAD_DIM]) for h in range(heads)],
                             axis=1) * (HEAD_DIM ** -0.5)
        kn = jnp.concatenate([_l2norm(ka[:, h * HEAD_DIM:(h + 1) * HEAD_DIM]) for h in range(heads)],
                             axis=1)

        ab = ab_ref[pl.ds(r0, GDN_ROWS), :]
        la = -jnp.exp(alog_ref[...]) * _softplus(ab + dtb_ref[...])
        be = _sigmoid(ab)
        gc = jnp.concatenate([_dot_exact_lhs(tri, la[c * CHUNK:(c + 1) * CHUNK]) for c in range(cpi)],
                             axis=0)
        g_rep = _dot_exact_rhs(gc, sel_a)
        be_rep = _dot_exact_rhs(be, sel_b)

        q3 = _problems(qn, heads, cpi)
        k3 = _problems(kn, heads, cpi)
        v3 = _problems(va, heads, cpi)
        g3 = _problems(g_rep, heads, cpi)
        b3 = _problems(be_rep, heads, cpi)

        g_sq = g3[:, :, :CHUNK]
        g_row = jnp.sum(g_sq * eye, axis=1, keepdims=True)
        gamma = jnp.exp(jnp.where(incl, g_sq - g_row, -jnp.inf))
        kb = k3.astype(BF16)
        mm = jnp.where(strict, b3[:, :, :CHUNK] * _bmm_nt(kb, kb) * gamma, 0.0)
        nd = jnp.where(same_blk, -mm, 0.0)
        off = jnp.where(same_blk, 0.0, mm)
        dinv = eye + nd
        pw = nd
        for _ in range(3):
            pw = _bmm(pw, pw)
            dinv = dinv + _bmm(dinv, pw)
        fm = -_bmm(dinv, off)
        ipf = eye + fm
        tinv = _bmm(ipf + _bmm(ipf, _bmm(fm, fm)), dinv)

        eg = jnp.exp(g3)
        uw = _bmm(tinv, jnp.concatenate([v3 * b3, k3 * (b3 * eg)], axis=2))
        g_last = g3[:, CHUNK - 1:CHUNK, :]
        p0 = pl.multiple_of(i * (cpi * heads), cpi * heads)
        ps = pl.ds(p0, cpi * heads)
        u_ref[ps] = uw[:, :, :HEAD_DIM]
        w_ref[ps] = uw[:, :, HEAD_DIM:].astype(BF16)
        qd_ref[ps] = (q3 * eg).astype(BF16)
        kt_ref[ps] = (k3 * jnp.exp(g_last - g3)).astype(BF16)
        qk_ref[ps] = (_bmm_nt(q3, kb) * gamma).astype(BF16)
        tail_ref[ps] = jnp.exp(g_last)
        return carry

    lax.fori_loop(0, n_chunks // cpi, phase1, 0)
    hq_ref[...] = q_ref[tb - HALO:tb, :]
    hk_ref[...] = k_ref[tb - HALO:tb, :]
    hv_ref[...] = v_ref[tb - HALO:tb, :]

    nw = nw_ref[...]

    def phase2(c, carry):
        ps = pl.ds(pl.multiple_of(c * heads, heads), heads)
        rows = pl.ds(pl.multiple_of(c * CHUNK, CHUNK), CHUNK)
        s = s_ref[...]
        sb = s.astype(BF16)
        v_new = u_ref[ps] - _bmm(w_ref[ps], sb)
        vb = v_new.astype(BF16)
        o = _bmm(qd_ref[ps], sb) + _bmm(qk_ref[ps], vb)
        s_ref[...] = s * tail_ref[ps] + _bmm_tn(kt_ref[ps], vb)
        for h in range(heads):
            cols = slice(h * HEAD_DIM, (h + 1) * HEAD_DIM)
            o_ref[rows, cols] = _gated_norm(o[h], nw, g_ref[rows, cols]).astype(o_ref.dtype)
        return carry

    lax.fori_loop(0, n_chunks, phase2, 0)


def _gdn(proj, ab, conv_w, a_log_pad, dt_bias_pad, norm_w, batch, seq, heads, col0, tb=512):
    per_b = seq // tb
    width = heads * HEAD_DIM
    cb0 = col0 // heads

    def col(group):
        return pl.BlockSpec((tb, width), lambda b, j, g=group: (b * per_b + j, cb0 + g))

    def wcol(group):
        return pl.BlockSpec((CONV_K, width), lambda b, j, g=group: (0, g))

    small = pl.BlockSpec((1, LANES), lambda b, j: (0, 0))
    n_prob = (tb // CHUNK) * heads
    halo = pltpu.VMEM((HALO, width), F32)
    prob = lambda n, dt: pltpu.VMEM((n_prob, CHUNK, n), dt)
    return pl.pallas_call(
        functools.partial(_gdn_kernel, heads=heads, tb=tb),
        grid=(batch, per_b),
        in_specs=[col(0), col(1), col(2), col(3),
                  pl.BlockSpec((tb, LANES), lambda b, j: (b * per_b + j, 0)),
                  wcol(0), wcol(1), wcol(2), small, small, small],
        out_specs=pl.BlockSpec((tb, width), lambda b, j: (b * per_b + j, 0)),
        out_shape=jax.ShapeDtypeStruct((batch * seq, width), BF16),
        scratch_shapes=[pltpu.VMEM((heads, HEAD_DIM, HEAD_DIM), F32), halo, halo, halo,
                        prob(HEAD_DIM, F32), prob(HEAD_DIM, BF16), prob(HEAD_DIM, BF16),
                        prob(HEAD_DIM, BF16), prob(CHUNK, BF16),
                        pltpu.VMEM((n_prob, 1, HEAD_DIM), F32)],
        compiler_params=_cparams(("arbitrary", "arbitrary")),
        name="gdn",
    )(proj, proj, proj, proj, ab, conv_w, conv_w, conv_w, a_log_pad, dt_bias_pad, norm_w)


def _rms(y, w):
    return (y * lax.rsqrt(jnp.mean(y * y, axis=-1, keepdims=True) + EPS)) * w


def _outproj_kernel(oh_ref, og_ref, wh_ref, wg_ref, x_ref, gt_ref, pw_ref, fw_ref, sc_ref, sh_ref,
                    x1_ref, h2_ref):
    y = (jnp.dot(oh_ref[...], wh_ref[...], preferred_element_type=F32)
         + jnp.dot(og_ref[...], wg_ref[...], preferred_element_type=F32))
    x1 = x_ref[...] + gt_ref[0] * _rms(y, pw_ref[...])
    x1_ref[...] = x1
    h2_ref[...] = (_rms(x1, fw_ref[...]) * (1.0 + sc_ref[0]) + sh_ref[0]).astype(BF16)


def _outproj(o_hg, o_gdn, w_hg, w_gdn, x2, gt, post_w, ffn_w, sc, sh, seq, bm=512):
    m, d = x2.shape
    kh = o_hg.shape[1]
    per_b = seq // bm
    vec = pl.BlockSpec((1, d), lambda i: (0, 0))
    mod = pl.BlockSpec((1, 1, d), lambda i: (i // per_b, 0, 0))
    return pl.pallas_call(
        _outproj_kernel,
        grid=(m // bm,),
        in_specs=[pl.BlockSpec((bm, kh), lambda i: (i, 0)),
                  pl.BlockSpec((bm, kh), lambda i: (i, 0)),
                  pl.BlockSpec((kh, d), lambda i: (0, 0)),
                  pl.BlockSpec((kh, d), lambda i: (0, 0)),
                  pl.BlockSpec((bm, d), lambda i: (i, 0)),
                  mod, vec, vec, mod, mod],
        out_specs=[pl.BlockSpec((bm, d), lambda i: (i, 0)),
                   pl.BlockSpec((bm, d), lambda i: (i, 0))],
        out_shape=[jax.ShapeDtypeStruct((m, d), F32),
                   jax.ShapeDtypeStruct((m, d), BF16)],
        compiler_params=_cparams(("arbitrary",)),
        name="outproj",
    )(o_hg, o_gdn, w_hg, w_gdn, x2, gt, post_w, ffn_w, sc, sh)


def _ffn_kernel(h_ref, w1_ref, w2_ref, x1_ref, gt_ref, pw_ref, o_ref, acc_ref):
    f = pl.program_id(1)

    @pl.when(f == 0)
    def _():
        acc_ref[...] = jnp.zeros_like(acc_ref)

    a = jnp.maximum(jnp.dot(h_ref[...], w1_ref[...], preferred_element_type=F32), 0.0)
    acc_ref[...] += jnp.dot((a * a).astype(BF16), w2_ref[...], preferred_element_type=F32)

    @pl.when(f == pl.num_programs(1) - 1)
    def _():
        o_ref[...] = x1_ref[...] + gt_ref[0] * _rms(acc_ref[...], pw_ref[...])


def _ffn(h2, w1, w2, x1, gt, post_w, seq, bm=512, bf=1024):
    m, d = h2.shape
    dff = w1.shape[1]
    per_b = seq // bm
    return pl.pallas_call(
        _ffn_kernel,
        grid=(m // bm, dff // bf),
        in_specs=[pl.BlockSpec((bm, d), lambda i, f: (i, 0)),
                  pl.BlockSpec((d, bf), lambda i, f: (0, f)),
                  pl.BlockSpec((bf, d), lambda i, f: (f, 0)),
                  pl.BlockSpec((bm, d), lambda i, f: (i, 0)),
                  pl.BlockSpec((1, 1, d), lambda i, f: (i // per_b, 0, 0)),
                  pl.BlockSpec((1, d), lambda i, f: (0, 0))],
        out_specs=pl.BlockSpec((bm, d), lambda i, f: (i, 0)),
        out_shape=jax.ShapeDtypeStruct((m, d), F32),
        scratch_shapes=[pltpu.VMEM((bm, d), F32)],
        compiler_params=_cparams(("arbitrary", "arbitrary")),
        name="ffn",
    )(h2, w1, w2, x1, gt, post_w)


def _layer(x, mod, pre_mix_w, post_mix_w, pre_ffn_w, post_ffn_w, w_in, lb_logits, hg_norm_w,
           conv_w, a_log, dt_bias, gdn_norm_w, w_out, w_ff1, w_ff2):
    batch, seq, d = x.shape
    hg_heads = lb_logits.shape[1]
    gdn_heads = a_log.shape[0]
    hg_cols = 4 * hg_heads * HEAD_DIM
    n_main = hg_cols + 4 * gdn_heads * HEAD_DIM

    x2 = x.reshape(batch * seq, d)
    sh_m, sc_m, gt_m, sh_f, sc_f, gt_f = [mod[:batch, None, i * d:(i + 1) * d] for i in range(6)]
    row = lambda v: v.reshape(1, -1)
    pad_lanes = lambda v: jnp.pad(v.reshape(1, -1), ((0, 0), (0, LANES - v.shape[-1])))

    w_main = w_in[:, :n_main].astype(BF16)
    w_ab = jnp.pad(w_in[:, n_main:], ((0, 0), (0, LANES - (w_in.shape[1] - n_main)))).astype(BF16)
    proj, ab = _inproj(x2, row(pre_mix_w), sc_m, sh_m, w_main, w_ab, seq)

    o_hg = _hgrn(proj, lb_logits.reshape(lb_logits.shape[0], -1), row(hg_norm_w), batch, seq,
                 hg_heads, 0)
    o_gdn = _gdn(proj, ab, conv_w, pad_lanes(a_log), pad_lanes(dt_bias), row(gdn_norm_w),
                 batch, seq, gdn_heads, hg_cols // HEAD_DIM)

    kh = hg_heads * HEAD_DIM
    w_out_b = w_out.astype(BF16)
    x1, h2 = _outproj(o_hg, o_gdn, w_out_b[:kh], w_out_b[kh:], x2, gt_m, row(post_mix_w),
                      row(pre_ffn_w), sc_f, sh_f, seq)
    out = _ffn(h2, w_ff1.astype(BF16), w_ff2.astype(BF16), x1, gt_f, row(post_ffn_w), seq)
    return out.reshape(batch, seq, d)


def kernel(x, c, w_ada, b_ada, pre_mix_norm, post_mix_norm, pre_ffn_norm, post_ffn_norm, w_in,
           hg_lb_logits, hg_norm, gdn_conv_w, gdn_a_log, gdn_dt_bias, gdn_norm, w_out, w_ff1, w_ff2):
    depth = w_ada.shape[0]
    assert depth == 1 and hg_lb_logits.shape[0] == 2, "single-layer configuration only"
    batch = c.shape[0]
    c_pad = jnp.pad(c, ((0, 8 - batch), (0, 0)))
    for l in range(depth):
        mod = _ada(c_pad, w_ada[l], b_ada[l][None, :])
        x = _layer(x, mod, pre_mix_norm[l], post_mix_norm[l], pre_ffn_norm[l], post_ffn_norm[l],
                   w_in[l], hg_lb_logits, hg_norm[l], gdn_conv_w[l], gdn_a_log[l], gdn_dt_bias[l],
                   gdn_norm[l], w_out[l], w_ff1[l], w_ff2[l])
    return x
```

```python
import functools

import jax
import jax.numpy as jnp
from jax import lax
from jax.experimental import pallas as pl
from jax.experimental.pallas import tpu as pltpu

F32 = jnp.float32
BF16 = jnp.bfloat16
EPS = 1e-6

LANES = 128
HEAD_DIM = 128
CHUNK = 64
SUB = 16
CONV_K = 4
HALO = 8
VMEM_LIMIT = 56 * 1024 * 1024


def _cparams(sem):
    return pltpu.CompilerParams(dimension_semantics=sem, vmem_limit_bytes=VMEM_LIMIT)


def _dot(a, b):
    return jnp.dot(a.astype(BF16), b.astype(BF16), preferred_element_type=F32)


def _dot_nt(a, b):
    return lax.dot_general(a.astype(BF16), b.astype(BF16), (((1,), (1,)), ((), ())),
                           preferred_element_type=F32)


def _dot_tn(a, b):
    return lax.dot_general(a.astype(BF16), b.astype(BF16), (((0,), (0,)), ((), ())),
                           preferred_element_type=F32)


def _split3(x):
    hi = x.astype(BF16)
    r = x - hi.astype(F32)
    mid = r.astype(BF16)
    lo = (r - mid.astype(F32)).astype(BF16)
    return hi, mid, lo


def _dot_exact_lhs(m_bf16, x):
    hi, mid, lo = _split3(x)
    d = lambda p: jnp.dot(m_bf16, p, preferred_element_type=F32)
    return d(hi) + d(mid) + d(lo)


def _dot_exact_rhs(x, m_bf16):
    hi, mid, lo = _split3(x)
    d = lambda p: jnp.dot(p, m_bf16, preferred_element_type=F32)
    return d(hi) + d(mid) + d(lo)


def _sigmoid(x):
    return 1.0 / (1.0 + jnp.exp(-x))


def _silu(x):
    return x * _sigmoid(x)


def _iota2(shape, dim):
    return lax.broadcasted_iota(jnp.int32, shape, dim)


def _ada_kernel(c_ref, w_ref, b_ref, o_ref):
    c = c_ref[...]
    o_ref[...] = _dot(_silu(c), w_ref[...]) + b_ref[...]


def _ada(c_pad, w_ada, b_ada, bn=1024):
    rows, d = c_pad.shape
    n = w_ada.shape[1]
    return pl.pallas_call(
        _ada_kernel,
        grid=(n // bn,),
        in_specs=[pl.BlockSpec((rows, d), lambda j: (0, 0)),
                  pl.BlockSpec((d, bn), lambda j: (0, j)),
                  pl.BlockSpec((1, bn), lambda j: (0, j))],
        out_specs=pl.BlockSpec((rows, bn), lambda j: (0, j)),
        out_shape=jax.ShapeDtypeStruct((rows, n), F32),
        compiler_params=_cparams(("arbitrary",)),
        name="ada",
    )(c_pad, w_ada, b_ada)


def _inproj_kernel(x_ref, nw_ref, sc_ref, sh_ref, w_ref, wab_ref, o_ref, ab_ref, h_ref):
    @pl.when(pl.program_id(1) == 0)
    def _():
        x = x_ref[...]
        y = x * lax.rsqrt(jnp.mean(x * x, axis=-1, keepdims=True) + EPS)
        h = (y * nw_ref[...]) * (1.0 + sc_ref[0]) + sh_ref[0]
        hb = h.astype(BF16)
        h_ref[...] = hb
        ab_ref[...] = jnp.dot(hb, wab_ref[...], preferred_element_type=F32)

    o_ref[...] = jnp.dot(h_ref[...], w_ref[...], preferred_element_type=F32)


def _inproj(x2, nw, sc, sh, w_main, w_ab, n, seq, bm=1024, bn=1024):
    m, d = x2.shape
    per_b = seq // bm
    return pl.pallas_call(
        _inproj_kernel,
        grid=(m // bm, n // bn),
        in_specs=[pl.BlockSpec((bm, d), lambda i, j: (i, 0)),
                  pl.BlockSpec((1, d), lambda i, j: (0, 0)),
                  pl.BlockSpec((1, 1, d), lambda i, j: (i // per_b, 0, 0)),
                  pl.BlockSpec((1, 1, d), lambda i, j: (i // per_b, 0, 0)),
                  pl.BlockSpec((d, bn), lambda i, j: (0, j)),
                  pl.BlockSpec((d, LANES), lambda i, j: (0, 0))],
        out_specs=[pl.BlockSpec((bm, bn), lambda i, j: (i, j)),
                   pl.BlockSpec((bm, LANES), lambda i, j: (i, 0))],
        out_shape=[jax.ShapeDtypeStruct((m, n), F32),
                   jax.ShapeDtypeStruct((m, LANES), F32)],
        scratch_shapes=[pltpu.VMEM((bm, d), BF16)],
        compiler_params=_cparams(("arbitrary", "arbitrary")),
        name="inproj",
    )(x2, nw, sc, sh, w_main, w_ab)


def _gated_norm(o, nw, g):
    y = o * lax.rsqrt(jnp.mean(o * o, axis=-1, keepdims=True) + EPS)
    return (y * nw) * _silu(g)


HGRN_ROWS = 2 * CHUNK


def _hgrn_kernel(lbl_ref, q_ref, f_ref, i_ref, g_ref, nw_ref, o_ref, st_ref, oi_ref, qd_ref, kv_ref,
                 dec_ref, *, heads, tb):
    n_chunks = tb // CHUNK
    cpi = HGRN_ROWS // CHUNK

    @pl.when(pl.program_id(1) == 0)
    def _():
        st_ref[...] = jnp.zeros_like(st_ref)

    l0 = lbl_ref[0:1, :]
    l1 = lbl_ref[1:2, :]
    mx = jnp.maximum(l0, l1)
    e0 = jnp.exp(l0 - mx)
    lb = e0 / (e0 + jnp.exp(l1 - mx))

    tri = (_iota2((CHUNK, CHUNK), 1) <= _iota2((CHUNK, CHUNK), 0)).astype(BF16)

    def phase1(i, carry):
        rows = pl.ds(pl.multiple_of(i * HGRN_ROWS, HGRN_ROWS), HGRN_ROWS)
        f = lb + (1.0 - lb) * _sigmoid(f_ref[rows, :])
        logf = jnp.log(f)
        bc = jnp.concatenate([_dot_exact_lhs(tri, logf[c * CHUNK:(c + 1) * CHUNK]) for c in range(cpi)],
                             axis=0)
        q3 = _problems(q_ref[rows, :], heads, cpi)
        k3 = _problems(1.0 - f, heads, cpi)
        b3 = _problems(bc, heads, cpi)
        v3 = _problems(i_ref[rows, :], heads, cpi).astype(BF16)

        parts = []
        for blk in range(CHUNK // SUB):
            r0 = blk * SUB
            n = r0 + SUB
            p = b3[:, r0 + SUB // 2:r0 + SUB // 2 + 1, :]
            qt = q3[:, r0:n] * jnp.exp(b3[:, r0:n] - p)
            kt = k3[:, 0:n] * jnp.exp(p - b3[:, 0:n])
            a = _bmm_nt(qt, kt)
            keep = _iota2((SUB, n), 1) <= _iota2((SUB, n), 0) + r0
            parts.append(_bmm(jnp.where(keep, a, 0.0), v3[:, 0:n]))
        b_last = b3[:, CHUNK - 1:CHUNK, :]
        ps = pl.ds(pl.multiple_of(i * (cpi * heads), cpi * heads), cpi * heads)
        oi_ref[ps] = jnp.concatenate(parts, axis=1)
        qd_ref[ps] = (q3 * jnp.exp(b3)).astype(BF16)
        kv_ref[ps] = _bmm_tn(v3, k3 * jnp.exp(b_last - b3))
        dec_ref[ps] = jnp.exp(b_last)
        return carry

    lax.fori_loop(0, n_chunks // cpi, phase1, 0)

    nw = nw_ref[...]

    def phase2(c, carry):
        ps = pl.ds(pl.multiple_of(c * heads, heads), heads)
        rows = pl.ds(pl.multiple_of(c * CHUNK, CHUNK), CHUNK)
        st = st_ref[...]
        o = oi_ref[ps] + _bmm_nt(qd_ref[ps], st)
        st_ref[...] = st * dec_ref[ps] + kv_ref[ps]
        for h in range(heads):
            cols = slice(h * HEAD_DIM, (h + 1) * HEAD_DIM)
            o_ref[rows, cols] = _gated_norm(o[h], nw, g_ref[rows, cols]).astype(o_ref.dtype)
        return carry

    lax.fori_loop(0, n_chunks, phase2, 0, unroll=2)


def _hgrn(proj, lb_logits2, norm_w, batch, seq, heads, col0, tb=512):
    per_b = seq // tb
    width = heads * HEAD_DIM
    cb0 = col0 // heads

    def col(group):
        return pl.BlockSpec((tb, width), lambda b, j, g=group: (b * per_b + j, cb0 + g))

    n_prob = (tb // CHUNK) * heads
    return pl.pallas_call(
        functools.partial(_hgrn_kernel, heads=heads, tb=tb),
        grid=(batch, per_b),
        in_specs=[pl.BlockSpec(lb_logits2.shape, lambda b, j: (0, 0)),
                  col(0), col(1), col(2), col(3),
                  pl.BlockSpec((1, HEAD_DIM), lambda b, j: (0, 0))],
        out_specs=pl.BlockSpec((tb, width), lambda b, j: (b * per_b + j, 0)),
        out_shape=jax.ShapeDtypeStruct((batch * seq, width), BF16),
        scratch_shapes=[pltpu.VMEM((heads, HEAD_DIM, HEAD_DIM), F32),
                        pltpu.VMEM((n_prob, CHUNK, HEAD_DIM), F32),
                        pltpu.VMEM((n_prob, CHUNK, HEAD_DIM), BF16),
                        pltpu.VMEM((n_prob, HEAD_DIM, HEAD_DIM), F32),
                        pltpu.VMEM((n_prob, 1, HEAD_DIM), F32)],
        compiler_params=_cparams(("arbitrary", "arbitrary")),
        name="hgrn",
    )(lb_logits2, proj, proj, proj, proj, norm_w)


def _softplus(x):
    return jnp.maximum(x, 0.0) + jnp.log(1.0 + jnp.exp(-jnp.abs(x)))


def _l2norm(x):
    return x * lax.rsqrt(jnp.sum(x * x, axis=-1, keepdims=True) + EPS)


def _bmm(a, b):
    return lax.dot_general(a.astype(BF16), b.astype(BF16), (((2,), (1,)), ((0,), (0,))),
                           preferred_element_type=F32)


def _bmm_nt(a, b):
    return lax.dot_general(a.astype(BF16), b.astype(BF16), (((2,), (2,)), ((0,), (0,))),
                           preferred_element_type=F32)


def _bmm_tn(a, b):
    return lax.dot_general(a.astype(BF16), b.astype(BF16), (((1,), (1,)), ((0,), (0,))),
                           preferred_element_type=F32)


def _conv_silu_tile(x_ref, halo_ref, w, r0, rb):
    cur = x_ref[pl.ds(r0, rb), :]
    prev = x_ref[pl.ds(pl.multiple_of(jnp.maximum(r0 - HALO, 0), HALO), HALO), :]
    prev = jnp.where(r0 == 0, halo_ref[...], prev)
    ext = jnp.concatenate([prev, cur], axis=0)
    acc = w[CONV_K - 1:CONV_K, :] * cur
    for j in range(CONV_K - 1):
        s = HALO - (CONV_K - 1) + j
        acc = acc + w[j:j + 1, :] * ext[s:s + rb, :]
    return _silu(acc)


def _problems(x, heads, chunks):
    return jnp.stack([x[c * CHUNK:(c + 1) * CHUNK, h * HEAD_DIM:(h + 1) * HEAD_DIM]
                      for c in range(chunks) for h in range(heads)], axis=0)


GDN_ROWS = 2 * CHUNK


def _gdn_kernel(q_ref, k_ref, v_ref, g_ref, ab_ref, wq_ref, wk_ref, wv_ref, alog_ref, dtb_ref,
                nw_ref, o_ref, s_ref, hq_ref, hk_ref, hv_ref, u_ref, w_ref, qd_ref, kt_ref, qk_ref,
                tail_ref, *, heads, tb):
    n_chunks = tb // CHUNK
    cpi = GDN_ROWS // CHUNK

    @pl.when(pl.program_id(1) == 0)
    def _():
        s_ref[...] = jnp.zeros_like(s_ref)
        hq_ref[...] = jnp.zeros_like(hq_ref)
        hk_ref[...] = jnp.zeros_like(hk_ref)
        hv_ref[...] = jnp.zeros_like(hv_ref)

    width = heads * HEAD_DIM
    row = _iota2((CHUNK, CHUNK), 0)
    colm = _iota2((CHUNK, CHUNK), 1)
    tri = (colm <= row).astype(BF16)
    incl = colm <= row
    strict = colm < row
    sub_shift = SUB.bit_length() - 1
    same_blk = (row >> sub_shift) == (colm >> sub_shift)
    eye = (row == colm).astype(F32)
    src = _iota2((LANES, width), 0)
    dst_head = _iota2((LANES, width), 1) >> (HEAD_DIM.bit_length() - 1)
    sel_a = (src == dst_head).astype(BF16)
    sel_b = (src == dst_head + heads).astype(BF16)

    def phase1(i, carry):
        r0 = pl.multiple_of(i * GDN_ROWS, GDN_ROWS)
        qa = _conv_silu_tile(q_ref, hq_ref, wq_ref[...], r0, GDN_ROWS)
        ka = _conv_silu_tile(k_ref, hk_ref, wk_ref[...], r0, GDN_ROWS)
        va = _conv_silu_tile(v_ref, hv_ref, wv_ref[...], r0, GDN_ROWS)
        qn = jnp.concatenate([_l2norm(qa[:, h * HEAD_DIM:(h + 1) * HEAD_DIM]) for h in range(heads)],
                             axis=1) * (HEAD_DIM ** -0.5)
        kn = jnp.concatenate([_l2norm(ka[:, h * HEAD_DIM:(h + 1) * HEAD_DIM]) for h in range(heads)],
                             axis=1)

        ab = ab_ref[pl.ds(r0, GDN_ROWS), :]
        la = -jnp.exp(alog_ref[...]) * _softplus(ab + dtb_ref[...])
        be = _sigmoid(ab)
        gc = jnp.concatenate([_dot_exact_lhs(tri, la[c * CHUNK:(c + 1) * CHUNK]) for c in range(cpi)],
                             axis=0)
        g_rep = _dot_exact_rhs(gc, sel_a)
        be_rep = _dot_exact_rhs(be, sel_b)

        q3 = _problems(qn, heads, cpi)
        k3 = _problems(kn, heads, cpi)
        v3 = _problems(va, heads, cpi)
        g3 = _problems(g_rep, heads, cpi)
        b3 = _problems(be_rep, heads, cpi)

        g_sq = g3[:, :, :CHUNK]
        g_row = jnp.sum(g_sq * eye, axis=1, keepdims=True)
        gamma = jnp.exp(jnp.where(incl, g_sq - g_row, -jnp.inf))
        kb = k3.astype(BF16)
        mm = jnp.where(strict, b3[:, :, :CHUNK] * _bmm_nt(kb, kb) * gamma, 0.0)
        nd = jnp.where(same_blk, -mm, 0.0)
        off = jnp.where(same_blk, 0.0, mm)
        dinv = eye + nd
        pw = nd
        for _ in range(3):
            pw = _bmm(pw, pw)
            dinv = dinv + _bmm(dinv, pw)
        fm = -_bmm(dinv, off)
        ipf = eye + fm
        tinv = _bmm(ipf + _bmm(ipf, _bmm(fm, fm)), dinv)

        eg = jnp.exp(g3)
        uw = _bmm(tinv, jnp.concatenate([v3 * b3, k3 * (b3 * eg)], axis=2))
        g_last = g3[:, CHUNK - 1:CHUNK, :]
        p0 = pl.multiple_of(i * (cpi * heads), cpi * heads)
        ps = pl.ds(p0, cpi * heads)
        u_ref[ps] = uw[:, :, :HEAD_DIM]
        w_ref[ps] = uw[:, :, HEAD_DIM:].astype(BF16)
        qd_ref[ps] = (q3 * eg).astype(BF16)
        kt_ref[ps] = (k3 * jnp.exp(g_last - g3)).astype(BF16)
        qk_ref[ps] = (_bmm_nt(q3, kb) * gamma).astype(BF16)
        tail_ref[ps] = jnp.exp(g_last)
        return carry

    lax.fori_loop(0, n_chunks // cpi, phase1, 0)
    hq_ref[...] = q_ref[tb - HALO:tb, :]
    hk_ref[...] = k_ref[tb - HALO:tb, :]
    hv_ref[...] = v_ref[tb - HALO:tb, :]

    nw = nw_ref[...]

    def phase2(c, carry):
        ps = pl.ds(pl.multiple_of(c * heads, heads), heads)
        rows = pl.ds(pl.multiple_of(c * CHUNK, CHUNK), CHUNK)
        s = s_ref[...]
        sb = s.astype(BF16)
        v_new = u_ref[ps] - _bmm(w_ref[ps], sb)
        vb = v_new.astype(BF16)
        o = _bmm(qd_ref[ps], sb) + _bmm(qk_ref[ps], vb)
        s_ref[...] = s * tail_ref[ps] + _bmm_tn(kt_ref[ps], vb)
        for h in range(heads):
            cols = slice(h * HEAD_DIM, (h + 1) * HEAD_DIM)
            o_ref[rows, cols] = _gated_norm(o[h], nw, g_ref[rows, cols]).astype(o_ref.dtype)
        return carry

    lax.fori_loop(0, n_chunks, phase2, 0, unroll=2)


def _gdn(proj, ab, conv_w, a_log_pad, dt_bias_pad, norm_w, batch, seq, heads, col0, tb=512):
    per_b = seq // tb
    width = heads * HEAD_DIM
    cb0 = col0 // heads

    def col(group):
        return pl.BlockSpec((tb, width), lambda b, j, g=group: (b * per_b + j, cb0 + g))

    def wcol(group):
        return pl.BlockSpec((CONV_K, width), lambda b, j, g=group: (0, g))

    small = pl.BlockSpec((1, LANES), lambda b, j: (0, 0))
    n_prob = (tb // CHUNK) * heads
    halo = pltpu.VMEM((HALO, width), F32)
    prob = lambda n, dt: pltpu.VMEM((n_prob, CHUNK, n), dt)
    return pl.pallas_call(
        functools.partial(_gdn_kernel, heads=heads, tb=tb),
        grid=(batch, per_b),
        in_specs=[col(0), col(1), col(2), col(3),
                  pl.BlockSpec((tb, LANES), lambda b, j: (b * per_b + j, 0)),
                  wcol(0), wcol(1), wcol(2), small, small, small],
        out_specs=pl.BlockSpec((tb, width), lambda b, j: (b * per_b + j, 0)),
        out_shape=jax.ShapeDtypeStruct((batch * seq, width), BF16),
        scratch_shapes=[pltpu.VMEM((heads, HEAD_DIM, HEAD_DIM), F32), halo, halo, halo,
                        prob(HEAD_DIM, F32), prob(HEAD_DIM, BF16), prob(HEAD_DIM, BF16),
                        prob(HEAD_DIM, BF16), prob(CHUNK, BF16),
                        pltpu.VMEM((n_prob, 1, HEAD_DIM), F32)],
        compiler_params=_cparams(("arbitrary", "arbitrary")),
        name="gdn",
    )(proj, proj, proj, proj, ab, conv_w, conv_w, conv_w, a_log_pad, dt_bias_pad, norm_w)


def _rms(y, w):
    return (y * lax.rsqrt(jnp.mean(y * y, axis=-1, keepdims=True) + EPS)) * w


def _outproj_kernel(oh_ref, og_ref, wh_ref, wg_ref, x_ref, gt_ref, pw_ref, fw_ref, sc_ref, sh_ref,
                    x1_ref, h2_ref):
    y = (jnp.dot(oh_ref[...], wh_ref[...], preferred_element_type=F32)
         + jnp.dot(og_ref[...], wg_ref[...], preferred_element_type=F32))
    x1 = x_ref[...] + gt_ref[0] * _rms(y, pw_ref[...])
    x1_ref[...] = x1
    h2_ref[...] = (_rms(x1, fw_ref[...]) * (1.0 + sc_ref[0]) + sh_ref[0]).astype(BF16)


def _outproj(o_hg, o_gdn, w_hg, w_gdn, x2, gt, post_w, ffn_w, sc, sh, seq, bm=512):
    m, d = x2.shape
    kh = o_hg.shape[1]
    per_b = seq // bm
    vec = pl.BlockSpec((1, d), lambda i: (0, 0))
    mod = pl.BlockSpec((1, 1, d), lambda i: (i // per_b, 0, 0))
    return pl.pallas_call(
        _outproj_kernel,
        grid=(m // bm,),
        in_specs=[pl.BlockSpec((bm, kh), lambda i: (i, 0)),
                  pl.BlockSpec((bm, kh), lambda i: (i, 0)),
                  pl.BlockSpec((kh, d), lambda i: (0, 0)),
                  pl.BlockSpec((kh, d), lambda i: (0, 0)),
                  pl.BlockSpec((bm, d), lambda i: (i, 0)),
                  mod, vec, vec, mod, mod],
        out_specs=[pl.BlockSpec((bm, d), lambda i: (i, 0)),
                   pl.BlockSpec((bm, d), lambda i: (i, 0))],
        out_shape=[jax.ShapeDtypeStruct((m, d), F32),
                   jax.ShapeDtypeStruct((m, d), BF16)],
        compiler_params=_cparams(("arbitrary",)),
        name="outproj",
    )(o_hg, o_gdn, w_hg, w_gdn, x2, gt, post_w, ffn_w, sc, sh)


def _ffn_kernel(h_ref, w1_ref, w2_ref, x1_ref, gt_ref, pw_ref, o_ref):
    f = pl.program_id(1)

    @pl.when(f == 0)
    def _():
        o_ref[...] = jnp.zeros_like(o_ref)

    a = jnp.maximum(jnp.dot(h_ref[...], w1_ref[...].astype(BF16), preferred_element_type=F32), 0.0)
    o_ref[...] += jnp.dot((a * a).astype(BF16), w2_ref[...].astype(BF16), preferred_element_type=F32)

    @pl.when(f == pl.num_programs(1) - 1)
    def _():
        o_ref[...] = x1_ref[...] + gt_ref[0] * _rms(o_ref[...], pw_ref[...])


def _ffn(h2, w1, w2, x1, gt, post_w, seq, bm=1024, bf=512):
    m, d = h2.shape
    dff = w1.shape[1]
    per_b = seq // bm
    once = pl.Buffered(1)
    return pl.pallas_call(
        _ffn_kernel,
        grid=(m // bm, dff // bf),
        in_specs=[pl.BlockSpec((bm, d), lambda i, f: (i, 0), pipeline_mode=once),
                  pl.BlockSpec((d, bf), lambda i, f: (0, f)),
                  pl.BlockSpec((bf, d), lambda i, f: (f, 0)),
                  pl.BlockSpec((bm, d), lambda i, f: (i, 0), pipeline_mode=once),
                  pl.BlockSpec((1, 1, d), lambda i, f: (i // per_b, 0, 0)),
                  pl.BlockSpec((1, d), lambda i, f: (0, 0))],
        out_specs=pl.BlockSpec((bm, d), lambda i, f: (i, 0), pipeline_mode=once),
        out_shape=jax.ShapeDtypeStruct((m, d), F32),
        compiler_params=_cparams(("arbitrary", "arbitrary")),
        name="ffn",
    )(h2, w1, w2, x1, gt, post_w)


def _layer(x, mod, pre_mix_w, post_mix_w, pre_ffn_w, post_ffn_w, w_in, lb_logits, hg_norm_w,
           conv_w, a_log, dt_bias, gdn_norm_w, w_out, w_ff1, w_ff2):
    batch, seq, d = x.shape
    hg_heads = lb_logits.shape[1]
    gdn_heads = a_log.shape[0]
    hg_cols = 4 * hg_heads * HEAD_DIM
    n_main = hg_cols + 4 * gdn_heads * HEAD_DIM

    x2 = x.reshape(batch * seq, d)
    sh_m, sc_m, gt_m, sh_f, sc_f, gt_f = [mod[:batch, None, i * d:(i + 1) * d] for i in range(6)]
    row = lambda v: v.reshape(1, -1)
    pad_lanes = lambda v: jnp.pad(v.reshape(1, -1), ((0, 0), (0, LANES - v.shape[-1])))

    w_in_b = w_in.astype(BF16)
    w_ab = jnp.pad(w_in_b[:, n_main:], ((0, 0), (0, LANES - (w_in.shape[1] - n_main))))
    proj, ab = _inproj(x2, row(pre_mix_w), sc_m, sh_m, w_in_b, w_ab, n_main, seq)

    o_hg = _hgrn(proj, lb_logits.reshape(lb_logits.shape[0], -1), row(hg_norm_w), batch, seq,
                 hg_heads, 0)
    o_gdn = _gdn(proj, ab, conv_w, pad_lanes(a_log), pad_lanes(dt_bias), row(gdn_norm_w),
                 batch, seq, gdn_heads, hg_cols // HEAD_DIM)

    kh = hg_heads * HEAD_DIM
    w_out_b = w_out.astype(BF16)
    x1, h2 = _outproj(o_hg, o_gdn, w_out_b[:kh], w_out_b[kh:], x2, gt_m, row(post_mix_w),
                      row(pre_ffn_w), sc_f, sh_f, seq)
    out = _ffn(h2, w_ff1, w_ff2, x1, gt_f, row(post_ffn_w), seq)
    return out.reshape(batch, seq, d)


def kernel(x, c, w_ada, b_ada, pre_mix_norm, post_mix_norm, pre_ffn_norm, post_ffn_norm, w_in,
           hg_lb_logits, hg_norm, gdn_conv_w, gdn_a_log, gdn_dt_bias, gdn_norm, w_out, w_ff1, w_ff2):
    depth = w_ada.shape[0]
    assert depth == 1 and hg_lb_logits.shape[0] == 2, "single-layer configuration only"
    batch = c.shape[0]
    c_pad = jnp.pad(c, ((0, 8 - batch), (0, 0)))
    for l in range(depth):
        mod = _ada(c_pad, w_ada[l], b_ada[l][None, :])
        x = _layer(x, mod, pre_mix_norm[l], post_mix_norm[l], pre_ffn_norm[l], post_ffn_norm[l],
                   w_in[l], hg_lb_logits, hg_norm[l], gdn_conv_w[l], gdn_a_log[l], gdn_dt_bias[l],
                   gdn_norm[l], w_out[l], w_ff1[l], w_ff2[l])
    return x
```

```python
import functools

import jax
import jax.numpy as jnp
from jax import lax
from jax.experimental import pallas as pl
from jax.experimental.pallas import tpu as pltpu

F32 = jnp.float32
BF16 = jnp.bfloat16
EPS = 1e-6

LANES = 128
HEAD_DIM = 128
CHUNK = 64
SUB = 16
CONV_K = 4
HALO = 8
VMEM_LIMIT = 60 * 1024 * 1024


def _cparams(sem):
    return pltpu.CompilerParams(dimension_semantics=sem, vmem_limit_bytes=VMEM_LIMIT)


def _dot(a, b):
    return jnp.dot(a.astype(BF16), b.astype(BF16), preferred_element_type=F32)


def _dot_nt(a, b):
    return lax.dot_general(a.astype(BF16), b.astype(BF16), (((1,), (1,)), ((), ())),
                           preferred_element_type=F32)


def _dot_tn(a, b):
    return lax.dot_general(a.astype(BF16), b.astype(BF16), (((0,), (0,)), ((), ())),
                           preferred_element_type=F32)


def _split3(x):
    hi = x.astype(BF16)
    r = x - hi.astype(F32)
    mid = r.astype(BF16)
    lo = (r - mid.astype(F32)).astype(BF16)
    return hi, mid, lo


def _dot_exact_lhs(m_bf16, x):
    hi, mid, lo = _split3(x)
    d = lambda p: jnp.dot(m_bf16, p, preferred_element_type=F32)
    return d(hi) + d(mid) + d(lo)


def _dot_exact_rhs(x, m_bf16):
    hi, mid, lo = _split3(x)
    d = lambda p: jnp.dot(p, m_bf16, preferred_element_type=F32)
    return d(hi) + d(mid) + d(lo)


def _sigmoid(x):
    return 1.0 / (1.0 + jnp.exp(-x))


def _silu(x):
    return x * _sigmoid(x)


def _iota2(shape, dim):
    return lax.broadcasted_iota(jnp.int32, shape, dim)


def _ada_kernel(c_ref, w_ref, b_ref, o_ref):
    c = c_ref[...]
    o_ref[...] = _dot(_silu(c), w_ref[...]) + b_ref[...]


def _ada(c_pad, w_ada, b_ada, bn=1024):
    rows, d = c_pad.shape
    n = w_ada.shape[1]
    return pl.pallas_call(
        _ada_kernel,
        grid=(n // bn,),
        in_specs=[pl.BlockSpec((rows, d), lambda j: (0, 0)),
                  pl.BlockSpec((d, bn), lambda j: (0, j)),
                  pl.BlockSpec((1, bn), lambda j: (0, j))],
        out_specs=pl.BlockSpec((rows, bn), lambda j: (0, j)),
        out_shape=jax.ShapeDtypeStruct((rows, n), F32),
        compiler_params=_cparams(("arbitrary",)),
        name="ada",
    )(c_pad, w_ada, b_ada)


def _inproj_kernel(x_ref, nw_ref, sc_ref, sh_ref, w_ref, wab_ref, o_ref, ab_ref, h_ref):
    @pl.when(pl.program_id(1) == 0)
    def _():
        def rows_body(r, carry):
            rows = pl.ds(pl.multiple_of(r * NORM_ROWS, NORM_ROWS), NORM_ROWS)
            x = x_ref[rows, :]
            y = x * lax.rsqrt(jnp.mean(x * x, axis=-1, keepdims=True) + EPS)
            h = (y * nw_ref[...]) * (1.0 + sc_ref[0]) + sh_ref[0]
            hb = h.astype(BF16)
            h_ref[rows, :] = hb
            ab_ref[rows, :] = jnp.dot(hb, wab_ref[...], preferred_element_type=F32)
            return carry

        lax.fori_loop(0, x_ref.shape[0] // NORM_ROWS, rows_body, 0)

    o_ref[...] = jnp.dot(h_ref[...], w_ref[...].astype(BF16), preferred_element_type=F32)


NORM_ROWS = 256


def _inproj(x2, nw, sc, sh, w_main, w_ab, n, seq, bm=2048, bn=512):
    m, d = x2.shape
    per_b = seq // bm
    return pl.pallas_call(
        _inproj_kernel,
        grid=(m // bm, n // bn),
        in_specs=[pl.BlockSpec((bm, d), lambda i, j: (i, 0), pipeline_mode=pl.Buffered(1)),
                  pl.BlockSpec((1, d), lambda i, j: (0, 0)),
                  pl.BlockSpec((1, 1, d), lambda i, j: (i // per_b, 0, 0)),
                  pl.BlockSpec((1, 1, d), lambda i, j: (i // per_b, 0, 0)),
                  pl.BlockSpec((d, bn), lambda i, j: (0, j)),
                  pl.BlockSpec((d, LANES), lambda i, j: (0, 0))],
        out_specs=[pl.BlockSpec((bm, bn), lambda i, j: (i, j)),
                   pl.BlockSpec((bm, LANES), lambda i, j: (i, 0))],
        out_shape=[jax.ShapeDtypeStruct((m, n), F32),
                   jax.ShapeDtypeStruct((m, LANES), F32)],
        scratch_shapes=[pltpu.VMEM((bm, d), BF16)],
        compiler_params=_cparams(("arbitrary", "arbitrary")),
        name="inproj",
    )(x2, nw, sc, sh, w_main, w_ab)


def _gated_norm(o, nw, g):
    y = o * lax.rsqrt(jnp.mean(o * o, axis=-1, keepdims=True) + EPS)
    return (y * nw) * _silu(g)


HGRN_ROWS = 2 * CHUNK


def _hgrn_kernel(lbl_ref, q_ref, f_ref, i_ref, g_ref, nw_ref, o_ref, st_ref, oi_ref, qd_ref, kv_ref,
                 dec_ref, *, heads, tb):
    n_chunks = tb // CHUNK
    cpi = HGRN_ROWS // CHUNK

    @pl.when(pl.program_id(1) == 0)
    def _():
        st_ref[...] = jnp.zeros_like(st_ref)

    l0 = lbl_ref[0:1, :]
    l1 = lbl_ref[1:2, :]
    mx = jnp.maximum(l0, l1)
    e0 = jnp.exp(l0 - mx)
    lb = e0 / (e0 + jnp.exp(l1 - mx))

    tri = (_iota2((CHUNK, CHUNK), 1) <= _iota2((CHUNK, CHUNK), 0)).astype(BF16)

    def phase1(i, carry):
        rows = pl.ds(pl.multiple_of(i * HGRN_ROWS, HGRN_ROWS), HGRN_ROWS)
        f = lb + (1.0 - lb) * _sigmoid(f_ref[rows, :])
        logf = jnp.log(f)
        bc = jnp.concatenate([_dot_exact_lhs(tri, logf[c * CHUNK:(c + 1) * CHUNK]) for c in range(cpi)],
                             axis=0)
        q3 = _problems(q_ref[rows, :], heads, cpi)
        k3 = _problems(1.0 - f, heads, cpi)
        b3 = _problems(bc, heads, cpi)
        v3 = _problems(i_ref[rows, :], heads, cpi).astype(BF16)

        parts = []
        for blk in range(CHUNK // SUB):
            r0 = blk * SUB
            n = r0 + SUB
            p = b3[:, r0 + SUB // 2:r0 + SUB // 2 + 1, :]
            qt = q3[:, r0:n] * jnp.exp(b3[:, r0:n] - p)
            kt = k3[:, 0:n] * jnp.exp(p - b3[:, 0:n])
            a = _bmm_nt(qt, kt)
            keep = _iota2((SUB, n), 1) <= _iota2((SUB, n), 0) + r0
            parts.append(_bmm(jnp.where(keep, a, 0.0), v3[:, 0:n]))
        b_last = b3[:, CHUNK - 1:CHUNK, :]
        ps = pl.ds(pl.multiple_of(i * (cpi * heads), cpi * heads), cpi * heads)
        oi_ref[ps] = jnp.concatenate(parts, axis=1)
        qd_ref[ps] = (q3 * jnp.exp(b3)).astype(BF16)
        kv_ref[ps] = _bmm_tn(v3, k3 * jnp.exp(b_last - b3))
        dec_ref[ps] = jnp.exp(b_last)
        return carry

    lax.fori_loop(0, n_chunks // cpi, phase1, 0)

    nw = nw_ref[...]

    def phase2(c, carry):
        ps = pl.ds(pl.multiple_of(c * heads, heads), heads)
        rows = pl.ds(pl.multiple_of(c * CHUNK, CHUNK), CHUNK)
        st = st_ref[...]
        o = oi_ref[ps] + _bmm_nt(qd_ref[ps], st)
        st_ref[...] = st * dec_ref[ps] + kv_ref[ps]
        for h in range(heads):
            cols = slice(h * HEAD_DIM, (h + 1) * HEAD_DIM)
            o_ref[rows, cols] = _gated_norm(o[h], nw, g_ref[rows, cols]).astype(o_ref.dtype)
        return carry

    lax.fori_loop(0, n_chunks, phase2, 0, unroll=2)


def _hgrn(proj, lb_logits2, norm_w, batch, seq, heads, col0, tb=512):
    per_b = seq // tb
    width = heads * HEAD_DIM
    cb0 = col0 // heads

    def col(group):
        return pl.BlockSpec((tb, width), lambda b, j, g=group: (b * per_b + j, cb0 + g))

    n_prob = (tb // CHUNK) * heads
    return pl.pallas_call(
        functools.partial(_hgrn_kernel, heads=heads, tb=tb),
        grid=(batch, per_b),
        in_specs=[pl.BlockSpec(lb_logits2.shape, lambda b, j: (0, 0)),
                  col(0), col(1), col(2), col(3),
                  pl.BlockSpec((1, HEAD_DIM), lambda b, j: (0, 0))],
        out_specs=pl.BlockSpec((tb, width), lambda b, j: (b * per_b + j, 0)),
        out_shape=jax.ShapeDtypeStruct((batch * seq, width), BF16),
        scratch_shapes=[pltpu.VMEM((heads, HEAD_DIM, HEAD_DIM), F32),
                        pltpu.VMEM((n_prob, CHUNK, HEAD_DIM), F32),
                        pltpu.VMEM((n_prob, CHUNK, HEAD_DIM), BF16),
                        pltpu.VMEM((n_prob, HEAD_DIM, HEAD_DIM), F32),
                        pltpu.VMEM((n_prob, 1, HEAD_DIM), F32)],
        compiler_params=_cparams(("arbitrary", "arbitrary")),
        name="hgrn",
    )(lb_logits2, proj, proj, proj, proj, norm_w)


def _softplus(x):
    return jnp.maximum(x, 0.0) + jnp.log(1.0 + jnp.exp(-jnp.abs(x)))


def _l2norm(x):
    return x * lax.rsqrt(jnp.sum(x * x, axis=-1, keepdims=True) + EPS)


def _bmm(a, b):
    return lax.dot_general(a.astype(BF16), b.astype(BF16), (((2,), (1,)), ((0,), (0,))),
                           preferred_element_type=F32)


def _bmm_nt(a, b):
    return lax.dot_general(a.astype(BF16), b.astype(BF16), (((2,), (2,)), ((0,), (0,))),
                           preferred_element_type=F32)


def _bmm_tn(a, b):
    return lax.dot_general(a.astype(BF16), b.astype(BF16), (((1,), (1,)), ((0,), (0,))),
                           preferred_element_type=F32)


def _conv_silu_tile(x_ref, halo_ref, w, r0, rb):
    cur = x_ref[pl.ds(r0, rb), :]
    prev = x_ref[pl.ds(pl.multiple_of(jnp.maximum(r0 - HALO, 0), HALO), HALO), :]
    prev = jnp.where(r0 == 0, halo_ref[...], prev)
    ext = jnp.concatenate([prev, cur], axis=0)
    acc = w[CONV_K - 1:CONV_K, :] * cur
    for j in range(CONV_K - 1):
        s = HALO - (CONV_K - 1) + j
        acc = acc + w[j:j + 1, :] * ext[s:s + rb, :]
    return _silu(acc)


def _problems(x, heads, chunks):
    return jnp.stack([x[c * CHUNK:(c + 1) * CHUNK, h * HEAD_DIM:(h + 1) * HEAD_DIM]
                      for c in range(chunks) for h in range(heads)], axis=0)


GDN_ROWS = 2 * CHUNK


def _gdn_kernel(q_ref, k_ref, v_ref, g_ref, ab_ref, wq_ref, wk_ref, wv_ref, alog_ref, dtb_ref,
                nw_ref, o_ref, s_ref, hq_ref, hk_ref, hv_ref, u_ref, w_ref, qd_ref, kt_ref, qk_ref,
                tail_ref, *, heads, tb):
    n_chunks = tb // CHUNK
    cpi = GDN_ROWS // CHUNK

    @pl.when(pl.program_id(1) == 0)
    def _():
        s_ref[...] = jnp.zeros_like(s_ref)
        hq_ref[...] = jnp.zeros_like(hq_ref)
        hk_ref[...] = jnp.zeros_like(hk_ref)
        hv_ref[...] = jnp.zeros_like(hv_ref)

    width = heads * HEAD_DIM
    row = _iota2((CHUNK, CHUNK), 0)
    colm = _iota2((CHUNK, CHUNK), 1)
    tri = (colm <= row).astype(BF16)
    incl = colm <= row
    strict = colm < row
    sub_shift = SUB.bit_length() - 1
    same_blk = (row >> sub_shift) == (colm >> sub_shift)
    eye = (row == colm).astype(F32)
    src = _iota2((LANES, width), 0)
    dst_head = _iota2((LANES, width), 1) >> (HEAD_DIM.bit_length() - 1)
    sel_a = (src == dst_head).astype(BF16)
    sel_b = (src == dst_head + heads).astype(BF16)

    def phase1(i, carry):
        r0 = pl.multiple_of(i * GDN_ROWS, GDN_ROWS)
        qa = _conv_silu_tile(q_ref, hq_ref, wq_ref[...], r0, GDN_ROWS)
        ka = _conv_silu_tile(k_ref, hk_ref, wk_ref[...], r0, GDN_ROWS)
        va = _conv_silu_tile(v_ref, hv_ref, wv_ref[...], r0, GDN_ROWS)
        qn = jnp.concatenate([_l2norm(qa[:, h * HEAD_DIM:(h + 1) * HEAD_DIM]) for h in range(heads)],
                             axis=1) * (HEAD_DIM ** -0.5)
        kn = jnp.concatenate([_l2norm(ka[:, h * HEAD_DIM:(h + 1) * HEAD_DIM]) for h in range(heads)],
                             axis=1)

        ab = ab_ref[pl.ds(r0, GDN_ROWS), :]
        la = -jnp.exp(alog_ref[...]) * _softplus(ab + dtb_ref[...])
        be = _sigmoid(ab)
        gc = jnp.concatenate([_dot_exact_lhs(tri, la[c * CHUNK:(c + 1) * CHUNK]) for c in range(cpi)],
                             axis=0)
        g_rep = _dot_exact_rhs(gc, sel_a)
        be_rep = _dot_exact_rhs(be, sel_b)

        q3 = _problems(qn, heads, cpi)
        k3 = _problems(kn, heads, cpi)
        v3 = _problems(va, heads, cpi)
        g3 = _problems(g_rep, heads, cpi)
        b3 = _problems(be_rep, heads, cpi)

        g_sq = g3[:, :, :CHUNK]
        g_row = jnp.sum(g_sq * eye, axis=1, keepdims=True)
        gamma = jnp.exp(jnp.where(incl, g_sq - g_row, -jnp.inf))
        kb = k3.astype(BF16)
        mm = jnp.where(strict, b3[:, :, :CHUNK] * _bmm_nt(kb, kb) * gamma, 0.0)
        nd = jnp.where(same_blk, -mm, 0.0)
        off = jnp.where(same_blk, 0.0, mm)
        dinv = eye + nd
        pw = nd
        for _ in range(3):
            pw = _bmm(pw, pw)
            dinv = dinv + _bmm(dinv, pw)
        fm = -_bmm(dinv, off)
        ipf = eye + fm
        tinv = _bmm(ipf + _bmm(ipf, _bmm(fm, fm)), dinv)

        eg = jnp.exp(g3)
        uw = _bmm(tinv, jnp.concatenate([v3 * b3, k3 * (b3 * eg)], axis=2))
        g_last = g3[:, CHUNK - 1:CHUNK, :]
        p0 = pl.multiple_of(i * (cpi * heads), cpi * heads)
        ps = pl.ds(p0, cpi * heads)
        u_ref[ps] = uw[:, :, :HEAD_DIM]
        w_ref[ps] = uw[:, :, HEAD_DIM:].astype(BF16)
        qd_ref[ps] = (q3 * eg).astype(BF16)
        kt_ref[ps] = (k3 * jnp.exp(g_last - g3)).astype(BF16)
        qk_ref[ps] = (_bmm_nt(q3, kb) * gamma).astype(BF16)
        tail_ref[ps] = jnp.exp(g_last)
        return carry

    lax.fori_loop(0, n_chunks // cpi, phase1, 0)
    hq_ref[...] = q_ref[tb - HALO:tb, :]
    hk_ref[...] = k_ref[tb - HALO:tb, :]
    hv_ref[...] = v_ref[tb - HALO:tb, :]

    nw = nw_ref[...]

    def phase2(c, carry):
        ps = pl.ds(pl.multiple_of(c * heads, heads), heads)
        rows = pl.ds(pl.multiple_of(c * CHUNK, CHUNK), CHUNK)
        s = s_ref[...]
        sb = s.astype(BF16)
        v_new = u_ref[ps] - _bmm(w_ref[ps], sb)
        vb = v_new.astype(BF16)
        o = _bmm(qd_ref[ps], sb) + _bmm(qk_ref[ps], vb)
        s_ref[...] = s * tail_ref[ps] + _bmm_tn(kt_ref[ps], vb)
        for h in range(heads):
            cols = slice(h * HEAD_DIM, (h + 1) * HEAD_DIM)
            o_ref[rows, cols] = _gated_norm(o[h], nw, g_ref[rows, cols]).astype(o_ref.dtype)
        return carry

    lax.fori_loop(0, n_chunks, phase2, 0, unroll=2)


def _gdn(proj, ab, conv_w, a_log_pad, dt_bias_pad, norm_w, batch, seq, heads, col0, tb=512):
    per_b = seq // tb
    width = heads * HEAD_DIM
    cb0 = col0 // heads

    def col(group):
        return pl.BlockSpec((tb, width), lambda b, j, g=group: (b * per_b + j, cb0 + g))

    def wcol(group):
        return pl.BlockSpec((CONV_K, width), lambda b, j, g=group: (0, g))

    small = pl.BlockSpec((1, LANES), lambda b, j: (0, 0))
    n_prob = (tb // CHUNK) * heads
    halo = pltpu.VMEM((HALO, width), F32)
    prob = lambda n, dt: pltpu.VMEM((n_prob, CHUNK, n), dt)
    return pl.pallas_call(
        functools.partial(_gdn_kernel, heads=heads, tb=tb),
        grid=(batch, per_b),
        in_specs=[col(0), col(1), col(2), col(3),
                  pl.BlockSpec((tb, LANES), lambda b, j: (b * per_b + j, 0)),
                  wcol(0), wcol(1), wcol(2), small, small, small],
        out_specs=pl.BlockSpec((tb, width), lambda b, j: (b * per_b + j, 0)),
        out_shape=jax.ShapeDtypeStruct((batch * seq, width), BF16),
        scratch_shapes=[pltpu.VMEM((heads, HEAD_DIM, HEAD_DIM), F32), halo, halo, halo,
                        prob(HEAD_DIM, F32), prob(HEAD_DIM, BF16), prob(HEAD_DIM, BF16),
                        prob(HEAD_DIM, BF16), prob(CHUNK, BF16),
                        pltpu.VMEM((n_prob, 1, HEAD_DIM), F32)],
        compiler_params=_cparams(("arbitrary", "arbitrary")),
        name="gdn",
    )(proj, proj, proj, proj, ab, conv_w, conv_w, conv_w, a_log_pad, dt_bias_pad, norm_w)


def _rms(y, w):
    return (y * lax.rsqrt(jnp.mean(y * y, axis=-1, keepdims=True) + EPS)) * w


OUT_ROWS = 256


def _outproj_kernel(oh_ref, og_ref, wh_ref, wg_ref, x_ref, gt_ref, pw_ref, fw_ref, sc_ref, sh_ref,
                    x1_ref, h2_ref):
    for r in range(oh_ref.shape[0] // OUT_ROWS):
        rows = pl.ds(r * OUT_ROWS, OUT_ROWS)
        y = (jnp.dot(oh_ref[rows, :], wh_ref[...], preferred_element_type=F32)
             + jnp.dot(og_ref[rows, :], wg_ref[...], preferred_element_type=F32))
        x1 = x_ref[rows, :] + gt_ref[0] * _rms(y, pw_ref[...])
        x1_ref[rows, :] = x1
        h2_ref[rows, :] = (_rms(x1, fw_ref[...]) * (1.0 + sc_ref[0]) + sh_ref[0]).astype(BF16)


def _outproj(o_hg, o_gdn, w_hg, w_gdn, x2, gt, post_w, ffn_w, sc, sh, seq, bm=512):
    m, d = x2.shape
    kh = o_hg.shape[1]
    per_b = seq // bm
    vec = pl.BlockSpec((1, d), lambda i: (0, 0))
    mod = pl.BlockSpec((1, 1, d), lambda i: (i // per_b, 0, 0))
    return pl.pallas_call(
        _outproj_kernel,
        grid=(m // bm,),
        in_specs=[pl.BlockSpec((bm, kh), lambda i: (i, 0)),
                  pl.BlockSpec((bm, kh), lambda i: (i, 0)),
                  pl.BlockSpec((kh, d), lambda i: (0, 0)),
                  pl.BlockSpec((kh, d), lambda i: (0, 0)),
                  pl.BlockSpec((bm, d), lambda i: (i, 0)),
                  mod, vec, vec, mod, mod],
        out_specs=[pl.BlockSpec((bm, d), lambda i: (i, 0)),
                   pl.BlockSpec((bm, d), lambda i: (i, 0))],
        out_shape=[jax.ShapeDtypeStruct((m, d), F32),
                   jax.ShapeDtypeStruct((m, d), BF16)],
        compiler_params=_cparams(("arbitrary",)),
        name="outproj",
    )(o_hg, o_gdn, w_hg, w_gdn, x2, gt, post_w, ffn_w, sc, sh)


def _ffn_kernel(h_ref, w1_ref, w2_ref, x1_ref, gt_ref, pw_ref, o_ref):
    f = pl.program_id(1)

    @pl.when(f == 0)
    def _():
        o_ref[...] = jnp.zeros_like(o_ref)

    a = jnp.maximum(jnp.dot(h_ref[...], w1_ref[...], preferred_element_type=F32), 0.0)
    o_ref[...] += jnp.dot((a * a).astype(BF16), w2_ref[...], preferred_element_type=F32)

    @pl.when(f == pl.num_programs(1) - 1)
    def _():
        o_ref[...] = x1_ref[...] + gt_ref[0] * _rms(o_ref[...], pw_ref[...])


def _ffn(h2, w1, w2, x1, gt, post_w, seq, bm=1024, bf=512):
    m, d = h2.shape
    dff = w1.shape[1]
    per_b = seq // bm
    return pl.pallas_call(
        _ffn_kernel,
        grid=(m // bm, dff // bf),
        in_specs=[pl.BlockSpec((bm, d), lambda i, f: (i, 0)),
                  pl.BlockSpec((d, bf), lambda i, f: (0, f)),
                  pl.BlockSpec((bf, d), lambda i, f: (f, 0)),
                  pl.BlockSpec((bm, d), lambda i, f: (i, 0)),
                  pl.BlockSpec((1, 1, d), lambda i, f: (i // per_b, 0, 0)),
                  pl.BlockSpec((1, d), lambda i, f: (0, 0))],
        out_specs=pl.BlockSpec((bm, d), lambda i, f: (i, 0)),
        out_shape=jax.ShapeDtypeStruct((m, d), F32),
        compiler_params=_cparams(("arbitrary", "arbitrary")),
        name="ffn",
    )(h2, w1, w2, x1, gt, post_w)


def _layer(x, mod, pre_mix_w, post_mix_w, pre_ffn_w, post_ffn_w, w_in, lb_logits, hg_norm_w,
           conv_w, a_log, dt_bias, gdn_norm_w, w_out, w_ff1, w_ff2):
    batch, seq, d = x.shape
    hg_heads = lb_logits.shape[1]
    gdn_heads = a_log.shape[0]
    hg_cols = 4 * hg_heads * HEAD_DIM
    n_main = hg_cols + 4 * gdn_heads * HEAD_DIM

    x2 = x.reshape(batch * seq, d)
    sh_m, sc_m, gt_m, sh_f, sc_f, gt_f = [mod[:batch, None, i * d:(i + 1) * d] for i in range(6)]
    row = lambda v: v.reshape(1, -1)
    pad_lanes = lambda v: jnp.pad(v.reshape(1, -1), ((0, 0), (0, LANES - v.shape[-1])))

    w_ab = jnp.pad(w_in[:, n_main:], ((0, 0), (0, LANES - (w_in.shape[1] - n_main)))).astype(BF16)
    proj, ab = _inproj(x2, row(pre_mix_w), sc_m, sh_m, w_in, w_ab, n_main, seq)

    o_hg = _hgrn(proj, lb_logits.reshape(lb_logits.shape[0], -1), row(hg_norm_w), batch, seq,
                 hg_heads, 0)
    o_gdn = _gdn(proj, ab, conv_w, pad_lanes(a_log), pad_lanes(dt_bias), row(gdn_norm_w),
                 batch, seq, gdn_heads, hg_cols // HEAD_DIM)

    kh = hg_heads * HEAD_DIM
    w_out_b = w_out.astype(BF16)
    x1, h2 = _outproj(o_hg, o_gdn, w_out_b[:kh], w_out_b[kh:], x2, gt_m, row(post_mix_w),
                      row(pre_ffn_w), sc_f, sh_f, seq)
    out = _ffn(h2, w_ff1.astype(BF16), w_ff2.astype(BF16), x1, gt_f, row(post_ffn_w), seq)
    return out.reshape(batch, seq, d)


def kernel(x, c, w_ada, b_ada, pre_mix_norm, post_mix_norm, pre_ffn_norm, post_ffn_norm, w_in,
           hg_lb_logits, hg_norm, gdn_conv_w, gdn_a_log, gdn_dt_bias, gdn_norm, w_out, w_ff1, w_ff2):
    depth = w_ada.shape[0]
    assert depth == 1 and hg_lb_logits.shape[0] == 2, "single-layer configuration only"
    batch = c.shape[0]
    c_pad = jnp.pad(c, ((0, 8 - batch), (0, 0)))
    for l in range(depth):
        mod = _ada(c_pad, w_ada[l], b_ada[l][None, :])
        x = _layer(x, mod, pre_mix_norm[l], post_mix_norm[l], pre_ffn_norm[l], post_ffn_norm[l],
                   w_in[l], hg_lb_logits, hg_norm[l], gdn_conv_w[l], gdn_a_log[l], gdn_dt_bias[l],
                   gdn_norm[l], w_out[l], w_ff1[l], w_ff2[l])
    return x
```

```python
import functools

import jax
import jax.numpy as jnp
from jax import lax
from jax.experimental import pallas as pl
from jax.experimental.pallas import tpu as pltpu

F32 = jnp.float32
BF16 = jnp.bfloat16
EPS = 1e-6

LANES = 128
HEAD_DIM = 128
CHUNK = 64
SUB = 16
CONV_K = 4
HALO = 8
VMEM_LIMIT = 60 * 1024 * 1024


def _cparams(sem):
    return pltpu.CompilerParams(dimension_semantics=sem, vmem_limit_bytes=VMEM_LIMIT)


def _dot(a, b):
    return jnp.dot(a.astype(BF16), b.astype(BF16), preferred_element_type=F32)


def _dot_nt(a, b):
    return lax.dot_general(a.astype(BF16), b.astype(BF16), (((1,), (1,)), ((), ())),
                           preferred_element_type=F32)


def _dot_tn(a, b):
    return lax.dot_general(a.astype(BF16), b.astype(BF16), (((0,), (0,)), ((), ())),
                           preferred_element_type=F32)


def _split3(x):
    hi = x.astype(BF16)
    r = x - hi.astype(F32)
    mid = r.astype(BF16)
    lo = (r - mid.astype(F32)).astype(BF16)
    return hi, mid, lo


def _dot_exact_lhs(m_bf16, x):
    hi, mid, lo = _split3(x)
    d = lambda p: jnp.dot(m_bf16, p, preferred_element_type=F32)
    return d(hi) + d(mid) + d(lo)


def _dot_exact_rhs(x, m_bf16):
    hi, mid, lo = _split3(x)
    d = lambda p: jnp.dot(p, m_bf16, preferred_element_type=F32)
    return d(hi) + d(mid) + d(lo)


def _sigmoid(x):
    return 1.0 / (1.0 + jnp.exp(-x))


def _silu(x):
    return x * _sigmoid(x)


def _iota2(shape, dim):
    return lax.broadcasted_iota(jnp.int32, shape, dim)


def _cast_specs(arrays, n_steps, step_index):
    specs, shapes = [], []
    for a in arrays:
        rows, cols = a.shape
        specs.append(pl.BlockSpec((rows // n_steps, cols), lambda *g: (step_index(*g), 0)))
        shapes.append(jax.ShapeDtypeStruct((rows, cols), BF16))
    return specs, shapes


def _cast_slabs(src_refs, dst_refs):
    for src, dst in zip(src_refs, dst_refs):
        dst[...] = src[...].astype(BF16)


def _ada_kernel(c_ref, w_ref, b_ref, o_ref):
    c = c_ref[...]
    o_ref[...] = _dot(_silu(c), w_ref[...]) + b_ref[...]


def _ada(c_pad, w_ada, b_ada, bn=1024):
    rows, d = c_pad.shape
    n = w_ada.shape[1]
    return pl.pallas_call(
        _ada_kernel,
        grid=(n // bn,),
        in_specs=[pl.BlockSpec((rows, d), lambda j: (0, 0)),
                  pl.BlockSpec((d, bn), lambda j: (0, j)),
                  pl.BlockSpec((1, bn), lambda j: (0, j))],
        out_specs=pl.BlockSpec((rows, bn), lambda j: (0, j)),
        out_shape=jax.ShapeDtypeStruct((rows, n), F32),
        compiler_params=_cparams(("arbitrary",)),
        name="ada",
    )(c_pad, w_ada, b_ada)


def _inproj_kernel(x_ref, nw_ref, sc_ref, sh_ref, w_ref, wab_ref, o_ref, ab_ref, h_ref):
    @pl.when(pl.program_id(1) == 0)
    def _():
        def rows_body(r, carry):
            rows = pl.ds(pl.multiple_of(r * NORM_ROWS, NORM_ROWS), NORM_ROWS)
            x = x_ref[rows, :]
            y = x * lax.rsqrt(jnp.mean(x * x, axis=-1, keepdims=True) + EPS)
            h = (y * nw_ref[...]) * (1.0 + sc_ref[0]) + sh_ref[0]
            hb = h.astype(BF16)
            h_ref[rows, :] = hb
            ab_ref[rows, :] = _dot_nt(hb, wab_ref[...])
            return carry

        lax.fori_loop(0, x_ref.shape[0] // NORM_ROWS, rows_body, 0)

    o_ref[...] = _dot_nt(h_ref[...], w_ref[...])


NORM_ROWS = 256


def _inproj(x2, nw, sc, sh, w_t, w_ab_t, n, seq, bm=2048, bn=512):
    m, d = x2.shape
    per_b = seq // bm
    return pl.pallas_call(
        _inproj_kernel,
        grid=(m // bm, n // bn),
        in_specs=[pl.BlockSpec((bm, d), lambda i, j: (i, 0), pipeline_mode=pl.Buffered(1)),
                  pl.BlockSpec((1, d), lambda i, j: (0, 0)),
                  pl.BlockSpec((1, 1, d), lambda i, j: (i // per_b, 0, 0)),
                  pl.BlockSpec((1, 1, d), lambda i, j: (i // per_b, 0, 0)),
                  pl.BlockSpec((bn, d), lambda i, j: (j, 0)),
                  pl.BlockSpec((LANES, d), lambda i, j: (0, 0))],
        out_specs=[pl.BlockSpec((bm, bn), lambda i, j: (i, j)),
                   pl.BlockSpec((bm, LANES), lambda i, j: (i, 0))],
        out_shape=[jax.ShapeDtypeStruct((m, n), F32),
                   jax.ShapeDtypeStruct((m, LANES), F32)],
        scratch_shapes=[pltpu.VMEM((bm, d), BF16)],
        compiler_params=_cparams(("arbitrary", "arbitrary")),
        name="inproj",
    )(x2, nw, sc, sh, w_t, w_ab_t)


def _gated_norm(o, nw, g):
    y = o * lax.rsqrt(jnp.mean(o * o, axis=-1, keepdims=True) + EPS)
    return (y * nw) * _silu(g)


HGRN_ROWS = 2 * CHUNK


def _hgrn_kernel(lbl_ref, q_ref, f_ref, i_ref, g_ref, nw_ref, *rest, heads, tb, n_cast):
    cast_src, (o_ref, *cast_dst) = rest[:n_cast], rest[n_cast:2 * n_cast + 1]
    st_ref, oi_ref, qd_ref, kv_ref, dec_ref = rest[2 * n_cast + 1:]
    _cast_slabs(cast_src, cast_dst)
    n_chunks = tb // CHUNK
    cpi = HGRN_ROWS // CHUNK

    @pl.when(pl.program_id(1) == 0)
    def _():
        st_ref[...] = jnp.zeros_like(st_ref)

    l0 = lbl_ref[0:1, :]
    l1 = lbl_ref[1:2, :]
    mx = jnp.maximum(l0, l1)
    e0 = jnp.exp(l0 - mx)
    lb = e0 / (e0 + jnp.exp(l1 - mx))

    tri = (_iota2((CHUNK, CHUNK), 1) <= _iota2((CHUNK, CHUNK), 0)).astype(BF16)

    def phase1(i, carry):
        rows = pl.ds(pl.multiple_of(i * HGRN_ROWS, HGRN_ROWS), HGRN_ROWS)
        f = lb + (1.0 - lb) * _sigmoid(f_ref[rows, :])
        logf = jnp.log(f)
        bc = jnp.concatenate([_dot_exact_lhs(tri, logf[c * CHUNK:(c + 1) * CHUNK]) for c in range(cpi)],
                             axis=0)
        q3 = _problems(q_ref[rows, :], heads, cpi)
        k3 = _problems(1.0 - f, heads, cpi)
        b3 = _problems(bc, heads, cpi)
        v3 = _problems(i_ref[rows, :], heads, cpi).astype(BF16)

        parts = []
        for blk in range(CHUNK // SUB):
            r0 = blk * SUB
            n = r0 + SUB
            p = b3[:, r0 + SUB // 2:r0 + SUB // 2 + 1, :]
            qt = q3[:, r0:n] * jnp.exp(b3[:, r0:n] - p)
            kt = k3[:, 0:n] * jnp.exp(p - b3[:, 0:n])
            a = _bmm_nt(qt, kt)
            keep = _iota2((SUB, n), 1) <= _iota2((SUB, n), 0) + r0
            parts.append(_bmm(jnp.where(keep, a, 0.0), v3[:, 0:n]))
        b_last = b3[:, CHUNK - 1:CHUNK, :]
        ps = pl.ds(pl.multiple_of(i * (cpi * heads), cpi * heads), cpi * heads)
        oi_ref[ps] = jnp.concatenate(parts, axis=1)
        qd_ref[ps] = (q3 * jnp.exp(b3)).astype(BF16)
        kv_ref[ps] = _bmm_tn(v3, k3 * jnp.exp(b_last - b3))
        dec_ref[ps] = jnp.exp(b_last)
        return carry

    lax.fori_loop(0, n_chunks // cpi, phase1, 0)

    nw = nw_ref[...]

    def phase2(c, carry):
        ps = pl.ds(pl.multiple_of(c * heads, heads), heads)
        rows = pl.ds(pl.multiple_of(c * CHUNK, CHUNK), CHUNK)
        st = st_ref[...]
        o = oi_ref[ps] + _bmm_nt(qd_ref[ps], st)
        st_ref[...] = st * dec_ref[ps] + kv_ref[ps]
        for h in range(heads):
            cols = slice(h * HEAD_DIM, (h + 1) * HEAD_DIM)
            o_ref[rows, cols] = _gated_norm(o[h], nw, g_ref[rows, cols]).astype(o_ref.dtype)
        return carry

    lax.fori_loop(0, n_chunks, phase2, 0, unroll=2)


def _hgrn(proj, lb_logits2, norm_w, casts, batch, seq, heads, col0, tb=512):
    per_b = seq // tb
    cast_specs, cast_shapes = _cast_specs(casts, batch * per_b, lambda b, j: b * per_b + j)
    width = heads * HEAD_DIM
    cb0 = col0 // heads

    def col(group):
        return pl.BlockSpec((tb, width), lambda b, j, g=group: (b * per_b + j, cb0 + g))

    n_prob = (tb // CHUNK) * heads
    return pl.pallas_call(
        functools.partial(_hgrn_kernel, heads=heads, tb=tb, n_cast=len(casts)),
        grid=(batch, per_b),
        in_specs=[pl.BlockSpec(lb_logits2.shape, lambda b, j: (0, 0)),
                  col(0), col(1), col(2), col(3),
                  pl.BlockSpec((1, HEAD_DIM), lambda b, j: (0, 0))] + cast_specs,
        out_specs=[pl.BlockSpec((tb, width), lambda b, j: (b * per_b + j, 0))] + cast_specs,
        out_shape=[jax.ShapeDtypeStruct((batch * seq, width), BF16)] + cast_shapes,
        scratch_shapes=[pltpu.VMEM((heads, HEAD_DIM, HEAD_DIM), F32),
                        pltpu.VMEM((n_prob, CHUNK, HEAD_DIM), F32),
                        pltpu.VMEM((n_prob, CHUNK, HEAD_DIM), BF16),
                        pltpu.VMEM((n_prob, HEAD_DIM, HEAD_DIM), F32),
                        pltpu.VMEM((n_prob, 1, HEAD_DIM), F32)],
        compiler_params=_cparams(("arbitrary", "arbitrary")),
        name="hgrn",
    )(lb_logits2, proj, proj, proj, proj, norm_w, *casts)


def _softplus(x):
    return jnp.maximum(x, 0.0) + jnp.log(1.0 + jnp.exp(-jnp.abs(x)))


def _l2norm(x):
    return x * lax.rsqrt(jnp.sum(x * x, axis=-1, keepdims=True) + EPS)


def _bmm(a, b):
    return lax.dot_general(a.astype(BF16), b.astype(BF16), (((2,), (1,)), ((0,), (0,))),
                           preferred_element_type=F32)


def _bmm_nt(a, b):
    return lax.dot_general(a.astype(BF16), b.astype(BF16), (((2,), (2,)), ((0,), (0,))),
                           preferred_element_type=F32)


def _bmm_tn(a, b):
    return lax.dot_general(a.astype(BF16), b.astype(BF16), (((1,), (1,)), ((0,), (0,))),
                           preferred_element_type=F32)


def _conv_silu_tile(x_ref, halo_ref, w, r0, rb):
    cur = x_ref[pl.ds(r0, rb), :]
    prev = x_ref[pl.ds(pl.multiple_of(jnp.maximum(r0 - HALO, 0), HALO), HALO), :]
    prev = jnp.where(r0 == 0, halo_ref[...], prev)
    ext = jnp.concatenate([prev, cur], axis=0)
    acc = w[CONV_K - 1:CONV_K, :] * cur
    for j in range(CONV_K - 1):
        s = HALO - (CONV_K - 1) + j
        acc = acc + w[j:j + 1, :] * ext[s:s + rb, :]
    return _silu(acc)


def _problems(x, heads, chunks):
    return jnp.stack([x[c * CHUNK:(c + 1) * CHUNK, h * HEAD_DIM:(h + 1) * HEAD_DIM]
                      for c in range(chunks) for h in range(heads)], axis=0)


GDN_ROWS = 2 * CHUNK


def _gdn_kernel(q_ref, k_ref, v_ref, g_ref, ab_ref, wq_ref, wk_ref, wv_ref, alog_ref, dtb_ref,
                nw_ref, *rest, heads, tb, n_cast):
    cast_src, (o_ref, *cast_dst) = rest[:n_cast], rest[n_cast:2 * n_cast + 1]
    s_ref, hq_ref, hk_ref, hv_ref, u_ref, w_ref, qd_ref, kt_ref, qk_ref, tail_ref = rest[2 * n_cast + 1:]
    _cast_slabs(cast_src, cast_dst)
    n_chunks = tb // CHUNK
    cpi = GDN_ROWS // CHUNK

    @pl.when(pl.program_id(1) == 0)
    def _():
        s_ref[...] = jnp.zeros_like(s_ref)
        hq_ref[...] = jnp.zeros_like(hq_ref)
        hk_ref[...] = jnp.zeros_like(hk_ref)
        hv_ref[...] = jnp.zeros_like(hv_ref)

    width = heads * HEAD_DIM
    row = _iota2((CHUNK, CHUNK), 0)
    colm = _iota2((CHUNK, CHUNK), 1)
    tri = (colm <= row).astype(BF16)
    incl = colm <= row
    strict = colm < row
    sub_shift = SUB.bit_length() - 1
    same_blk = (row >> sub_shift) == (colm >> sub_shift)
    eye = (row == colm).astype(F32)
    src = _iota2((LANES, width), 0)
    dst_head = _iota2((LANES, width), 1) >> (HEAD_DIM.bit_length() - 1)
    sel_a = (src == dst_head).astype(BF16)
    sel_b = (src == dst_head + heads).astype(BF16)

    def phase1(i, carry):
        r0 = pl.multiple_of(i * GDN_ROWS, GDN_ROWS)
        qa = _conv_silu_tile(q_ref, hq_ref, wq_ref[...], r0, GDN_ROWS)
        ka = _conv_silu_tile(k_ref, hk_ref, wk_ref[...], r0, GDN_ROWS)
        va = _conv_silu_tile(v_ref, hv_ref, wv_ref[...], r0, GDN_ROWS)
        qn = jnp.concatenate([_l2norm(qa[:, h * HEAD_DIM:(h + 1) * HEAD_DIM]) for h in range(heads)],
                             axis=1) * (HEAD_DIM ** -0.5)
        kn = jnp.concatenate([_l2norm(ka[:, h * HEAD_DIM:(h + 1) * HEAD_DIM]) for h in range(heads)],
                             axis=1)

        ab = ab_ref[pl.ds(r0, GDN_ROWS), :]
        la = -jnp.exp(alog_ref[...]) * _softplus(ab + dtb_ref[...])
        be = _sigmoid(ab)
        gc = jnp.concatenate([_dot_exact_lhs(tri, la[c * CHUNK:(c + 1) * CHUNK]) for c in range(cpi)],
                             axis=0)
        g_rep = _dot_exact_rhs(gc, sel_a)
        be_rep = _dot_exact_rhs(be, sel_b)

        q3 = _problems(qn, heads, cpi)
        k3 = _problems(kn, heads, cpi)
        v3 = _problems(va, heads, cpi)
        g3 = _problems(g_rep, heads, cpi)
        b3 = _problems(be_rep, heads, cpi)

        g_sq = g3[:, :, :CHUNK]
        g_row = jnp.sum(g_sq * eye, axis=1, keepdims=True)
        gamma = jnp.exp(jnp.where(incl, g_sq - g_row, -jnp.inf))
        kb = k3.astype(BF16)
        mm = jnp.where(strict, b3[:, :, :CHUNK] * _bmm_nt(kb, kb) * gamma, 0.0)
        nd = jnp.where(same_blk, -mm, 0.0)
        off = jnp.where(same_blk, 0.0, mm)
        dinv = eye + nd
        pw = nd
        for _ in range(3):
            pw = _bmm(pw, pw)
            dinv = dinv + _bmm(dinv, pw)
        fm = -_bmm(dinv, off)
        ipf = eye + fm
        tinv = _bmm(ipf + _bmm(ipf, _bmm(fm, fm)), dinv)

        eg = jnp.exp(g3)
        uw = _bmm(tinv, jnp.concatenate([v3 * b3, k3 * (b3 * eg)], axis=2))
        g_last = g3[:, CHUNK - 1:CHUNK, :]
        p0 = pl.multiple_of(i * (cpi * heads), cpi * heads)
        ps = pl.ds(p0, cpi * heads)
        u_ref[ps] = uw[:, :, :HEAD_DIM]
        w_ref[ps] = uw[:, :, HEAD_DIM:].astype(BF16)
        qd_ref[ps] = (q3 * eg).astype(BF16)
        kt_ref[ps] = (k3 * jnp.exp(g_last - g3)).astype(BF16)
        qk_ref[ps] = (_bmm_nt(q3, kb) * gamma).astype(BF16)
        tail_ref[ps] = jnp.exp(g_last)
        return carry

    lax.fori_loop(0, n_chunks // cpi, phase1, 0)
    hq_ref[...] = q_ref[tb - HALO:tb, :]
    hk_ref[...] = k_ref[tb - HALO:tb, :]
    hv_ref[...] = v_ref[tb - HALO:tb, :]

    nw = nw_ref[...]

    def phase2(c, carry):
        ps = pl.ds(pl.multiple_of(c * heads, heads), heads)
        rows = pl.ds(pl.multiple_of(c * CHUNK, CHUNK), CHUNK)
        s = s_ref[...]
        sb = s.astype(BF16)
        v_new = u_ref[ps] - _bmm(w_ref[ps], sb)
        vb = v_new.astype(BF16)
        o = _bmm(qd_ref[ps], sb) + _bmm(qk_ref[ps], vb)
        s_ref[...] = s * tail_ref[ps] + _bmm_tn(kt_ref[ps], vb)
        for h in range(heads):
            cols = slice(h * HEAD_DIM, (h + 1) * HEAD_DIM)
            o_ref[rows, cols] = _gated_norm(o[h], nw, g_ref[rows, cols]).astype(o_ref.dtype)
        return carry

    lax.fori_loop(0, n_chunks, phase2, 0, unroll=2)


def _gdn(proj, ab, conv_w, a_log_pad, dt_bias_pad, norm_w, casts, batch, seq, heads, col0, tb=512):
    per_b = seq // tb
    cast_specs, cast_shapes = _cast_specs(casts, batch * per_b, lambda b, j: b * per_b + j)
    width = heads * HEAD_DIM
    cb0 = col0 // heads

    def col(group):
        return pl.BlockSpec((tb, width), lambda b, j, g=group: (b * per_b + j, cb0 + g))

    def wcol(group):
        return pl.BlockSpec((CONV_K, width), lambda b, j, g=group: (0, g))

    small = pl.BlockSpec((1, LANES), lambda b, j: (0, 0))
    n_prob = (tb // CHUNK) * heads
    halo = pltpu.VMEM((HALO, width), F32)
    prob = lambda n, dt: pltpu.VMEM((n_prob, CHUNK, n), dt)
    return pl.pallas_call(
        functools.partial(_gdn_kernel, heads=heads, tb=tb, n_cast=len(casts)),
        grid=(batch, per_b),
        in_specs=[col(0), col(1), col(2), col(3),
                  pl.BlockSpec((tb, LANES), lambda b, j: (b * per_b + j, 0)),
                  wcol(0), wcol(1), wcol(2), small, small, small] + cast_specs,
        out_specs=[pl.BlockSpec((tb, width), lambda b, j: (b * per_b + j, 0))] + cast_specs,
        out_shape=[jax.ShapeDtypeStruct((batch * seq, width), BF16)] + cast_shapes,
        scratch_shapes=[pltpu.VMEM((heads, HEAD_DIM, HEAD_DIM), F32), halo, halo, halo,
                        prob(HEAD_DIM, F32), prob(HEAD_DIM, BF16), prob(HEAD_DIM, BF16),
                        prob(HEAD_DIM, BF16), prob(CHUNK, BF16),
                        pltpu.VMEM((n_prob, 1, HEAD_DIM), F32)],
        compiler_params=_cparams(("arbitrary", "arbitrary")),
        name="gdn",
    )(proj, proj, proj, proj, ab, conv_w, conv_w, conv_w, a_log_pad, dt_bias_pad, norm_w, *casts)


def _rms(y, w):
    return (y * lax.rsqrt(jnp.mean(y * y, axis=-1, keepdims=True) + EPS)) * w


OUT_ROWS = 256


def _outproj_kernel(oh_ref, og_ref, wh_ref, wg_ref, x_ref, gt_ref, pw_ref, fw_ref, sc_ref, sh_ref,
                    x1_ref, h2_ref):
    for r in range(oh_ref.shape[0] // OUT_ROWS):
        rows = pl.ds(r * OUT_ROWS, OUT_ROWS)
        y = (jnp.dot(oh_ref[rows, :], wh_ref[...], preferred_element_type=F32)
             + jnp.dot(og_ref[rows, :], wg_ref[...], preferred_element_type=F32))
        x1 = x_ref[rows, :] + gt_ref[0] * _rms(y, pw_ref[...])
        x1_ref[rows, :] = x1
        h2_ref[rows, :] = (_rms(x1, fw_ref[...]) * (1.0 + sc_ref[0]) + sh_ref[0]).astype(BF16)


def _outproj(o_hg, o_gdn, w_out, x2, gt, post_w, ffn_w, sc, sh, seq, kh, bm=512):
    m, d = x2.shape
    per_b = seq // bm
    vec = pl.BlockSpec((1, d), lambda i: (0, 0))
    mod = pl.BlockSpec((1, 1, d), lambda i: (i // per_b, 0, 0))
    return pl.pallas_call(
        _outproj_kernel,
        grid=(m // bm,),
        in_specs=[pl.BlockSpec((bm, kh), lambda i: (i, 0)),
                  pl.BlockSpec((bm, kh), lambda i: (i, 0)),
                  pl.BlockSpec((kh, d), lambda i: (0, 0)),
                  pl.BlockSpec((kh, d), lambda i: (1, 0)),
                  pl.BlockSpec((bm, d), lambda i: (i, 0)),
                  mod, vec, vec, mod, mod],
        out_specs=[pl.BlockSpec((bm, d), lambda i: (i, 0)),
                   pl.BlockSpec((bm, d), lambda i: (i, 0))],
        out_shape=[jax.ShapeDtypeStruct((m, d), F32),
                   jax.ShapeDtypeStruct((m, d), BF16)],
        compiler_params=_cparams(("arbitrary",)),
        name="outproj",
    )(o_hg, o_gdn, w_out, w_out, x2, gt, post_w, ffn_w, sc, sh)


def _ffn_kernel(h_ref, w1_ref, w2_ref, x1_ref, gt_ref, pw_ref, o_ref):
    f = pl.program_id(1)

    @pl.when(f == 0)
    def _():
        o_ref[...] = jnp.zeros_like(o_ref)

    a = jnp.maximum(jnp.dot(h_ref[...], w1_ref[...], preferred_element_type=F32), 0.0)
    o_ref[...] += jnp.dot((a * a).astype(BF16), w2_ref[...], preferred_element_type=F32)

    @pl.when(f == pl.num_programs(1) - 1)
    def _():
        o_ref[...] = x1_ref[...] + gt_ref[0] * _rms(o_ref[...], pw_ref[...])


def _ffn(h2, w1, w2, x1, gt, post_w, seq, bm=512, bf=1024):
    m, d = h2.shape
    dff = w1.shape[1]
    per_b = seq // bm
    return pl.pallas_call(
        _ffn_kernel,
        grid=(m // bm, dff // bf),
        in_specs=[pl.BlockSpec((bm, d), lambda i, f: (i, 0)),
                  pl.BlockSpec((d, bf), lambda i, f: (0, f)),
                  pl.BlockSpec((bf, d), lambda i, f: (f, 0)),
                  pl.BlockSpec((bm, d), lambda i, f: (i, 0)),
                  pl.BlockSpec((1, 1, d), lambda i, f: (i // per_b, 0, 0)),
                  pl.BlockSpec((1, d), lambda i, f: (0, 0))],
        out_specs=pl.BlockSpec((bm, d), lambda i, f: (i, 0)),
        out_shape=jax.ShapeDtypeStruct((m, d), F32),
        compiler_params=_cparams(("arbitrary", "arbitrary")),
        name="ffn",
    )(h2, w1, w2, x1, gt, post_w)


def _layer(x, mod, pre_mix_w, post_mix_w, pre_ffn_w, post_ffn_w, w_in, lb_logits, hg_norm_w,
           conv_w, a_log, dt_bias, gdn_norm_w, w_out, w_ff1, w_ff2):
    batch, seq, d = x.shape
    hg_heads = lb_logits.shape[1]
    gdn_heads = a_log.shape[0]
    hg_cols = 4 * hg_heads * HEAD_DIM
    n_main = hg_cols + 4 * gdn_heads * HEAD_DIM

    x2 = x.reshape(batch * seq, d)
    sh_m, sc_m, gt_m, sh_f, sc_f, gt_f = [mod[:batch, None, i * d:(i + 1) * d] for i in range(6)]
    row = lambda v: v.reshape(1, -1)
    pad_lanes = lambda v: jnp.pad(v.reshape(1, -1), ((0, 0), (0, LANES - v.shape[-1])))

    w_in_t = w_in.T
    w_ab_t = jnp.pad(w_in_t[n_main:], ((0, LANES - (w_in_t.shape[0] - n_main)), (0, 0))).astype(BF16)
    proj, ab = _inproj(x2, row(pre_mix_w), sc_m, sh_m, w_in_t, w_ab_t, n_main, seq)

    o_hg, w_ff2_b, w_out_b = _hgrn(proj, lb_logits.reshape(lb_logits.shape[0], -1), row(hg_norm_w),
                                   [w_ff2, w_out], batch, seq, hg_heads, 0)
    o_gdn, w_ff1_b = _gdn(proj, ab, conv_w, pad_lanes(a_log), pad_lanes(dt_bias), row(gdn_norm_w),
                          [w_ff1], batch, seq, gdn_heads, hg_cols // HEAD_DIM)

    kh = hg_heads * HEAD_DIM
    x1, h2 = _outproj(o_hg, o_gdn, w_out_b, x2, gt_m, row(post_mix_w), row(pre_ffn_w), sc_f, sh_f,
                      seq, kh)
    out = _ffn(h2, w_ff1_b, w_ff2_b, x1, gt_f, row(post_ffn_w), seq)
    return out.reshape(batch, seq, d)


def kernel(x, c, w_ada, b_ada, pre_mix_norm, post_mix_norm, pre_ffn_norm, post_ffn_norm, w_in,
           hg_lb_logits, hg_norm, gdn_conv_w, gdn_a_log, gdn_dt_bias, gdn_norm, w_out, w_ff1, w_ff2):
    depth = w_ada.shape[0]
    assert depth == 1 and hg_lb_logits.shape[0] == 2, "single-layer configuration only"
    batch = c.shape[0]
    c_pad = jnp.pad(c, ((0, 8 - batch), (0, 0)))
    for l in range(depth):
        mod = _ada(c_pad, w_ada[l], b_ada[l][None, :])
        x = _layer(x, mod, pre_mix_norm[l], post_mix_norm[l], pre_ffn_norm[l], post_ffn_norm[l],
                   w_in[l], hg_lb_logits, hg_norm[l], gdn_conv_w[l], gdn_a_log[l], gdn_dt_bias[l],
                   gdn_norm[l], w_out[l], w_ff1[l], w_ff2[l])
    return x
```

```python
import functools

import jax
import jax.numpy as jnp
from jax import lax
from jax.experimental import pallas as pl
from jax.experimental.pallas import tpu as pltpu

F32 = jnp.float32
BF16 = jnp.bfloat16
EPS = 1e-6

LANES = 128
HEAD_DIM = 128
CHUNK = 64
SUB = 16
CONV_K = 4
HALO = 8
VMEM_LIMIT = 60 * 1024 * 1024


def _cparams(sem):
    return pltpu.CompilerParams(dimension_semantics=sem, vmem_limit_bytes=VMEM_LIMIT)


def _dot(a, b):
    return jnp.dot(a.astype(BF16), b.astype(BF16), preferred_element_type=F32)


def _dot_nt(a, b):
    return lax.dot_general(a.astype(BF16), b.astype(BF16), (((1,), (1,)), ((), ())),
                           preferred_element_type=F32)


def _dot_tn(a, b):
    return lax.dot_general(a.astype(BF16), b.astype(BF16), (((0,), (0,)), ((), ())),
                           preferred_element_type=F32)


def _split3(x):
    hi = x.astype(BF16)
    r = x - hi.astype(F32)
    mid = r.astype(BF16)
    lo = (r - mid.astype(F32)).astype(BF16)
    return hi, mid, lo


def _dot_exact_lhs(m_bf16, x):
    hi, mid, lo = _split3(x)
    d = lambda p: jnp.dot(m_bf16, p, preferred_element_type=F32)
    return d(hi) + d(mid) + d(lo)


def _dot_exact_rhs(x, m_bf16):
    hi, mid, lo = _split3(x)
    d = lambda p: jnp.dot(p, m_bf16, preferred_element_type=F32)
    return d(hi) + d(mid) + d(lo)


def _sigmoid(x):
    return 1.0 / (1.0 + jnp.exp(-x))


def _silu(x):
    return x * _sigmoid(x)


def _iota2(shape, dim):
    return lax.broadcasted_iota(jnp.int32, shape, dim)


def _cast_specs(arrays, n_steps, step_index):
    specs, shapes = [], []
    for a in arrays:
        rows, cols = a.shape
        specs.append(pl.BlockSpec((rows // n_steps, cols), lambda *g: (step_index(*g), 0)))
        shapes.append(jax.ShapeDtypeStruct((rows, cols), BF16))
    return specs, shapes


def _cast_slabs(src_refs, dst_refs):
    for src, dst in zip(src_refs, dst_refs):
        dst[...] = src[...].astype(BF16)


def _ada_kernel(c_ref, w_ref, b_ref, o_ref):
    c = c_ref[...]
    o_ref[...] = _dot(_silu(c), w_ref[...]) + b_ref[...]


def _ada(c_pad, w_ada, b_ada, bn=1024):
    rows, d = c_pad.shape
    n = w_ada.shape[1]
    return pl.pallas_call(
        _ada_kernel,
        grid=(n // bn,),
        in_specs=[pl.BlockSpec((rows, d), lambda j: (0, 0)),
                  pl.BlockSpec((d, bn), lambda j: (0, j)),
                  pl.BlockSpec((1, bn), lambda j: (0, j))],
        out_specs=pl.BlockSpec((rows, bn), lambda j: (0, j)),
        out_shape=jax.ShapeDtypeStruct((rows, n), F32),
        compiler_params=_cparams(("arbitrary",)),
        name="ada",
    )(c_pad, w_ada, b_ada)


def _inproj_kernel(x_ref, nw_ref, sc_ref, sh_ref, w_ref, wab_ref, o_ref, ab_ref, h_ref):
    @pl.when(pl.program_id(1) == 0)
    def _():
        def rows_body(r, carry):
            rows = pl.ds(pl.multiple_of(r * NORM_ROWS, NORM_ROWS), NORM_ROWS)
            x = x_ref[rows, :]
            y = x * lax.rsqrt(jnp.mean(x * x, axis=-1, keepdims=True) + EPS)
            h = (y * nw_ref[...]) * (1.0 + sc_ref[0]) + sh_ref[0]
            hb = h.astype(BF16)
            h_ref[rows, :] = hb
            ab_ref[rows, :] = _dot_nt(hb, wab_ref[...])
            return carry

        lax.fori_loop(0, x_ref.shape[0] // NORM_ROWS, rows_body, 0)

    o_ref[...] = _dot_nt(h_ref[...], w_ref[...])


NORM_ROWS = 256


def _inproj(x2, nw, sc, sh, w_t, w_ab_t, n, seq, bm=2048, bn=512):
    m, d = x2.shape
    per_b = seq // bm
    return pl.pallas_call(
        _inproj_kernel,
        grid=(m // bm, n // bn),
        in_specs=[pl.BlockSpec((bm, d), lambda i, j: (i, 0), pipeline_mode=pl.Buffered(1)),
                  pl.BlockSpec((1, d), lambda i, j: (0, 0)),
                  pl.BlockSpec((1, 1, d), lambda i, j: (i // per_b, 0, 0)),
                  pl.BlockSpec((1, 1, d), lambda i, j: (i // per_b, 0, 0)),
                  pl.BlockSpec((bn, d), lambda i, j: (j, 0)),
                  pl.BlockSpec((LANES, d), lambda i, j: (0, 0))],
        out_specs=[pl.BlockSpec((bm, bn), lambda i, j: (i, j)),
                   pl.BlockSpec((bm, LANES), lambda i, j: (i, 0))],
        out_shape=[jax.ShapeDtypeStruct((m, n), F32),
                   jax.ShapeDtypeStruct((m, LANES), F32)],
        scratch_shapes=[pltpu.VMEM((bm, d), BF16)],
        compiler_params=_cparams(("arbitrary", "arbitrary")),
        name="inproj",
    )(x2, nw, sc, sh, w_t, w_ab_t)


def _gated_norm(o, nw, g):
    y = o * lax.rsqrt(jnp.mean(o * o, axis=-1, keepdims=True) + EPS)
    return (y * nw) * _silu(g)


HGRN_ROWS = 4 * CHUNK


def _hgrn_kernel(lbl_ref, q_ref, f_ref, i_ref, g_ref, nw_ref, *rest, heads, tb, n_cast):
    cast_src, (o_ref, *cast_dst) = rest[:n_cast], rest[n_cast:2 * n_cast + 1]
    st_ref, oi_ref, qd_ref, kv_ref, dec_ref = rest[2 * n_cast + 1:]
    _cast_slabs(cast_src, cast_dst)
    n_chunks = tb // CHUNK
    cpi = HGRN_ROWS // CHUNK

    @pl.when(pl.program_id(1) == 0)
    def _():
        st_ref[...] = jnp.zeros_like(st_ref)

    l0 = lbl_ref[0:1, :]
    l1 = lbl_ref[1:2, :]
    mx = jnp.maximum(l0, l1)
    e0 = jnp.exp(l0 - mx)
    lb = e0 / (e0 + jnp.exp(l1 - mx))

    tri = (_iota2((CHUNK, CHUNK), 1) <= _iota2((CHUNK, CHUNK), 0)).astype(BF16)

    def phase1(i, carry):
        rows = pl.ds(pl.multiple_of(i * HGRN_ROWS, HGRN_ROWS), HGRN_ROWS)
        f = lb + (1.0 - lb) * _sigmoid(f_ref[rows, :])
        logf = jnp.log(f)
        bc = jnp.concatenate([_dot_exact_lhs(tri, logf[c * CHUNK:(c + 1) * CHUNK]) for c in range(cpi)],
                             axis=0)
        q3 = _problems(q_ref[rows, :], heads, cpi)
        k3 = _problems(1.0 - f, heads, cpi)
        b3 = _problems(bc, heads, cpi)
        v3 = _problems(i_ref[rows, :], heads, cpi).astype(BF16)

        parts = []
        for blk in range(CHUNK // SUB):
            r0 = blk * SUB
            n = r0 + SUB
            p = b3[:, r0 + SUB // 2:r0 + SUB // 2 + 1, :]
            qt = q3[:, r0:n] * jnp.exp(b3[:, r0:n] - p)
            kt = k3[:, 0:n] * jnp.exp(p - b3[:, 0:n])
            a = _bmm_nt(qt, kt)
            keep = _iota2((SUB, n), 1) <= _iota2((SUB, n), 0) + r0
            parts.append(_bmm(jnp.where(keep, a, 0.0), v3[:, 0:n]))
        b_last = b3[:, CHUNK - 1:CHUNK, :]
        ps = pl.ds(pl.multiple_of(i * (cpi * heads), cpi * heads), cpi * heads)
        oi_ref[ps] = jnp.concatenate(parts, axis=1)
        qd_ref[ps] = (q3 * jnp.exp(b3)).astype(BF16)
        kv_ref[ps] = _bmm_tn(v3, k3 * jnp.exp(b_last - b3))
        dec_ref[ps] = jnp.exp(b_last)
        return carry

    lax.fori_loop(0, n_chunks // cpi, phase1, 0)

    nw = nw_ref[...]

    def phase2(c, carry):
        ps = pl.ds(pl.multiple_of(c * heads, heads), heads)
        rows = pl.ds(pl.multiple_of(c * CHUNK, CHUNK), CHUNK)
        st = st_ref[...]
        o = oi_ref[ps] + _bmm_nt(qd_ref[ps], st)
        st_ref[...] = st * dec_ref[ps] + kv_ref[ps]
        for h in range(heads):
            cols = slice(h * HEAD_DIM, (h + 1) * HEAD_DIM)
            o_ref[rows, cols] = _gated_norm(o[h], nw, g_ref[rows, cols]).astype(o_ref.dtype)
        return carry

    lax.fori_loop(0, n_chunks, phase2, 0, unroll=2)


def _hgrn(proj, lb_logits2, norm_w, casts, batch, seq, heads, col0, tb=512):
    per_b = seq // tb
    cast_specs, cast_shapes = _cast_specs(casts, batch * per_b, lambda b, j: b * per_b + j)
    width = heads * HEAD_DIM
    cb0 = col0 // heads

    def col(group):
        return pl.BlockSpec((tb, width), lambda b, j, g=group: (b * per_b + j, cb0 + g))

    n_prob = (tb // CHUNK) * heads
    return pl.pallas_call(
        functools.partial(_hgrn_kernel, heads=heads, tb=tb, n_cast=len(casts)),
        grid=(batch, per_b),
        in_specs=[pl.BlockSpec(lb_logits2.shape, lambda b, j: (0, 0)),
                  col(0), col(1), col(2), col(3),
                  pl.BlockSpec((1, HEAD_DIM), lambda b, j: (0, 0))] + cast_specs,
        out_specs=[pl.BlockSpec((tb, width), lambda b, j: (b * per_b + j, 0))] + cast_specs,
        out_shape=[jax.ShapeDtypeStruct((batch * seq, width), BF16)] + cast_shapes,
        scratch_shapes=[pltpu.VMEM((heads, HEAD_DIM, HEAD_DIM), F32),
                        pltpu.VMEM((n_prob, CHUNK, HEAD_DIM), F32),
                        pltpu.VMEM((n_prob, CHUNK, HEAD_DIM), BF16),
                        pltpu.VMEM((n_prob, HEAD_DIM, HEAD_DIM), F32),
                        pltpu.VMEM((n_prob, 1, HEAD_DIM), F32)],
        compiler_params=_cparams(("arbitrary", "arbitrary")),
        name="hgrn",
    )(lb_logits2, proj, proj, proj, proj, norm_w, *casts)


def _softplus(x):
    return jnp.maximum(x, 0.0) + jnp.log(1.0 + jnp.exp(-jnp.abs(x)))


def _l2norm(x):
    return x * lax.rsqrt(jnp.sum(x * x, axis=-1, keepdims=True) + EPS)


def _bmm(a, b):
    return lax.dot_general(a.astype(BF16), b.astype(BF16), (((2,), (1,)), ((0,), (0,))),
                           preferred_element_type=F32)


def _bmm_nt(a, b):
    return lax.dot_general(a.astype(BF16), b.astype(BF16), (((2,), (2,)), ((0,), (0,))),
                           preferred_element_type=F32)


def _bmm_tn(a, b):
    return lax.dot_general(a.astype(BF16), b.astype(BF16), (((1,), (1,)), ((0,), (0,))),
                           preferred_element_type=F32)


def _conv_silu_tile(x_ref, halo_ref, w, r0, rb):
    cur = x_ref[pl.ds(r0, rb), :]
    prev = x_ref[pl.ds(pl.multiple_of(jnp.maximum(r0 - HALO, 0), HALO), HALO), :]
    prev = jnp.where(r0 == 0, halo_ref[...], prev)
    ext = jnp.concatenate([prev, cur], axis=0)
    acc = w[CONV_K - 1:CONV_K, :] * cur
    for j in range(CONV_K - 1):
        s = HALO - (CONV_K - 1) + j
        acc = acc + w[j:j + 1, :] * ext[s:s + rb, :]
    return _silu(acc)


def _problems(x, heads, chunks):
    return jnp.stack([x[c * CHUNK:(c + 1) * CHUNK, h * HEAD_DIM:(h + 1) * HEAD_DIM]
                      for c in range(chunks) for h in range(heads)], axis=0)


GDN_ROWS = 4 * CHUNK


def _gdn_kernel(q_ref, k_ref, v_ref, g_ref, ab_ref, wq_ref, wk_ref, wv_ref, alog_ref, dtb_ref,
                nw_ref, *rest, heads, tb, n_cast):
    cast_src, (o_ref, *cast_dst) = rest[:n_cast], rest[n_cast:2 * n_cast + 1]
    s_ref, hq_ref, hk_ref, hv_ref, u_ref, w_ref, qd_ref, kt_ref, qk_ref, tail_ref = rest[2 * n_cast + 1:]
    _cast_slabs(cast_src, cast_dst)
    n_chunks = tb // CHUNK
    cpi = GDN_ROWS // CHUNK

    @pl.when(pl.program_id(1) == 0)
    def _():
        s_ref[...] = jnp.zeros_like(s_ref)
        hq_ref[...] = jnp.zeros_like(hq_ref)
        hk_ref[...] = jnp.zeros_like(hk_ref)
        hv_ref[...] = jnp.zeros_like(hv_ref)

    width = heads * HEAD_DIM
    row = _iota2((CHUNK, CHUNK), 0)
    colm = _iota2((CHUNK, CHUNK), 1)
    tri = (colm <= row).astype(BF16)
    incl = colm <= row
    strict = colm < row
    sub_shift = SUB.bit_length() - 1
    same_blk = (row >> sub_shift) == (colm >> sub_shift)
    eye = (row == colm).astype(F32)
    src = _iota2((LANES, width), 0)
    dst_head = _iota2((LANES, width), 1) >> (HEAD_DIM.bit_length() - 1)
    sel_a = (src == dst_head).astype(BF16)
    sel_b = (src == dst_head + heads).astype(BF16)

    def phase1(i, carry):
        r0 = pl.multiple_of(i * GDN_ROWS, GDN_ROWS)
        qa = _conv_silu_tile(q_ref, hq_ref, wq_ref[...], r0, GDN_ROWS)
        ka = _conv_silu_tile(k_ref, hk_ref, wk_ref[...], r0, GDN_ROWS)
        va = _conv_silu_tile(v_ref, hv_ref, wv_ref[...], r0, GDN_ROWS)
        qn = jnp.concatenate([_l2norm(qa[:, h * HEAD_DIM:(h + 1) * HEAD_DIM]) for h in range(heads)],
                             axis=1) * (HEAD_DIM ** -0.5)
        kn = jnp.concatenate([_l2norm(ka[:, h * HEAD_DIM:(h + 1) * HEAD_DIM]) for h in range(heads)],
                             axis=1)

        ab = ab_ref[pl.ds(r0, GDN_ROWS), :]
        la = -jnp.exp(alog_ref[...]) * _softplus(ab + dtb_ref[...])
        be = _sigmoid(ab)
        gc = jnp.concatenate([_dot_exact_lhs(tri, la[c * CHUNK:(c + 1) * CHUNK]) for c in range(cpi)],
                             axis=0)
        g_rep = _dot_exact_rhs(gc, sel_a)
        be_rep = _dot_exact_rhs(be, sel_b)

        q3 = _problems(qn, heads, cpi)
        k3 = _problems(kn, heads, cpi)
        v3 = _problems(va, heads, cpi)
        g3 = _problems(g_rep, heads, cpi)
        b3 = _problems(be_rep, heads, cpi)

        g_sq = g3[:, :, :CHUNK]
        g_row = jnp.sum(g_sq * eye, axis=1, keepdims=True)
        gamma = jnp.exp(jnp.where(incl, g_sq - g_row, -jnp.inf))
        kb = k3.astype(BF16)
        mm = jnp.where(strict, b3[:, :, :CHUNK] * _bmm_nt(kb, kb) * gamma, 0.0)
        nd = jnp.where(same_blk, -mm, 0.0)
        off = jnp.where(same_blk, 0.0, mm)
        dinv = eye + nd
        pw = nd
        for _ in range(3):
            pw = _bmm(pw, pw)
            dinv = dinv + _bmm(dinv, pw)
        fm = -_bmm(dinv, off)
        ipf = eye + fm
        tinv = _bmm(ipf + _bmm(ipf, _bmm(fm, fm)), dinv)

        eg = jnp.exp(g3)
        uw = _bmm(tinv, jnp.concatenate([v3 * b3, k3 * (b3 * eg)], axis=2))
        g_last = g3[:, CHUNK - 1:CHUNK, :]
        ps = pl.ds(pl.multiple_of(i * (cpi * heads), cpi * heads), cpi * heads)
        u_ref[ps] = uw[:, :, :HEAD_DIM]
        w_ref[ps] = uw[:, :, HEAD_DIM:].astype(BF16)
        qd_ref[ps] = (q3 * eg).astype(BF16)
        kt_ref[ps] = (k3 * jnp.exp(g_last - g3)).astype(BF16)
        qk_ref[ps] = (_bmm_nt(q3, kb) * gamma).astype(BF16)
        tail_ref[ps] = jnp.exp(g_last)
        return carry

    lax.fori_loop(0, n_chunks // cpi, phase1, 0)
    hq_ref[...] = q_ref[tb - HALO:tb, :]
    hk_ref[...] = k_ref[tb - HALO:tb, :]
    hv_ref[...] = v_ref[tb - HALO:tb, :]

    nw = nw_ref[...]

    def phase2(c, carry):
        ps = pl.ds(pl.multiple_of(c * heads, heads), heads)
        rows = pl.ds(pl.multiple_of(c * CHUNK, CHUNK), CHUNK)
        s = s_ref[...]
        sb = s.astype(BF16)
        v_new = u_ref[ps] - _bmm(w_ref[ps], sb)
        vb = v_new.astype(BF16)
        o = _bmm(qd_ref[ps], sb) + _bmm(qk_ref[ps], vb)
        s_ref[...] = s * tail_ref[ps] + _bmm_tn(kt_ref[ps], vb)
        for h in range(heads):
            cols = slice(h * HEAD_DIM, (h + 1) * HEAD_DIM)
            o_ref[rows, cols] = _gated_norm(o[h], nw, g_ref[rows, cols]).astype(o_ref.dtype)
        return carry

    lax.fori_loop(0, n_chunks, phase2, 0, unroll=2)


def _gdn(proj, ab, conv_w, a_log_pad, dt_bias_pad, norm_w, casts, batch, seq, heads, col0, tb=512):
    per_b = seq // tb
    cast_specs, cast_shapes = _cast_specs(casts, batch * per_b, lambda b, j: b * per_b + j)
    width = heads * HEAD_DIM
    cb0 = col0 // heads

    def col(group):
        return pl.BlockSpec((tb, width), lambda b, j, g=group: (b * per_b + j, cb0 + g))

    def wcol(group):
        return pl.BlockSpec((CONV_K, width), lambda b, j, g=group: (0, g))

    small = pl.BlockSpec((1, LANES), lambda b, j: (0, 0))
    n_prob = (tb // CHUNK) * heads
    halo = pltpu.VMEM((HALO, width), F32)
    prob = lambda n, dt: pltpu.VMEM((n_prob, CHUNK, n), dt)
    return pl.pallas_call(
        functools.partial(_gdn_kernel, heads=heads, tb=tb, n_cast=len(casts)),
        grid=(batch, per_b),
        in_specs=[col(0), col(1), col(2), col(3),
                  pl.BlockSpec((tb, LANES), lambda b, j: (b * per_b + j, 0)),
                  wcol(0), wcol(1), wcol(2), small, small, small] + cast_specs,
        out_specs=[pl.BlockSpec((tb, width), lambda b, j: (b * per_b + j, 0))] + cast_specs,
        out_shape=[jax.ShapeDtypeStruct((batch * seq, width), BF16)] + cast_shapes,
        scratch_shapes=[pltpu.VMEM((heads, HEAD_DIM, HEAD_DIM), F32), halo, halo, halo,
                        prob(HEAD_DIM, F32), prob(HEAD_DIM, BF16), prob(HEAD_DIM, BF16),
                        prob(HEAD_DIM, BF16), prob(CHUNK, BF16),
                        pltpu.VMEM((n_prob, 1, HEAD_DIM), F32)],
        compiler_params=_cparams(("arbitrary", "arbitrary")),
        name="gdn",
    )(proj, proj, proj, proj, ab, conv_w, conv_w, conv_w, a_log_pad, dt_bias_pad, norm_w, *casts)


def _rms(y, w):
    return (y * lax.rsqrt(jnp.mean(y * y, axis=-1, keepdims=True) + EPS)) * w


OUT_ROWS = 256


def _outproj_kernel(oh_ref, og_ref, wh_ref, wg_ref, x_ref, gt_ref, pw_ref, fw_ref, sc_ref, sh_ref,
                    x1_ref, h2_ref):
    for r in range(oh_ref.shape[0] // OUT_ROWS):
        rows = pl.ds(r * OUT_ROWS, OUT_ROWS)
        y = (jnp.dot(oh_ref[rows, :], wh_ref[...], preferred_element_type=F32)
             + jnp.dot(og_ref[rows, :], wg_ref[...], preferred_element_type=F32))
        x1 = x_ref[rows, :] + gt_ref[0] * _rms(y, pw_ref[...])
        x1_ref[rows, :] = x1
        h2_ref[rows, :] = (_rms(x1, fw_ref[...]) * (1.0 + sc_ref[0]) + sh_ref[0]).astype(BF16)


def _outproj(o_hg, o_gdn, w_out, x2, gt, post_w, ffn_w, sc, sh, seq, kh, bm=512):
    m, d = x2.shape
    per_b = seq // bm
    vec = pl.BlockSpec((1, d), lambda i: (0, 0))
    mod = pl.BlockSpec((1, 1, d), lambda i: (i // per_b, 0, 0))
    return pl.pallas_call(
        _outproj_kernel,
        grid=(m // bm,),
        in_specs=[pl.BlockSpec((bm, kh), lambda i: (i, 0)),
                  pl.BlockSpec((bm, kh), lambda i: (i, 0)),
                  pl.BlockSpec((kh, d), lambda i: (0, 0)),
                  pl.BlockSpec((kh, d), lambda i: (1, 0)),
                  pl.BlockSpec((bm, d), lambda i: (i, 0)),
                  mod, vec, vec, mod, mod],
        out_specs=[pl.BlockSpec((bm, d), lambda i: (i, 0)),
                   pl.BlockSpec((bm, d), lambda i: (i, 0))],
        out_shape=[jax.ShapeDtypeStruct((m, d), F32),
                   jax.ShapeDtypeStruct((m, d), BF16)],
        compiler_params=_cparams(("arbitrary",)),
        name="outproj",
    )(o_hg, o_gdn, w_out, w_out, x2, gt, post_w, ffn_w, sc, sh)


def _ffn_kernel(h_ref, w1_ref, w2_ref, x1_ref, gt_ref, pw_ref, o_ref):
    f = pl.program_id(1)

    @pl.when(f == 0)
    def _():
        o_ref[...] = jnp.zeros_like(o_ref)

    a = jnp.maximum(jnp.dot(h_ref[...], w1_ref[...], preferred_element_type=F32), 0.0)
    o_ref[...] += jnp.dot((a * a).astype(BF16), w2_ref[...], preferred_element_type=F32)

    @pl.when(f == pl.num_programs(1) - 1)
    def _():
        o_ref[...] = x1_ref[...] + gt_ref[0] * _rms(o_ref[...], pw_ref[...])


def _ffn(h2, w1, w2, x1, gt, post_w, seq, bm=512, bf=2048):
    m, d = h2.shape
    dff = w1.shape[1]
    per_b = seq // bm
    return pl.pallas_call(
        _ffn_kernel,
        grid=(m // bm, dff // bf),
        in_specs=[pl.BlockSpec((bm, d), lambda i, f: (i, 0)),
                  pl.BlockSpec((d, bf), lambda i, f: (0, f)),
                  pl.BlockSpec((bf, d), lambda i, f: (f, 0)),
                  pl.BlockSpec((bm, d), lambda i, f: (i, 0)),
                  pl.BlockSpec((1, 1, d), lambda i, f: (i // per_b, 0, 0)),
                  pl.BlockSpec((1, d), lambda i, f: (0, 0))],
        out_specs=pl.BlockSpec((bm, d), lambda i, f: (i, 0)),
        out_shape=jax.ShapeDtypeStruct((m, d), F32),
        compiler_params=_cparams(("arbitrary", "arbitrary")),
        name="ffn",
    )(h2, w1, w2, x1, gt, post_w)


def _layer(x, mod, pre_mix_w, post_mix_w, pre_ffn_w, post_ffn_w, w_in, lb_logits, hg_norm_w,
           conv_w, a_log, dt_bias, gdn_norm_w, w_out, w_ff1, w_ff2):
    batch, seq, d = x.shape
    hg_heads = lb_logits.shape[1]
    gdn_heads = a_log.shape[0]
    hg_cols = 4 * hg_heads * HEAD_DIM
    n_main = hg_cols + 4 * gdn_heads * HEAD_DIM

    x2 = x.reshape(batch * seq, d)
    sh_m, sc_m, gt_m, sh_f, sc_f, gt_f = [mod[:batch, None, i * d:(i + 1) * d] for i in range(6)]
    row = lambda v: v.reshape(1, -1)
    pad_lanes = lambda v: jnp.pad(v.reshape(1, -1), ((0, 0), (0, LANES - v.shape[-1])))

    w_in_t = w_in.T
    w_ab_t = jnp.pad(w_in_t[n_main:], ((0, LANES - (w_in_t.shape[0] - n_main)), (0, 0))).astype(BF16)
    proj, ab = _inproj(x2, row(pre_mix_w), sc_m, sh_m, w_in_t, w_ab_t, n_main, seq)

    o_hg, w_ff2_b, w_out_b = _hgrn(proj, lb_logits.reshape(lb_logits.shape[0], -1), row(hg_norm_w),
                                   [w_ff2, w_out], batch, seq, hg_heads, 0)
    o_gdn, w_ff1_b = _gdn(proj, ab, conv_w, pad_lanes(a_log), pad_lanes(dt_bias), row(gdn_norm_w),
                          [w_ff1], batch, seq, gdn_heads, hg_cols // HEAD_DIM)

    kh = hg_heads * HEAD_DIM
    x1, h2 = _outproj(o_hg, o_gdn, w_out_b, x2, gt_m, row(post_mix_w), row(pre_ffn_w), sc_f, sh_f,
                      seq, kh)
    out = _ffn(h2, w_ff1_b, w_ff2_b, x1, gt_f, row(post_ffn_w), seq)
    return out.reshape(batch, seq, d)


def kernel(x, c, w_ada, b_ada, pre_mix_norm, post_mix_norm, pre_ffn_norm, post_ffn_norm, w_in,
           hg_lb_logits, hg_norm, gdn_conv_w, gdn_a_log, gdn_dt_bias, gdn_norm, w_out, w_ff1, w_ff2):
    depth = w_ada.shape[0]
    assert depth == 1 and hg_lb_logits.shape[0] == 2, "single-layer configuration only"
    batch = c.shape[0]
    c_pad = jnp.pad(c, ((0, 8 - batch), (0, 0)))
    for l in range(depth):
        mod = _ada(c_pad, w_ada[l], b_ada[l][None, :])
        x = _layer(x, mod, pre_mix_norm[l], post_mix_norm[l], pre_ffn_norm[l], post_ffn_norm[l],
                   w_in[l], hg_lb_logits, hg_norm[l], gdn_conv_w[l], gdn_a_log[l], gdn_dt_bias[l],
                   gdn_norm[l], w_out[l], w_ff1[l], w_ff2[l])
    return x
```

```python
import functools

import jax
import jax.numpy as jnp
from jax import lax
from jax.experimental import pallas as pl
from jax.experimental.pallas import tpu as pltpu

F32 = jnp.float32
BF16 = jnp.bfloat16
EPS = 1e-6

LANES = 128
HEAD_DIM = 128
CHUNK = 64
SUB = 16
CONV_K = 4
HALO = 8
VMEM_LIMIT = 60 * 1024 * 1024


def _cparams(sem):
    return pltpu.CompilerParams(dimension_semantics=sem, vmem_limit_bytes=VMEM_LIMIT)


def _dot(a, b):
    return jnp.dot(a.astype(BF16), b.astype(BF16), preferred_element_type=F32)


def _dot_nt(a, b):
    return lax.dot_general(a.astype(BF16), b.astype(BF16), (((1,), (1,)), ((), ())),
                           preferred_element_type=F32)


def _dot_tn(a, b):
    return lax.dot_general(a.astype(BF16), b.astype(BF16), (((0,), (0,)), ((), ())),
                           preferred_element_type=F32)


def _split3(x):
    hi = x.astype(BF16)
    r = x - hi.astype(F32)
    mid = r.astype(BF16)
    lo = (r - mid.astype(F32)).astype(BF16)
    return hi, mid, lo


def _dot_exact_lhs(m_bf16, x):
    hi, mid, lo = _split3(x)
    d = lambda p: jnp.dot(m_bf16, p, preferred_element_type=F32)
    return d(hi) + d(mid) + d(lo)


def _dot_exact_rhs(x, m_bf16):
    hi, mid, lo = _split3(x)
    d = lambda p: jnp.dot(p, m_bf16, preferred_element_type=F32)
    return d(hi) + d(mid) + d(lo)


def _sigmoid(x):
    return 1.0 / (1.0 + jnp.exp(-x))


def _silu(x):
    return x * _sigmoid(x)


def _aligned(x, m):
    return x if isinstance(x, int) else pl.multiple_of(x, m)


def _iota2(shape, dim):
    return lax.broadcasted_iota(jnp.int32, shape, dim)


def _cast_specs(arrays, n_steps, step_index):
    specs, shapes = [], []
    for a in arrays:
        rows, cols = a.shape
        specs.append(pl.BlockSpec((rows // n_steps, cols), lambda *g: (step_index(*g), 0)))
        shapes.append(jax.ShapeDtypeStruct((rows, cols), BF16))
    return specs, shapes


def _cast_slabs(src_refs, dst_refs):
    for src, dst in zip(src_refs, dst_refs):
        dst[...] = src[...].astype(BF16)


def _ada_kernel(c_ref, w_ref, b_ref, o_ref):
    c = c_ref[...]
    o_ref[...] = _dot(_silu(c), w_ref[...]) + b_ref[...]


def _ada(c_pad, w_ada, b_ada, bn=1024):
    rows, d = c_pad.shape
    n = w_ada.shape[1]
    return pl.pallas_call(
        _ada_kernel,
        grid=(n // bn,),
        in_specs=[pl.BlockSpec((rows, d), lambda j: (0, 0)),
                  pl.BlockSpec((d, bn), lambda j: (0, j)),
                  pl.BlockSpec((1, bn), lambda j: (0, j))],
        out_specs=pl.BlockSpec((rows, bn), lambda j: (0, j)),
        out_shape=jax.ShapeDtypeStruct((rows, n), F32),
        compiler_params=_cparams(("arbitrary",)),
        name="ada",
    )(c_pad, w_ada, b_ada)


def _prep_kernel(x_ref, nw_ref, sc_ref, sh_ref, wab_ref, h_ref, ab_ref):
    x = x_ref[...]
    y = x * lax.rsqrt(jnp.mean(x * x, axis=-1, keepdims=True) + EPS)
    hb = ((y * nw_ref[...]) * (1.0 + sc_ref[0]) + sh_ref[0]).astype(BF16)
    h_ref[...] = hb
    ab_ref[...] = _dot_nt(hb, wab_ref[...])


def _prep(x2, nw, sc, sh, w_ab_t, seq, bm=512):
    m, d = x2.shape
    per_b = seq // bm
    mod = pl.BlockSpec((1, 1, d), lambda i: (i // per_b, 0, 0))
    return pl.pallas_call(
        _prep_kernel,
        grid=(m // bm,),
        in_specs=[pl.BlockSpec((bm, d), lambda i: (i, 0)),
                  pl.BlockSpec((1, d), lambda i: (0, 0)), mod, mod,
                  pl.BlockSpec((LANES, d), lambda i: (0, 0))],
        out_specs=[pl.BlockSpec((bm, d), lambda i: (i, 0)),
                   pl.BlockSpec((bm, LANES), lambda i: (i, 0))],
        out_shape=[jax.ShapeDtypeStruct((m, d), BF16),
                   jax.ShapeDtypeStruct((m, LANES), F32)],
        compiler_params=_cparams(("arbitrary",)),
        name="prep",
    )(x2, nw, sc, sh, w_ab_t)


def _inproj_kernel(h_ref, w_ref, o_ref):
    o_ref[...] = _dot_nt(h_ref[...], w_ref[...])


def _inproj(h, w_t, n, bm=2048, bn=1024):
    m, d = h.shape
    return pl.pallas_call(
        _inproj_kernel,
        grid=(m // bm, n // bn),
        in_specs=[pl.BlockSpec((bm, d), lambda i, j: (i, 0)),
                  pl.BlockSpec((bn, d), lambda i, j: (j, 0))],
        out_specs=pl.BlockSpec((bm, bn), lambda i, j: (i, j)),
        out_shape=jax.ShapeDtypeStruct((m, n), F32),
        compiler_params=_cparams(("arbitrary", "arbitrary")),
        name="inproj",
    )(h, w_t)


def _gated_norm(o, nw, g):
    y = o * lax.rsqrt(jnp.mean(o * o, axis=-1, keepdims=True) + EPS)
    return (y * nw) * _silu(g)


HGRN_ROWS = 4 * CHUNK


def _hgrn_kernel(lbl_ref, q_ref, f_ref, i_ref, g_ref, nw_ref, *rest, heads, tb, n_cast):
    cast_src, (o_ref, *cast_dst) = rest[:n_cast], rest[n_cast:2 * n_cast + 1]
    st_ref, oi_ref, qd_ref, kv_ref, dec_ref = rest[2 * n_cast + 1:]
    _cast_slabs(cast_src, cast_dst)
    n_chunks = tb // CHUNK
    cpi = HGRN_ROWS // CHUNK

    @pl.when(pl.program_id(1) == 0)
    def _():
        st_ref[...] = jnp.zeros_like(st_ref)

    l0 = lbl_ref[0:1, :]
    l1 = lbl_ref[1:2, :]
    mx = jnp.maximum(l0, l1)
    e0 = jnp.exp(l0 - mx)
    lb = e0 / (e0 + jnp.exp(l1 - mx))

    tri = (_iota2((CHUNK, CHUNK), 1) <= _iota2((CHUNK, CHUNK), 0)).astype(BF16)

    def phase1(i, carry):
        rows = pl.ds(_aligned(i * HGRN_ROWS, HGRN_ROWS), HGRN_ROWS)
        f = lb + (1.0 - lb) * _sigmoid(f_ref[rows, :])
        logf = jnp.log(f)
        bc = jnp.concatenate([_dot_exact_lhs(tri, logf[c * CHUNK:(c + 1) * CHUNK]) for c in range(cpi)],
                             axis=0)
        q3 = _problems(q_ref[rows, :], heads, cpi)
        k3 = _problems(1.0 - f, heads, cpi)
        b3 = _problems(bc, heads, cpi)
        v3 = _problems(i_ref[rows, :], heads, cpi).astype(BF16)

        parts = []
        for blk in range(CHUNK // SUB):
            r0 = blk * SUB
            n = r0 + SUB
            p = b3[:, r0 + SUB // 2:r0 + SUB // 2 + 1, :]
            qt = q3[:, r0:n] * jnp.exp(b3[:, r0:n] - p)
            kt = k3[:, 0:n] * jnp.exp(p - b3[:, 0:n])
            a = _bmm_nt(qt, kt)
            keep = _iota2((SUB, n), 1) <= _iota2((SUB, n), 0) + r0
            parts.append(_bmm(jnp.where(keep, a, 0.0), v3[:, 0:n]))
        b_last = b3[:, CHUNK - 1:CHUNK, :]
        ps = pl.ds(_aligned(i * (cpi * heads), cpi * heads), cpi * heads)
        oi_ref[ps] = jnp.concatenate(parts, axis=1)
        qd_ref[ps] = (q3 * jnp.exp(b3)).astype(BF16)
        kv_ref[ps] = _bmm_tn(v3, k3 * jnp.exp(b_last - b3))
        dec_ref[ps] = jnp.exp(b_last)
        return carry

    lax.fori_loop(0, n_chunks // cpi, phase1, 0)

    nw = nw_ref[...]

    def phase2(c, carry):
        ps = pl.ds(_aligned(c * heads, heads), heads)
        rows = pl.ds(_aligned(c * CHUNK, CHUNK), CHUNK)
        st = st_ref[...]
        o = oi_ref[ps] + _bmm_nt(qd_ref[ps], st)
        st_ref[...] = st * dec_ref[ps] + kv_ref[ps]
        for h in range(heads):
            cols = slice(h * HEAD_DIM, (h + 1) * HEAD_DIM)
            o_ref[rows, cols] = _gated_norm(o[h], nw, g_ref[rows, cols]).astype(o_ref.dtype)
        return carry

    lax.fori_loop(0, n_chunks, phase2, 0, unroll=2)


def _hgrn(proj, lb_logits2, norm_w, casts, batch, seq, heads, col0, tb=512):
    per_b = seq // tb
    cast_specs, cast_shapes = _cast_specs(casts, batch * per_b, lambda b, j: b * per_b + j)
    width = heads * HEAD_DIM
    cb0 = col0 // heads

    def col(group):
        return pl.BlockSpec((tb, width), lambda b, j, g=group: (b * per_b + j, cb0 + g))

    n_prob = (tb // CHUNK) * heads
    return pl.pallas_call(
        functools.partial(_hgrn_kernel, heads=heads, tb=tb, n_cast=len(casts)),
        grid=(batch, per_b),
        in_specs=[pl.BlockSpec(lb_logits2.shape, lambda b, j: (0, 0)),
                  col(0), col(1), col(2), col(3),
                  pl.BlockSpec((1, HEAD_DIM), lambda b, j: (0, 0))] + cast_specs,
        out_specs=[pl.BlockSpec((tb, width), lambda b, j: (b * per_b + j, 0))] + cast_specs,
        out_shape=[jax.ShapeDtypeStruct((batch * seq, width), BF16)] + cast_shapes,
        scratch_shapes=[pltpu.VMEM((heads, HEAD_DIM, HEAD_DIM), F32),
                        pltpu.VMEM((n_prob, CHUNK, HEAD_DIM), F32),
                        pltpu.VMEM((n_prob, CHUNK, HEAD_DIM), BF16),
                        pltpu.VMEM((n_prob, HEAD_DIM, HEAD_DIM), F32),
                        pltpu.VMEM((n_prob, 1, HEAD_DIM), F32)],
        compiler_params=_cparams(("arbitrary", "arbitrary")),
        name="hgrn",
    )(lb_logits2, proj, proj, proj, proj, norm_w, *casts)


def _softplus(x):
    return jnp.maximum(x, 0.0) + jnp.log(1.0 + jnp.exp(-jnp.abs(x)))


def _l2norm(x, scale=1.0):
    return x * (lax.rsqrt(jnp.sum(x * x, axis=-1, keepdims=True) + EPS) * scale)


def _bmm(a, b):
    return lax.dot_general(a.astype(BF16), b.astype(BF16), (((2,), (1,)), ((0,), (0,))),
                           preferred_element_type=F32)


def _bmm_nt(a, b):
    return lax.dot_general(a.astype(BF16), b.astype(BF16), (((2,), (2,)), ((0,), (0,))),
                           preferred_element_type=F32)


def _bmm_tn(a, b):
    return lax.dot_general(a.astype(BF16), b.astype(BF16), (((1,), (1,)), ((0,), (0,))),
                           preferred_element_type=F32)


def _conv_silu_tile(x_ref, halo_ref, w, r0, rb):
    cur = x_ref[pl.ds(r0, rb), :]
    prev = x_ref[pl.ds(_aligned(jnp.maximum(r0 - HALO, 0), HALO), HALO), :]
    prev = jnp.where(r0 == 0, halo_ref[...], prev)
    ext = jnp.concatenate([prev, cur], axis=0)
    acc = w[CONV_K - 1:CONV_K, :] * cur
    for j in range(CONV_K - 1):
        s = HALO - (CONV_K - 1) + j
        acc = acc + w[j:j + 1, :] * ext[s:s + rb, :]
    return _silu(acc)


def _problems(x, heads, chunks):
    return jnp.stack([x[c * CHUNK:(c + 1) * CHUNK, h * HEAD_DIM:(h + 1) * HEAD_DIM]
                      for c in range(chunks) for h in range(heads)], axis=0)


GDN_ROWS = 4 * CHUNK


def _gdn_kernel(q_ref, k_ref, v_ref, g_ref, ab_ref, wq_ref, wk_ref, wv_ref, alog_ref, dtb_ref,
                nw_ref, *rest, heads, tb, n_cast):
    cast_src, (o_ref, *cast_dst) = rest[:n_cast], rest[n_cast:2 * n_cast + 1]
    s_ref, hq_ref, hk_ref, hv_ref, u_ref, w_ref, qd_ref, kt_ref, qk_ref, tail_ref = rest[2 * n_cast + 1:]
    _cast_slabs(cast_src, cast_dst)
    n_chunks = tb // CHUNK
    cpi = GDN_ROWS // CHUNK

    @pl.when(pl.program_id(1) == 0)
    def _():
        s_ref[...] = jnp.zeros_like(s_ref)
        hq_ref[...] = jnp.zeros_like(hq_ref)
        hk_ref[...] = jnp.zeros_like(hk_ref)
        hv_ref[...] = jnp.zeros_like(hv_ref)

    width = heads * HEAD_DIM
    row = _iota2((CHUNK, CHUNK), 0)
    colm = _iota2((CHUNK, CHUNK), 1)
    tri = (colm <= row).astype(BF16)
    incl = colm <= row
    strict = colm < row
    sub_shift = SUB.bit_length() - 1
    same_blk = (row >> sub_shift) == (colm >> sub_shift)
    eye = (row == colm).astype(F32)
    neg_diag_blk = jnp.where(same_blk & strict, -1.0, 0.0)
    off_blk = jnp.where(strict & ~same_blk, 1.0, 0.0)
    src = _iota2((LANES, width), 0)
    dst_head = _iota2((LANES, width), 1) >> (HEAD_DIM.bit_length() - 1)
    sel_a = (src == dst_head).astype(BF16)
    sel_b = (src == dst_head + heads).astype(BF16)

    def phase1(i, carry):
        r0 = _aligned(i * GDN_ROWS, GDN_ROWS)
        qa = _conv_silu_tile(q_ref, hq_ref, wq_ref[...], r0, GDN_ROWS)
        ka = _conv_silu_tile(k_ref, hk_ref, wk_ref[...], r0, GDN_ROWS)
        va = _conv_silu_tile(v_ref, hv_ref, wv_ref[...], r0, GDN_ROWS)
        qn = jnp.concatenate([_l2norm(qa[:, h * HEAD_DIM:(h + 1) * HEAD_DIM], HEAD_DIM ** -0.5)
                              for h in range(heads)], axis=1)
        kn = jnp.concatenate([_l2norm(ka[:, h * HEAD_DIM:(h + 1) * HEAD_DIM]) for h in range(heads)],
                             axis=1)

        ab = ab_ref[pl.ds(r0, GDN_ROWS), :]
        la = -jnp.exp(alog_ref[...]) * _softplus(ab + dtb_ref[...])
        be = _sigmoid(ab)
        gc = jnp.concatenate([_dot_exact_lhs(tri, la[c * CHUNK:(c + 1) * CHUNK]) for c in range(cpi)],
                             axis=0)
        g_rep = _dot_exact_rhs(gc, sel_a)
        be_rep = _dot_exact_rhs(be, sel_b)

        q3 = _problems(qn, heads, cpi)
        k3 = _problems(kn, heads, cpi)
        v3 = _problems(va, heads, cpi)
        g3 = _problems(g_rep, heads, cpi)
        b3 = _problems(be_rep, heads, cpi)

        g_sq = g3[:, :, :CHUNK]
        g_row = jnp.sum(g_sq * eye, axis=1, keepdims=True)
        gamma = jnp.exp(jnp.where(incl, g_sq - g_row, -jnp.inf))
        kb = k3.astype(BF16)
        mm = _bmm_nt(kb, kb) * (b3[:, :, :CHUNK] * gamma)
        nd = mm * neg_diag_blk
        off = mm * off_blk
        dinv = eye + nd
        pw = nd
        for _ in range(3):
            pw = _bmm(pw, pw)
            dinv = dinv + _bmm(dinv, pw)
        fm = -_bmm(dinv, off)
        ipf = eye + fm
        tinv = _bmm(ipf + _bmm(ipf, _bmm(fm, fm)), dinv)

        eg = jnp.exp(g3)
        uw = _bmm(tinv, jnp.concatenate([v3 * b3, k3 * (b3 * eg)], axis=2))
        g_last = g3[:, CHUNK - 1:CHUNK, :]
        ps = pl.ds(_aligned(i * (cpi * heads), cpi * heads), cpi * heads)
        u_ref[ps] = uw[:, :, :HEAD_DIM]
        w_ref[ps] = uw[:, :, HEAD_DIM:].astype(BF16)
        qd_ref[ps] = (q3 * eg).astype(BF16)
        kt_ref[ps] = (k3 * jnp.exp(g_last - g3)).astype(BF16)
        qk_ref[ps] = (_bmm_nt(q3, kb) * gamma).astype(BF16)
        tail_ref[ps] = jnp.exp(g_last)
        return carry

    lax.fori_loop(0, n_chunks // cpi, phase1, 0)
    hq_ref[...] = q_ref[tb - HALO:tb, :]
    hk_ref[...] = k_ref[tb - HALO:tb, :]
    hv_ref[...] = v_ref[tb - HALO:tb, :]

    nw = nw_ref[...]

    def phase2(c, carry):
        ps = pl.ds(_aligned(c * heads, heads), heads)
        rows = pl.ds(_aligned(c * CHUNK, CHUNK), CHUNK)
        s = s_ref[...]
        sb = s.astype(BF16)
        v_new = u_ref[ps] - _bmm(w_ref[ps], sb)
        vb = v_new.astype(BF16)
        o = _bmm(qd_ref[ps], sb) + _bmm(qk_ref[ps], vb)
        s_ref[...] = s * tail_ref[ps] + _bmm_tn(kt_ref[ps], vb)
        for h in range(heads):
            cols = slice(h * HEAD_DIM, (h + 1) * HEAD_DIM)
            o_ref[rows, cols] = _gated_norm(o[h], nw, g_ref[rows, cols]).astype(o_ref.dtype)
        return carry

    lax.fori_loop(0, n_chunks, phase2, 0, unroll=2)


def _gdn(proj, ab, conv_w, a_log_pad, dt_bias_pad, norm_w, casts, batch, seq, heads, col0, tb=512):
    per_b = seq // tb
    cast_specs, cast_shapes = _cast_specs(casts, batch * per_b, lambda b, j: b * per_b + j)
    width = heads * HEAD_DIM
    cb0 = col0 // heads

    def col(group):
        return pl.BlockSpec((tb, width), lambda b, j, g=group: (b * per_b + j, cb0 + g))

    def wcol(group):
        return pl.BlockSpec((CONV_K, width), lambda b, j, g=group: (0, g))

    small = pl.BlockSpec((1, LANES), lambda b, j: (0, 0))
    n_prob = (tb // CHUNK) * heads
    halo = pltpu.VMEM((HALO, width), F32)
    prob = lambda n, dt: pltpu.VMEM((n_prob, CHUNK, n), dt)
    return pl.pallas_call(
        functools.partial(_gdn_kernel, heads=heads, tb=tb, n_cast=len(casts)),
        grid=(batch, per_b),
        in_specs=[col(0), col(1), col(2), col(3),
                  pl.BlockSpec((tb, LANES), lambda b, j: (b * per_b + j, 0)),
                  wcol(0), wcol(1), wcol(2), small, small, small] + cast_specs,
        out_specs=[pl.BlockSpec((tb, width), lambda b, j: (b * per_b + j, 0))] + cast_specs,
        out_shape=[jax.ShapeDtypeStruct((batch * seq, width), BF16)] + cast_shapes,
        scratch_shapes=[pltpu.VMEM((heads, HEAD_DIM, HEAD_DIM), F32), halo, halo, halo,
                        prob(HEAD_DIM, F32), prob(HEAD_DIM, BF16), prob(HEAD_DIM, BF16),
                        prob(HEAD_DIM, BF16), prob(CHUNK, BF16),
                        pltpu.VMEM((n_prob, 1, HEAD_DIM), F32)],
        compiler_params=_cparams(("arbitrary", "arbitrary")),
        name="gdn",
    )(proj, proj, proj, proj, ab, conv_w, conv_w, conv_w, a_log_pad, dt_bias_pad, norm_w, *casts)


def _rms(y, w):
    return (y * lax.rsqrt(jnp.mean(y * y, axis=-1, keepdims=True) + EPS)) * w


OUT_ROWS = 256


def _outproj_kernel(oh_ref, og_ref, wh_ref, wg_ref, x_ref, gt_ref, pw_ref, fw_ref, sc_ref, sh_ref,
                    x1_ref, h2_ref):
    for r in range(oh_ref.shape[0] // OUT_ROWS):
        rows = pl.ds(r * OUT_ROWS, OUT_ROWS)
        y = (jnp.dot(oh_ref[rows, :], wh_ref[...], preferred_element_type=F32)
             + jnp.dot(og_ref[rows, :], wg_ref[...], preferred_element_type=F32))
        x1 = x_ref[rows, :] + gt_ref[0] * _rms(y, pw_ref[...])
        x1_ref[rows, :] = x1
        h2_ref[rows, :] = (_rms(x1, fw_ref[...]) * (1.0 + sc_ref[0]) + sh_ref[0]).astype(BF16)


def _outproj(o_hg, o_gdn, w_out, x2, gt, post_w, ffn_w, sc, sh, seq, kh, bm=512):
    m, d = x2.shape
    per_b = seq // bm
    vec = pl.BlockSpec((1, d), lambda i: (0, 0))
    mod = pl.BlockSpec((1, 1, d), lambda i: (i // per_b, 0, 0))
    return pl.pallas_call(
        _outproj_kernel,
        grid=(m // bm,),
        in_specs=[pl.BlockSpec((bm, kh), lambda i: (i, 0)),
                  pl.BlockSpec((bm, kh), lambda i: (i, 0)),
                  pl.BlockSpec((kh, d), lambda i: (0, 0)),
                  pl.BlockSpec((kh, d), lambda i: (1, 0)),
                  pl.BlockSpec((bm, d), lambda i: (i, 0)),
                  mod, vec, vec, mod, mod],
        out_specs=[pl.BlockSpec((bm, d), lambda i: (i, 0)),
                   pl.BlockSpec((bm, d), lambda i: (i, 0))],
        out_shape=[jax.ShapeDtypeStruct((m, d), F32),
                   jax.ShapeDtypeStruct((m, d), BF16)],
        compiler_params=_cparams(("arbitrary",)),
        name="outproj",
    )(o_hg, o_gdn, w_out, w_out, x2, gt, post_w, ffn_w, sc, sh)


def _ffn_kernel(h_ref, w1_ref, w2_ref, x1_ref, gt_ref, pw_ref, o_ref):
    f = pl.program_id(1)

    @pl.when(f == 0)
    def _():
        o_ref[...] = jnp.zeros_like(o_ref)

    a = jnp.maximum(jnp.dot(h_ref[...], w1_ref[...], preferred_element_type=F32), 0.0)
    o_ref[...] += jnp.dot((a * a).astype(BF16), w2_ref[...], preferred_element_type=F32)

    @pl.when(f == pl.num_programs(1) - 1)
    def _():
        o_ref[...] = x1_ref[...] + gt_ref[0] * _rms(o_ref[...], pw_ref[...])


def _ffn(h2, w1, w2, x1, gt, post_w, seq, bm=512, bf=2048):
    m, d = h2.shape
    dff = w1.shape[1]
    per_b = seq // bm
    return pl.pallas_call(
        _ffn_kernel,
        grid=(m // bm, dff // bf),
        in_specs=[pl.BlockSpec((bm, d), lambda i, f: (i, 0)),
                  pl.BlockSpec((d, bf), lambda i, f: (0, f)),
                  pl.BlockSpec((bf, d), lambda i, f: (f, 0)),
                  pl.BlockSpec((bm, d), lambda i, f: (i, 0)),
                  pl.BlockSpec((1, 1, d), lambda i, f: (i // per_b, 0, 0)),
                  pl.BlockSpec((1, d), lambda i, f: (0, 0))],
        out_specs=pl.BlockSpec((bm, d), lambda i, f: (i, 0)),
        out_shape=jax.ShapeDtypeStruct((m, d), F32),
        compiler_params=_cparams(("arbitrary", "arbitrary")),
        name="ffn",
    )(h2, w1, w2, x1, gt, post_w)


def _layer(x, mod, pre_mix_w, post_mix_w, pre_ffn_w, post_ffn_w, w_in, lb_logits, hg_norm_w,
           conv_w, a_log, dt_bias, gdn_norm_w, w_out, w_ff1, w_ff2):
    batch, seq, d = x.shape
    hg_heads = lb_logits.shape[1]
    gdn_heads = a_log.shape[0]
    hg_cols = 4 * hg_heads * HEAD_DIM
    n_main = hg_cols + 4 * gdn_heads * HEAD_DIM

    x2 = x.reshape(batch * seq, d)
    sh_m, sc_m, gt_m, sh_f, sc_f, gt_f = [mod[:batch, None, i * d:(i + 1) * d] for i in range(6)]
    row = lambda v: v.reshape(1, -1)
    pad_lanes = lambda v: jnp.pad(v.reshape(1, -1), ((0, 0), (0, LANES - v.shape[-1])))

    w_in_t = w_in.T
    w_ab_t = jnp.pad(w_in_t[n_main:], ((0, LANES - (w_in_t.shape[0] - n_main)), (0, 0))).astype(BF16)
    h, ab = _prep(x2, row(pre_mix_w), sc_m, sh_m, w_ab_t, seq)
    proj = _inproj(h, w_in_t, n_main)

    o_hg, w_ff2_b, w_out_b = _hgrn(proj, lb_logits.reshape(lb_logits.shape[0], -1), row(hg_norm_w),
                                   [w_ff2, w_out], batch, seq, hg_heads, 0)
    o_gdn, w_ff1_b = _gdn(proj, ab, conv_w, pad_lanes(a_log), pad_lanes(dt_bias), row(gdn_norm_w),
                          [w_ff1], batch, seq, gdn_heads, hg_cols // HEAD_DIM)

    kh = hg_heads * HEAD_DIM
    x1, h2 = _outproj(o_hg, o_gdn, w_out_b, x2, gt_m, row(post_mix_w), row(pre_ffn_w), sc_f, sh_f,
                      seq, kh)
    out = _ffn(h2, w_ff1_b, w_ff2_b, x1, gt_f, row(post_ffn_w), seq)
    return out.reshape(batch, seq, d)


def kernel(x, c, w_ada, b_ada, pre_mix_norm, post_mix_norm, pre_ffn_norm, post_ffn_norm, w_in,
           hg_lb_logits, hg_norm, gdn_conv_w, gdn_a_log, gdn_dt_bias, gdn_norm, w_out, w_ff1, w_ff2):
    depth = w_ada.shape[0]
    assert depth == 1 and hg_lb_logits.shape[0] == 2, "single-layer configuration only"
    batch = c.shape[0]
    c_pad = jnp.pad(c, ((0, 8 - batch), (0, 0)))
    for l in range(depth):
        mod = _ada(c_pad, w_ada[l], b_ada[l][None, :])
        x = _layer(x, mod, pre_mix_norm[l], post_mix_norm[l], pre_ffn_norm[l], post_ffn_norm[l],
                   w_in[l], hg_lb_logits, hg_norm[l], gdn_conv_w[l], gdn_a_log[l], gdn_dt_bias[l],
                   gdn_norm[l], w_out[l], w_ff1[l], w_ff2[l])
    return x
```

```python
import functools

import jax
import jax.numpy as jnp
from jax import lax
from jax.experimental import pallas as pl
from jax.experimental.pallas import tpu as pltpu

F32 = jnp.float32
BF16 = jnp.bfloat16
EPS = 1e-6

LANES = 128
HEAD_DIM = 128
CHUNK = 64
SUB = 16
CONV_K = 4
HALO = 8
VMEM_LIMIT = 60 * 1024 * 1024


def _cparams(sem):
    return pltpu.CompilerParams(dimension_semantics=sem, vmem_limit_bytes=VMEM_LIMIT)


def _dot(a, b):
    return jnp.dot(a.astype(BF16), b.astype(BF16), preferred_element_type=F32)


def _dot_nt(a, b):
    return lax.dot_general(a.astype(BF16), b.astype(BF16), (((1,), (1,)), ((), ())),
                           preferred_element_type=F32)


def _dot_tn(a, b):
    return lax.dot_general(a.astype(BF16), b.astype(BF16), (((0,), (0,)), ((), ())),
                           preferred_element_type=F32)


def _split3(x):
    hi = x.astype(BF16)
    r = x - hi.astype(F32)
    mid = r.astype(BF16)
    lo = (r - mid.astype(F32)).astype(BF16)
    return hi, mid, lo


def _dot_exact_lhs(m_bf16, x):
    hi, mid, lo = _split3(x)
    d = lambda p: jnp.dot(m_bf16, p, preferred_element_type=F32)
    return d(hi) + d(mid) + d(lo)


def _dot_exact_rhs(x, m_bf16):
    hi, mid, lo = _split3(x)
    d = lambda p: jnp.dot(p, m_bf16, preferred_element_type=F32)
    return d(hi) + d(mid) + d(lo)


def _sigmoid(x):
    return 1.0 / (1.0 + jnp.exp(-x))


def _silu(x):
    return x * _sigmoid(x)


def _aligned(x, m):
    return x if isinstance(x, int) else pl.multiple_of(x, m)


def _iota2(shape, dim):
    return lax.broadcasted_iota(jnp.int32, shape, dim)


def _cast_specs(arrays, n_steps, step_index):
    specs, shapes = [], []
    for a in arrays:
        rows, cols = a.shape
        specs.append(pl.BlockSpec((rows // n_steps, cols), lambda *g: (step_index(*g), 0)))
        shapes.append(jax.ShapeDtypeStruct((rows, cols), BF16))
    return specs, shapes


def _cast_slabs(src_refs, dst_refs):
    for src, dst in zip(src_refs, dst_refs):
        dst[...] = src[...].astype(BF16)


def _ada_kernel(c_ref, w_ref, b_ref, o_ref):
    c = c_ref[...]
    o_ref[...] = _dot(_silu(c), w_ref[...]) + b_ref[...]


def _ada(c_pad, w_ada, b_ada, bn=2048):
    rows, d = c_pad.shape
    n = w_ada.shape[1]
    return pl.pallas_call(
        _ada_kernel,
        grid=(n // bn,),
        in_specs=[pl.BlockSpec((rows, d), lambda j: (0, 0)),
                  pl.BlockSpec((d, bn), lambda j: (0, j)),
                  pl.BlockSpec((1, bn), lambda j: (0, j))],
        out_specs=pl.BlockSpec((rows, bn), lambda j: (0, j)),
        out_shape=jax.ShapeDtypeStruct((rows, n), F32),
        compiler_params=_cparams(("arbitrary",)),
        name="ada",
    )(c_pad, w_ada, b_ada)


def _prep_kernel(x_ref, nw_ref, sc_ref, sh_ref, wab_ref, h_ref, ab_ref):
    x = x_ref[...]
    y = x * lax.rsqrt(jnp.mean(x * x, axis=-1, keepdims=True) + EPS)
    hb = ((y * nw_ref[...]) * (1.0 + sc_ref[0]) + sh_ref[0]).astype(BF16)
    h_ref[...] = hb
    ab_ref[...] = _dot_nt(hb, wab_ref[...])


def _prep(x2, nw, sc, sh, w_ab_t, seq, bm=1024):
    m, d = x2.shape
    per_b = seq // bm
    mod = pl.BlockSpec((1, 1, d), lambda i: (i // per_b, 0, 0))
    return pl.pallas_call(
        _prep_kernel,
        grid=(m // bm,),
        in_specs=[pl.BlockSpec((bm, d), lambda i: (i, 0)),
                  pl.BlockSpec((1, d), lambda i: (0, 0)), mod, mod,
                  pl.BlockSpec((LANES, d), lambda i: (0, 0))],
        out_specs=[pl.BlockSpec((bm, d), lambda i: (i, 0)),
                   pl.BlockSpec((bm, LANES), lambda i: (i, 0))],
        out_shape=[jax.ShapeDtypeStruct((m, d), BF16),
                   jax.ShapeDtypeStruct((m, LANES), F32)],
        compiler_params=_cparams(("arbitrary",)),
        name="prep",
    )(x2, nw, sc, sh, w_ab_t)


def _inproj_kernel(h_ref, w_ref, o_ref):
    o_ref[...] = _dot_nt(h_ref[...], w_ref[...])


def _inproj(h, w_t, n, bm=2048, bn=1024):
    m, d = h.shape
    return pl.pallas_call(
        _inproj_kernel,
        grid=(m // bm, n // bn),
        in_specs=[pl.BlockSpec((bm, d), lambda i, j: (i, 0)),
                  pl.BlockSpec((bn, d), lambda i, j: (j, 0))],
        out_specs=pl.BlockSpec((bm, bn), lambda i, j: (i, j)),
        out_shape=jax.ShapeDtypeStruct((m, n), F32),
        compiler_params=_cparams(("arbitrary", "arbitrary")),
        name="inproj",
    )(h, w_t)


def _gated_norm(o, nw, g):
    y = o * lax.rsqrt(jnp.mean(o * o, axis=-1, keepdims=True) + EPS)
    return (y * nw) * _silu(g)


HGRN_ROWS = 8 * CHUNK


def _hgrn_kernel(lbl_ref, q_ref, f_ref, i_ref, g_ref, nw_ref, *rest, heads, tb, n_cast):
    cast_src, (o_ref, *cast_dst) = rest[:n_cast], rest[n_cast:2 * n_cast + 1]
    st_ref, oi_ref, qd_ref, kv_ref, dec_ref = rest[2 * n_cast + 1:]
    _cast_slabs(cast_src, cast_dst)
    n_chunks = tb // CHUNK
    cpi = HGRN_ROWS // CHUNK

    @pl.when(pl.program_id(1) == 0)
    def _():
        st_ref[...] = jnp.zeros_like(st_ref)

    l0 = lbl_ref[0:1, :]
    l1 = lbl_ref[1:2, :]
    mx = jnp.maximum(l0, l1)
    e0 = jnp.exp(l0 - mx)
    lb = e0 / (e0 + jnp.exp(l1 - mx))

    tri = (_iota2((CHUNK, CHUNK), 1) <= _iota2((CHUNK, CHUNK), 0)).astype(BF16)

    def phase1(i, carry):
        rows = pl.ds(_aligned(i * HGRN_ROWS, HGRN_ROWS), HGRN_ROWS)
        f = lb + (1.0 - lb) * _sigmoid(f_ref[rows, :])
        logf = jnp.log(f)
        bc = jnp.concatenate([_dot_exact_lhs(tri, logf[c * CHUNK:(c + 1) * CHUNK]) for c in range(cpi)],
                             axis=0)
        q3 = _problems(q_ref[rows, :], heads, cpi)
        k3 = _problems(1.0 - f, heads, cpi)
        b3 = _problems(bc, heads, cpi)
        v3 = _problems(i_ref[rows, :], heads, cpi).astype(BF16)

        parts = []
        for blk in range(CHUNK // SUB):
            r0 = blk * SUB
            n = r0 + SUB
            p = b3[:, r0 + SUB // 2:r0 + SUB // 2 + 1, :]
            qt = q3[:, r0:n] * jnp.exp(b3[:, r0:n] - p)
            kt = k3[:, 0:n] * jnp.exp(p - b3[:, 0:n])
            a = _bmm_nt(qt, kt)
            keep = _iota2((SUB, n), 1) <= _iota2((SUB, n), 0) + r0
            parts.append(_bmm(jnp.where(keep, a, 0.0), v3[:, 0:n]))
        b_last = b3[:, CHUNK - 1:CHUNK, :]
        ps = pl.ds(_aligned(i * (cpi * heads), cpi * heads), cpi * heads)
        oi_ref[ps] = jnp.concatenate(parts, axis=1)
        qd_ref[ps] = (q3 * jnp.exp(b3)).astype(BF16)
        kv_ref[ps] = _bmm_tn(v3, k3 * jnp.exp(b_last - b3))
        dec_ref[ps] = jnp.exp(b_last)
        return carry

    lax.fori_loop(0, n_chunks // cpi, phase1, 0)

    nw = nw_ref[...]

    def phase2(c, carry):
        ps = pl.ds(_aligned(c * heads, heads), heads)
        rows = pl.ds(_aligned(c * CHUNK, CHUNK), CHUNK)
        st = st_ref[...]
        o = oi_ref[ps] + _bmm_nt(qd_ref[ps], st)
        st_ref[...] = st * dec_ref[ps] + kv_ref[ps]
        for h in range(heads):
            cols = slice(h * HEAD_DIM, (h + 1) * HEAD_DIM)
            o_ref[rows, cols] = _gated_norm(o[h], nw, g_ref[rows, cols]).astype(o_ref.dtype)
        return carry

    lax.fori_loop(0, n_chunks, phase2, 0, unroll=2)


def _hgrn(proj, lb_logits2, norm_w, casts, batch, seq, heads, col0, tb=512):
    per_b = seq // tb
    cast_specs, cast_shapes = _cast_specs(casts, batch * per_b, lambda b, j: b * per_b + j)
    width = heads * HEAD_DIM
    cb0 = col0 // heads

    def col(group):
        return pl.BlockSpec((tb, width), lambda b, j, g=group: (b * per_b + j, cb0 + g))

    n_prob = (tb // CHUNK) * heads
    return pl.pallas_call(
        functools.partial(_hgrn_kernel, heads=heads, tb=tb, n_cast=len(casts)),
        grid=(batch, per_b),
        in_specs=[pl.BlockSpec(lb_logits2.shape, lambda b, j: (0, 0)),
                  col(0), col(1), col(2), col(3),
                  pl.BlockSpec((1, HEAD_DIM), lambda b, j: (0, 0))] + cast_specs,
        out_specs=[pl.BlockSpec((tb, width), lambda b, j: (b * per_b + j, 0))] + cast_specs,
        out_shape=[jax.ShapeDtypeStruct((batch * seq, width), BF16)] + cast_shapes,
        scratch_shapes=[pltpu.VMEM((heads, HEAD_DIM, HEAD_DIM), F32),
                        pltpu.VMEM((n_prob, CHUNK, HEAD_DIM), F32),
                        pltpu.VMEM((n_prob, CHUNK, HEAD_DIM), BF16),
                        pltpu.VMEM((n_prob, HEAD_DIM, HEAD_DIM), F32),
                        pltpu.VMEM((n_prob, 1, HEAD_DIM), F32)],
        compiler_params=_cparams(("arbitrary", "arbitrary")),
        name="hgrn",
    )(lb_logits2, proj, proj, proj, proj, norm_w, *casts)


def _softplus(x):
    return jnp.maximum(x, 0.0) + jnp.log(1.0 + jnp.exp(-jnp.abs(x)))


def _l2norm(x, scale=1.0):
    return x * (lax.rsqrt(jnp.sum(x * x, axis=-1, keepdims=True) + EPS) * scale)


def _bmm(a, b):
    return lax.dot_general(a.astype(BF16), b.astype(BF16), (((2,), (1,)), ((0,), (0,))),
                           preferred_element_type=F32)


def _bmm_nt(a, b):
    return lax.dot_general(a.astype(BF16), b.astype(BF16), (((2,), (2,)), ((0,), (0,))),
                           preferred_element_type=F32)


def _bmm_tn(a, b):
    return lax.dot_general(a.astype(BF16), b.astype(BF16), (((1,), (1,)), ((0,), (0,))),
                           preferred_element_type=F32)


def _conv_silu_tile(x_ref, halo_ref, w, r0, rb):
    cur = x_ref[pl.ds(r0, rb), :]
    prev = x_ref[pl.ds(_aligned(jnp.maximum(r0 - HALO, 0), HALO), HALO), :]
    prev = jnp.where(r0 == 0, halo_ref[...], prev)
    ext = jnp.concatenate([prev, cur], axis=0)
    acc = w[CONV_K - 1:CONV_K, :] * cur
    for j in range(CONV_K - 1):
        s = HALO - (CONV_K - 1) + j
        acc = acc + w[j:j + 1, :] * ext[s:s + rb, :]
    return _silu(acc)


def _problems(x, heads, chunks):
    return jnp.stack([x[c * CHUNK:(c + 1) * CHUNK, h * HEAD_DIM:(h + 1) * HEAD_DIM]
                      for c in range(chunks) for h in range(heads)], axis=0)


GDN_ROWS = 4 * CHUNK


def _gdn_kernel(q_ref, k_ref, v_ref, g_ref, ab_ref, wq_ref, wk_ref, wv_ref, alog_ref, dtb_ref,
                nw_ref, *rest, heads, tb, n_cast):
    cast_src, (o_ref, *cast_dst) = rest[:n_cast], rest[n_cast:2 * n_cast + 1]
    s_ref, hq_ref, hk_ref, hv_ref, u_ref, w_ref, qd_ref, kt_ref, qk_ref, tail_ref = rest[2 * n_cast + 1:]
    _cast_slabs(cast_src, cast_dst)
    n_chunks = tb // CHUNK
    cpi = GDN_ROWS // CHUNK

    @pl.when(pl.program_id(1) == 0)
    def _():
        s_ref[...] = jnp.zeros_like(s_ref)
        hq_ref[...] = jnp.zeros_like(hq_ref)
        hk_ref[...] = jnp.zeros_like(hk_ref)
        hv_ref[...] = jnp.zeros_like(hv_ref)

    width = heads * HEAD_DIM
    row = _iota2((CHUNK, CHUNK), 0)
    colm = _iota2((CHUNK, CHUNK), 1)
    tri = (colm <= row).astype(BF16)
    incl = colm <= row
    strict = colm < row
    sub_shift = SUB.bit_length() - 1
    same_blk = (row >> sub_shift) == (colm >> sub_shift)
    eye = (row == colm).astype(F32)
    neg_diag_blk = jnp.where(same_blk & strict, -1.0, 0.0)
    off_blk = jnp.where(strict & ~same_blk, 1.0, 0.0)
    src = _iota2((LANES, width), 0)
    dst_head = _iota2((LANES, width), 1) >> (HEAD_DIM.bit_length() - 1)
    sel_a = (src == dst_head).astype(BF16)
    sel_b = (src == dst_head + heads).astype(BF16)

    def phase1(i, carry):
        r0 = _aligned(i * GDN_ROWS, GDN_ROWS)
        qa = _conv_silu_tile(q_ref, hq_ref, wq_ref[...], r0, GDN_ROWS)
        ka = _conv_silu_tile(k_ref, hk_ref, wk_ref[...], r0, GDN_ROWS)
        va = _conv_silu_tile(v_ref, hv_ref, wv_ref[...], r0, GDN_ROWS)
        qn = jnp.concatenate([_l2norm(qa[:, h * HEAD_DIM:(h + 1) * HEAD_DIM], HEAD_DIM ** -0.5)
                              for h in range(heads)], axis=1)
        kn = jnp.concatenate([_l2norm(ka[:, h * HEAD_DIM:(h + 1) * HEAD_DIM]) for h in range(heads)],
                             axis=1)

        ab = ab_ref[pl.ds(r0, GDN_ROWS), :]
        la = -jnp.exp(alog_ref[...]) * _softplus(ab + dtb_ref[...])
        be = _sigmoid(ab)
        gc = jnp.concatenate([_dot_exact_lhs(tri, la[c * CHUNK:(c + 1) * CHUNK]) for c in range(cpi)],
                             axis=0)
        g_rep = _dot_exact_rhs(gc, sel_a)
        be_rep = _dot_exact_rhs(be, sel_b)

        q3 = _problems(qn, heads, cpi)
        k3 = _problems(kn, heads, cpi)
        v3 = _problems(va, heads, cpi)
        g3 = _problems(g_rep, heads, cpi)
        b3 = _problems(be_rep, heads, cpi)

        g_sq = g3[:, :, :CHUNK]
        g_row = jnp.sum(g_sq * eye, axis=1, keepdims=True)
        gamma = jnp.exp(jnp.where(incl, g_sq - g_row, -jnp.inf))
        kb = k3.astype(BF16)
        mm = _bmm_nt(kb, kb) * (b3[:, :, :CHUNK] * gamma)
        nd = mm * neg_diag_blk
        off = mm * off_blk
        dinv = eye + nd
        pw = nd
        for _ in range(3):
            pw = _bmm(pw, pw)
            dinv = dinv + _bmm(dinv, pw)
        fm = -_bmm(dinv, off)
        ipf = eye + fm
        tinv = _bmm(ipf + _bmm(ipf, _bmm(fm, fm)), dinv)

        eg = jnp.exp(g3)
        uw = _bmm(tinv, jnp.concatenate([v3 * b3, k3 * (b3 * eg)], axis=2))
        g_last = g3[:, CHUNK - 1:CHUNK, :]
        ps = pl.ds(_aligned(i * (cpi * heads), cpi * heads), cpi * heads)
        u_ref[ps] = uw[:, :, :HEAD_DIM]
        w_ref[ps] = uw[:, :, HEAD_DIM:].astype(BF16)
        qd_ref[ps] = (q3 * eg).astype(BF16)
        kt_ref[ps] = (k3 * jnp.exp(g_last - g3)).astype(BF16)
        qk_ref[ps] = (_bmm_nt(q3, kb) * gamma).astype(BF16)
        tail_ref[ps] = jnp.exp(g_last)
        return carry

    lax.fori_loop(0, n_chunks // cpi, phase1, 0)
    hq_ref[...] = q_ref[tb - HALO:tb, :]
    hk_ref[...] = k_ref[tb - HALO:tb, :]
    hv_ref[...] = v_ref[tb - HALO:tb, :]

    nw = nw_ref[...]

    def phase2(c, carry):
        ps = pl.ds(_aligned(c * heads, heads), heads)
        rows = pl.ds(_aligned(c * CHUNK, CHUNK), CHUNK)
        s = s_ref[...]
        sb = s.astype(BF16)
        v_new = u_ref[ps] - _bmm(w_ref[ps], sb)
        vb = v_new.astype(BF16)
        o = _bmm(qd_ref[ps], sb) + _bmm(qk_ref[ps], vb)
        s_ref[...] = s * tail_ref[ps] + _bmm_tn(kt_ref[ps], vb)
        for h in range(heads):
            cols = slice(h * HEAD_DIM, (h + 1) * HEAD_DIM)
            o_ref[rows, cols] = _gated_norm(o[h], nw, g_ref[rows, cols]).astype(o_ref.dtype)
        return carry

    lax.fori_loop(0, n_chunks, phase2, 0, unroll=2)


def _gdn(proj, ab, conv_w, a_log_pad, dt_bias_pad, norm_w, casts, batch, seq, heads, col0, tb=512):
    per_b = seq // tb
    cast_specs, cast_shapes = _cast_specs(casts, batch * per_b, lambda b, j: b * per_b + j)
    width = heads * HEAD_DIM
    cb0 = col0 // heads

    def col(group):
        return pl.BlockSpec((tb, width), lambda b, j, g=group: (b * per_b + j, cb0 + g))

    def wcol(group):
        return pl.BlockSpec((CONV_K, width), lambda b, j, g=group: (0, g))

    small = pl.BlockSpec((1, LANES), lambda b, j: (0, 0))
    n_prob = (tb // CHUNK) * heads
    halo = pltpu.VMEM((HALO, width), F32)
    prob = lambda n, dt: pltpu.VMEM((n_prob, CHUNK, n), dt)
    return pl.pallas_call(
        functools.partial(_gdn_kernel, heads=heads, tb=tb, n_cast=len(casts)),
        grid=(batch, per_b),
        in_specs=[col(0), col(1), col(2), col(3),
                  pl.BlockSpec((tb, LANES), lambda b, j: (b * per_b + j, 0)),
                  wcol(0), wcol(1), wcol(2), small, small, small] + cast_specs,
        out_specs=[pl.BlockSpec((tb, width), lambda b, j: (b * per_b + j, 0))] + cast_specs,
        out_shape=[jax.ShapeDtypeStruct((batch * seq, width), BF16)] + cast_shapes,
        scratch_shapes=[pltpu.VMEM((heads, HEAD_DIM, HEAD_DIM), F32), halo, halo, halo,
                        prob(HEAD_DIM, F32), prob(HEAD_DIM, BF16), prob(HEAD_DIM, BF16),
                        prob(HEAD_DIM, BF16), prob(CHUNK, BF16),
                        pltpu.VMEM((n_prob, 1, HEAD_DIM), F32)],
        compiler_params=_cparams(("arbitrary", "arbitrary")),
        name="gdn",
    )(proj, proj, proj, proj, ab, conv_w, conv_w, conv_w, a_log_pad, dt_bias_pad, norm_w, *casts)


def _rms(y, w):
    return (y * lax.rsqrt(jnp.mean(y * y, axis=-1, keepdims=True) + EPS)) * w


OUT_ROWS = 256


def _outproj_kernel(oh_ref, og_ref, wh_ref, wg_ref, x_ref, gt_ref, pw_ref, fw_ref, sc_ref, sh_ref,
                    x1_ref, h2_ref):
    for r in range(oh_ref.shape[0] // OUT_ROWS):
        rows = pl.ds(r * OUT_ROWS, OUT_ROWS)
        y = (jnp.dot(oh_ref[rows, :], wh_ref[...], preferred_element_type=F32)
             + jnp.dot(og_ref[rows, :], wg_ref[...], preferred_element_type=F32))
        x1 = x_ref[rows, :] + gt_ref[0] * _rms(y, pw_ref[...])
        x1_ref[rows, :] = x1
        h2_ref[rows, :] = (_rms(x1, fw_ref[...]) * (1.0 + sc_ref[0]) + sh_ref[0]).astype(BF16)


def _outproj(o_hg, o_gdn, w_out, x2, gt, post_w, ffn_w, sc, sh, seq, kh, bm=512):
    m, d = x2.shape
    per_b = seq // bm
    vec = pl.BlockSpec((1, d), lambda i: (0, 0))
    mod = pl.BlockSpec((1, 1, d), lambda i: (i // per_b, 0, 0))
    return pl.pallas_call(
        _outproj_kernel,
        grid=(m // bm,),
        in_specs=[pl.BlockSpec((bm, kh), lambda i: (i, 0)),
                  pl.BlockSpec((bm, kh), lambda i: (i, 0)),
                  pl.BlockSpec((kh, d), lambda i: (0, 0)),
                  pl.BlockSpec((kh, d), lambda i: (1, 0)),
                  pl.BlockSpec((bm, d), lambda i: (i, 0)),
                  mod, vec, vec, mod, mod],
        out_specs=[pl.BlockSpec((bm, d), lambda i: (i, 0)),
                   pl.BlockSpec((bm, d), lambda i: (i, 0))],
        out_shape=[jax.ShapeDtypeStruct((m, d), F32),
                   jax.ShapeDtypeStruct((m, d), BF16)],
        compiler_params=_cparams(("arbitrary",)),
        name="outproj",
    )(o_hg, o_gdn, w_out, w_out, x2, gt, post_w, ffn_w, sc, sh)


NORM_ROWS = 128


def _ffn_kernel(h_ref, w1_ref, w2_ref, x1_ref, gt_ref, pw_ref, o_ref):
    f = pl.program_id(1)

    @pl.when(f == 0)
    def _():
        o_ref[...] = jnp.zeros_like(o_ref)

    a = jnp.maximum(jnp.dot(h_ref[...], w1_ref[...], preferred_element_type=F32), 0.0)
    o_ref[...] += jnp.dot((a * a).astype(BF16), w2_ref[...], preferred_element_type=F32)

    @pl.when(f == pl.num_programs(1) - 1)
    def _():
        scale = gt_ref[0] * pw_ref[...]

        def rows_body(r, carry):
            rows = pl.ds(_aligned(r * NORM_ROWS, NORM_ROWS), NORM_ROWS)
            o_ref[rows, :] = x1_ref[rows, :] + _rms(o_ref[rows, :], scale)
            return carry

        lax.fori_loop(0, o_ref.shape[0] // NORM_ROWS, rows_body, 0)


def _ffn(h2, w1, w2, x1, gt, post_w, seq, bm=512, bf=2048):
    m, d = h2.shape
    dff = w1.shape[1]
    per_b = seq // bm
    return pl.pallas_call(
        _ffn_kernel,
        grid=(m // bm, dff // bf),
        in_specs=[pl.BlockSpec((bm, d), lambda i, f: (i, 0)),
                  pl.BlockSpec((d, bf), lambda i, f: (0, f)),
                  pl.BlockSpec((bf, d), lambda i, f: (f, 0)),
                  pl.BlockSpec((bm, d), lambda i, f: (i, 0)),
                  pl.BlockSpec((1, 1, d), lambda i, f: (i // per_b, 0, 0)),
                  pl.BlockSpec((1, d), lambda i, f: (0, 0))],
        out_specs=pl.BlockSpec((bm, d), lambda i, f: (i, 0)),
        out_shape=jax.ShapeDtypeStruct((m, d), F32),
        compiler_params=_cparams(("arbitrary", "arbitrary")),
        name="ffn",
    )(h2, w1, w2, x1, gt, post_w)


def _layer(x, mod, pre_mix_w, post_mix_w, pre_ffn_w, post_ffn_w, w_in, lb_logits, hg_norm_w,
           conv_w, a_log, dt_bias, gdn_norm_w, w_out, w_ff1, w_ff2):
    batch, seq, d = x.shape
    hg_heads = lb_logits.shape[1]
    gdn_heads = a_log.shape[0]
    hg_cols = 4 * hg_heads * HEAD_DIM
    n_main = hg_cols + 4 * gdn_heads * HEAD_DIM

    x2 = x.reshape(batch * seq, d)
    sh_m, sc_m, gt_m, sh_f, sc_f, gt_f = [mod[:batch, None, i * d:(i + 1) * d] for i in range(6)]
    row = lambda v: v.reshape(1, -1)
    pad_lanes = lambda v: jnp.pad(v.reshape(1, -1), ((0, 0), (0, LANES - v.shape[-1])))

    w_in_t = w_in.T
    w_ab_t = jnp.pad(w_in_t[n_main:], ((0, LANES - (w_in_t.shape[0] - n_main)), (0, 0))).astype(BF16)
    h, ab = _prep(x2, row(pre_mix_w), sc_m, sh_m, w_ab_t, seq)
    proj = _inproj(h, w_in_t, n_main)

    o_hg, w_ff2_b, w_out_b = _hgrn(proj, lb_logits.reshape(lb_logits.shape[0], -1), row(hg_norm_w),
                                   [w_ff2, w_out], batch, seq, hg_heads, 0)
    o_gdn, w_ff1_b = _gdn(proj, ab, conv_w, pad_lanes(a_log), pad_lanes(dt_bias), row(gdn_norm_w),
                          [w_ff1], batch, seq, gdn_heads, hg_cols // HEAD_DIM)

    kh = hg_heads * HEAD_DIM
    x1, h2 = _outproj(o_hg, o_gdn, w_out_b, x2, gt_m, row(post_mix_w), row(pre_ffn_w), sc_f, sh_f,
                      seq, kh)
    out = _ffn(h2, w_ff1_b, w_ff2_b, x1, gt_f, row(post_ffn_w), seq)
    return out.reshape(batch, seq, d)


def kernel(x, c, w_ada, b_ada, pre_mix_norm, post_mix_norm, pre_ffn_norm, post_ffn_norm, w_in,
           hg_lb_logits, hg_norm, gdn_conv_w, gdn_a_log, gdn_dt_bias, gdn_norm, w_out, w_ff1, w_ff2):
    depth = w_ada.shape[0]
    assert depth == 1 and hg_lb_logits.shape[0] == 2, "single-layer configuration only"
    batch = c.shape[0]
    c_pad = jnp.pad(c, ((0, 8 - batch), (0, 0)))
    for l in range(depth):
        mod = _ada(c_pad, w_ada[l], b_ada[l][None, :])
        x = _layer(x, mod, pre_mix_norm[l], post_mix_norm[l], pre_ffn_norm[l], post_ffn_norm[l],
                   w_in[l], hg_lb_logits, hg_norm[l], gdn_conv_w[l], gdn_a_log[l], gdn_dt_bias[l],
                   gdn_norm[l], w_out[l], w_ff1[l], w_ff2[l])
    return x
```

```python
import functools

import jax
import jax.numpy as jnp
from jax import lax
from jax.experimental import pallas as pl
from jax.experimental.pallas import tpu as pltpu

F32 = jnp.float32
BF16 = jnp.bfloat16
EPS = 1e-6

LANES = 128
HEAD_DIM = 128
CHUNK = 64
SUB = 16
CONV_K = 4
HALO = 8
VMEM_LIMIT = 60 * 1024 * 1024


def _cparams(sem):
    return pltpu.CompilerParams(dimension_semantics=sem, vmem_limit_bytes=VMEM_LIMIT)


def _dot(a, b):
    return jnp.dot(a.astype(BF16), b.astype(BF16), preferred_element_type=F32)


def _dot_nt(a, b):
    return lax.dot_general(a.astype(BF16), b.astype(BF16), (((1,), (1,)), ((), ())),
                           preferred_element_type=F32)


def _dot_tn(a, b):
    return lax.dot_general(a.astype(BF16), b.astype(BF16), (((0,), (0,)), ((), ())),
                           preferred_element_type=F32)


def _split3(x):
    hi = x.astype(BF16)
    r = x - hi.astype(F32)
    mid = r.astype(BF16)
    lo = (r - mid.astype(F32)).astype(BF16)
    return hi, mid, lo


def _dot_exact_lhs(m_bf16, x):
    hi, mid, lo = _split3(x)
    d = lambda p: jnp.dot(m_bf16, p, preferred_element_type=F32)
    return d(hi) + d(mid) + d(lo)


def _dot_exact_rhs(x, m_bf16):
    hi, mid, lo = _split3(x)
    d = lambda p: jnp.dot(p, m_bf16, preferred_element_type=F32)
    return d(hi) + d(mid) + d(lo)


def _sigmoid(x):
    return jax.nn.sigmoid(x)


def _silu(x):
    return x * _sigmoid(x)


def _aligned(x, m):
    return x if isinstance(x, int) else pl.multiple_of(x, m)


def _iota2(shape, dim):
    return lax.broadcasted_iota(jnp.int32, shape, dim)


def _cast_specs(arrays, n_steps, step_index):
    specs, shapes = [], []
    for a in arrays:
        rows, cols = a.shape
        specs.append(pl.BlockSpec((rows // n_steps, cols), lambda *g: (step_index(*g), 0)))
        shapes.append(jax.ShapeDtypeStruct((rows, cols), BF16))
    return specs, shapes


def _cast_slabs(src_refs, dst_refs):
    for src, dst in zip(src_refs, dst_refs):
        dst[...] = src[...].astype(BF16)


def _ada_kernel(c_ref, w_ref, b_ref, o_ref):
    c = c_ref[...]
    o_ref[...] = _dot(_silu(c), w_ref[...]) + b_ref[...]


def _ada(c_pad, w_ada, b_ada, bn=1024):
    rows, d = c_pad.shape
    n = w_ada.shape[1]
    return pl.pallas_call(
        _ada_kernel,
        grid=(n // bn,),
        in_specs=[pl.BlockSpec((rows, d), lambda j: (0, 0)),
                  pl.BlockSpec((d, bn), lambda j: (0, j)),
                  pl.BlockSpec((1, bn), lambda j: (0, j))],
        out_specs=pl.BlockSpec((rows, bn), lambda j: (0, j)),
        out_shape=jax.ShapeDtypeStruct((rows, n), F32),
        compiler_params=_cparams(("arbitrary",)),
        name="ada",
    )(c_pad, w_ada, b_ada)


def _prep_kernel(x_ref, nw_ref, sc_ref, sh_ref, wab_ref, h_ref, ab_ref):
    x = x_ref[...]
    y = x * lax.rsqrt(jnp.mean(x * x, axis=-1, keepdims=True) + EPS)
    hb = ((y * nw_ref[...]) * (1.0 + sc_ref[0]) + sh_ref[0]).astype(BF16)
    h_ref[...] = hb
    ab_ref[...] = _dot_nt(hb, wab_ref[...])


def _prep(x2, nw, sc, sh, w_ab_t, seq, bm=1024):
    m, d = x2.shape
    per_b = seq // bm
    mod = pl.BlockSpec((1, 1, d), lambda i: (i // per_b, 0, 0))
    return pl.pallas_call(
        _prep_kernel,
        grid=(m // bm,),
        in_specs=[pl.BlockSpec((bm, d), lambda i: (i, 0)),
                  pl.BlockSpec((1, d), lambda i: (0, 0)), mod, mod,
                  pl.BlockSpec((LANES, d), lambda i: (0, 0))],
        out_specs=[pl.BlockSpec((bm, d), lambda i: (i, 0)),
                   pl.BlockSpec((bm, LANES), lambda i: (i, 0))],
        out_shape=[jax.ShapeDtypeStruct((m, d), BF16),
                   jax.ShapeDtypeStruct((m, LANES), F32)],
        compiler_params=_cparams(("arbitrary",)),
        name="prep",
    )(x2, nw, sc, sh, w_ab_t)


def _inproj_kernel(h_ref, w_ref, o_ref):
    o_ref[...] = _dot_nt(h_ref[...], w_ref[...])


def _inproj(h, w_t, n, bm=2048, bn=1024):
    m, d = h.shape
    return pl.pallas_call(
        _inproj_kernel,
        grid=(m // bm, n // bn),
        in_specs=[pl.BlockSpec((bm, d), lambda i, j: (i, 0)),
                  pl.BlockSpec((bn, d), lambda i, j: (j, 0))],
        out_specs=pl.BlockSpec((bm, bn), lambda i, j: (i, j)),
        out_shape=jax.ShapeDtypeStruct((m, n), F32),
        compiler_params=_cparams(("arbitrary", "arbitrary")),
        name="inproj",
    )(h, w_t)


def _gated_norm(o, nw, g):
    y = o * lax.rsqrt(jnp.mean(o * o, axis=-1, keepdims=True) + EPS)
    return (y * nw) * _silu(g)


HGRN_ROWS = 8 * CHUNK


def _hgrn_kernel(lbl_ref, q_ref, f_ref, i_ref, g_ref, nw_ref, *rest, heads, tb, n_cast):
    cast_src, (o_ref, *cast_dst) = rest[:n_cast], rest[n_cast:2 * n_cast + 1]
    st_ref, oi_ref, qd_ref, kv_ref, dec_ref = rest[2 * n_cast + 1:]
    _cast_slabs(cast_src, cast_dst)
    n_chunks = tb // CHUNK
    cpi = HGRN_ROWS // CHUNK

    @pl.when(pl.program_id(1) == 0)
    def _():
        st_ref[...] = jnp.zeros_like(st_ref)

    l0 = lbl_ref[0:1, :]
    l1 = lbl_ref[1:2, :]
    mx = jnp.maximum(l0, l1)
    e0 = jnp.exp(l0 - mx)
    lb = e0 / (e0 + jnp.exp(l1 - mx))

    tri = (_iota2((CHUNK, CHUNK), 1) <= _iota2((CHUNK, CHUNK), 0)).astype(BF16)

    def phase1(i, carry):
        rows = pl.ds(_aligned(i * HGRN_ROWS, HGRN_ROWS), HGRN_ROWS)
        f = lb + (1.0 - lb) * _sigmoid(f_ref[rows, :])
        logf = jnp.log(f)
        bc = jnp.concatenate([_dot_exact_lhs(tri, logf[c * CHUNK:(c + 1) * CHUNK]) for c in range(cpi)],
                             axis=0)
        q3 = _problems(q_ref[rows, :], heads, cpi)
        k3 = _problems(1.0 - f, heads, cpi)
        b3 = _problems(bc, heads, cpi)
        v3 = _problems(i_ref[rows, :], heads, cpi).astype(BF16)

        parts = []
        for blk in range(CHUNK // SUB):
            r0 = blk * SUB
            n = r0 + SUB
            p = b3[:, r0 + SUB // 2:r0 + SUB // 2 + 1, :]
            qt = q3[:, r0:n] * jnp.exp(b3[:, r0:n] - p)
            kt = k3[:, 0:n] * jnp.exp(p - b3[:, 0:n])
            a = _bmm_nt(qt, kt)
            keep = _iota2((SUB, n), 1) <= _iota2((SUB, n), 0) + r0
            parts.append(_bmm(jnp.where(keep, a, 0.0), v3[:, 0:n]))
        b_last = b3[:, CHUNK - 1:CHUNK, :]
        ps = pl.ds(_aligned(i * (cpi * heads), cpi * heads), cpi * heads)
        oi_ref[ps] = jnp.concatenate(parts, axis=1)
        qd_ref[ps] = (q3 * jnp.exp(b3)).astype(BF16)
        kv_ref[ps] = _bmm_tn(v3, k3 * jnp.exp(b_last - b3))
        dec_ref[ps] = jnp.exp(b_last)
        return carry

    lax.fori_loop(0, n_chunks // cpi, phase1, 0)

    nw = nw_ref[...]

    def phase2(c, carry):
        ps = pl.ds(_aligned(c * heads, heads), heads)
        rows = pl.ds(_aligned(c * CHUNK, CHUNK), CHUNK)
        st = st_ref[...]
        o = oi_ref[ps] + _bmm_nt(qd_ref[ps], st)
        st_ref[...] = st * dec_ref[ps] + kv_ref[ps]
        for h in range(heads):
            cols = slice(h * HEAD_DIM, (h + 1) * HEAD_DIM)
            o_ref[rows, cols] = _gated_norm(o[h], nw, g_ref[rows, cols]).astype(o_ref.dtype)
        return carry

    lax.fori_loop(0, n_chunks, phase2, 0, unroll=2)


def _hgrn(proj, lb_logits2, norm_w, casts, batch, seq, heads, col0, tb=512):
    per_b = seq // tb
    cast_specs, cast_shapes = _cast_specs(casts, batch * per_b, lambda b, j: b * per_b + j)
    width = heads * HEAD_DIM
    cb0 = col0 // heads

    def col(group):
        return pl.BlockSpec((tb, width), lambda b, j, g=group: (b * per_b + j, cb0 + g))

    n_prob = (tb // CHUNK) * heads
    return pl.pallas_call(
        functools.partial(_hgrn_kernel, heads=heads, tb=tb, n_cast=len(casts)),
        grid=(batch, per_b),
        in_specs=[pl.BlockSpec(lb_logits2.shape, lambda b, j: (0, 0)),
                  col(0), col(1), col(2), col(3),
                  pl.BlockSpec((1, HEAD_DIM), lambda b, j: (0, 0))] + cast_specs,
        out_specs=[pl.BlockSpec((tb, width), lambda b, j: (b * per_b + j, 0))] + cast_specs,
        out_shape=[jax.ShapeDtypeStruct((batch * seq, width), BF16)] + cast_shapes,
        scratch_shapes=[pltpu.VMEM((heads, HEAD_DIM, HEAD_DIM), F32),
                        pltpu.VMEM((n_prob, CHUNK, HEAD_DIM), F32),
                        pltpu.VMEM((n_prob, CHUNK, HEAD_DIM), BF16),
                        pltpu.VMEM((n_prob, HEAD_DIM, HEAD_DIM), F32),
                        pltpu.VMEM((n_prob, 1, HEAD_DIM), F32)],
        compiler_params=_cparams(("arbitrary", "arbitrary")),
        name="hgrn",
    )(lb_logits2, proj, proj, proj, proj, norm_w, *casts)


def _softplus(x):
    return jnp.maximum(x, 0.0) + jnp.log(1.0 + jnp.exp(-jnp.abs(x)))


def _l2norm(x, scale=1.0):
    return x * (lax.rsqrt(jnp.sum(x * x, axis=-1, keepdims=True) + EPS) * scale)


def _bmm(a, b):
    return lax.dot_general(a.astype(BF16), b.astype(BF16), (((2,), (1,)), ((0,), (0,))),
                           preferred_element_type=F32)


def _bmm_nt(a, b):
    return lax.dot_general(a.astype(BF16), b.astype(BF16), (((2,), (2,)), ((0,), (0,))),
                           preferred_element_type=F32)


def _bmm_tn(a, b):
    return lax.dot_general(a.astype(BF16), b.astype(BF16), (((1,), (1,)), ((0,), (0,))),
                           preferred_element_type=F32)


def _conv_silu_tile(x_ref, halo_ref, w, r0, rb):
    cur = x_ref[pl.ds(r0, rb), :]
    prev = x_ref[pl.ds(_aligned(jnp.maximum(r0 - HALO, 0), HALO), HALO), :]
    prev = jnp.where(r0 == 0, halo_ref[...], prev)
    ext = jnp.concatenate([prev, cur], axis=0)
    acc = w[CONV_K - 1:CONV_K, :] * cur
    for j in range(CONV_K - 1):
        s = HALO - (CONV_K - 1) + j
        acc = acc + w[j:j + 1, :] * ext[s:s + rb, :]
    return _silu(acc)


def _problems(x, heads, chunks):
    return jnp.stack([x[c * CHUNK:(c + 1) * CHUNK, h * HEAD_DIM:(h + 1) * HEAD_DIM]
                      for c in range(chunks) for h in range(heads)], axis=0)


GDN_ROWS = 4 * CHUNK


def _gdn_kernel(q_ref, k_ref, v_ref, g_ref, ab_ref, wq_ref, wk_ref, wv_ref, alog_ref, dtb_ref,
                nw_ref, *rest, heads, tb, n_cast):
    cast_src, (o_ref, *cast_dst) = rest[:n_cast], rest[n_cast:2 * n_cast + 1]
    s_ref, hq_ref, hk_ref, hv_ref, u_ref, w_ref, qd_ref, kt_ref, qk_ref, tail_ref = rest[2 * n_cast + 1:]
    _cast_slabs(cast_src, cast_dst)
    n_chunks = tb // CHUNK
    cpi = GDN_ROWS // CHUNK

    @pl.when(pl.program_id(1) == 0)
    def _():
        s_ref[...] = jnp.zeros_like(s_ref)
        hq_ref[...] = jnp.zeros_like(hq_ref)
        hk_ref[...] = jnp.zeros_like(hk_ref)
        hv_ref[...] = jnp.zeros_like(hv_ref)

    width = heads * HEAD_DIM
    row = _iota2((CHUNK, CHUNK), 0)
    colm = _iota2((CHUNK, CHUNK), 1)
    tri = (colm <= row).astype(BF16)
    incl = colm <= row
    strict = colm < row
    sub_shift = SUB.bit_length() - 1
    same_blk = (row >> sub_shift) == (colm >> sub_shift)
    eye = (row == colm).astype(F32)
    neg_diag_blk = jnp.where(same_blk & strict, -1.0, 0.0)
    off_blk = jnp.where(strict & ~same_blk, 1.0, 0.0)
    src = _iota2((LANES, width), 0)
    dst_head = _iota2((LANES, width), 1) >> (HEAD_DIM.bit_length() - 1)
    sel_a = (src == dst_head).astype(BF16)
    sel_b = (src == dst_head + heads).astype(BF16)

    def phase1(i, carry):
        r0 = _aligned(i * GDN_ROWS, GDN_ROWS)
        qa = _conv_silu_tile(q_ref, hq_ref, wq_ref[...], r0, GDN_ROWS)
        ka = _conv_silu_tile(k_ref, hk_ref, wk_ref[...], r0, GDN_ROWS)
        va = _conv_silu_tile(v_ref, hv_ref, wv_ref[...], r0, GDN_ROWS)
        qn = jnp.concatenate([_l2norm(qa[:, h * HEAD_DIM:(h + 1) * HEAD_DIM], HEAD_DIM ** -0.5)
                              for h in range(heads)], axis=1)
        kn = jnp.concatenate([_l2norm(ka[:, h * HEAD_DIM:(h + 1) * HEAD_DIM]) for h in range(heads)],
                             axis=1)

        ab = ab_ref[pl.ds(r0, GDN_ROWS), :]
        la = -jnp.exp(alog_ref[...]) * _softplus(ab + dtb_ref[...])
        be = _sigmoid(ab)
        gc = jnp.concatenate([_dot_exact_lhs(tri, la[c * CHUNK:(c + 1) * CHUNK]) for c in range(cpi)],
                             axis=0)
        g_rep = _dot_exact_rhs(gc, sel_a)
        be_rep = _dot_exact_rhs(be, sel_b)

        q3 = _problems(qn, heads, cpi)
        k3 = _problems(kn, heads, cpi)
        v3 = _problems(va, heads, cpi)
        g3 = _problems(g_rep, heads, cpi)
        b3 = _problems(be_rep, heads, cpi)

        g_sq = g3[:, :, :CHUNK]
        g_row = jnp.sum(g_sq * eye, axis=1, keepdims=True)
        gamma = jnp.exp(jnp.where(incl, g_sq - g_row, -jnp.inf))
        kb = k3.astype(BF16)
        mm = _bmm_nt(kb, kb) * (b3[:, :, :CHUNK] * gamma)
        nd = mm * neg_diag_blk
        off = mm * off_blk
        dinv = eye + nd
        pw = nd
        for _ in range(3):
            pw = _bmm(pw, pw)
            dinv = dinv + _bmm(dinv, pw)
        fm = -_bmm(dinv, off)
        ipf = eye + fm
        tinv = _bmm(ipf + _bmm(ipf, _bmm(fm, fm)), dinv)

        eg = jnp.exp(g3)
        uw = _bmm(tinv, jnp.concatenate([v3 * b3, k3 * (b3 * eg)], axis=2))
        g_last = g3[:, CHUNK - 1:CHUNK, :]
        ps = pl.ds(_aligned(i * (cpi * heads), cpi * heads), cpi * heads)
        u_ref[ps] = uw[:, :, :HEAD_DIM]
        w_ref[ps] = uw[:, :, HEAD_DIM:].astype(BF16)
        qd_ref[ps] = (q3 * eg).astype(BF16)
        kt_ref[ps] = (k3 * jnp.exp(g_last - g3)).astype(BF16)
        qk_ref[ps] = (_bmm_nt(q3, kb) * gamma).astype(BF16)
        tail_ref[ps] = jnp.exp(g_last)
        return carry

    lax.fori_loop(0, n_chunks // cpi, phase1, 0)
    hq_ref[...] = q_ref[tb - HALO:tb, :]
    hk_ref[...] = k_ref[tb - HALO:tb, :]
    hv_ref[...] = v_ref[tb - HALO:tb, :]

    nw = nw_ref[...]

    def phase2(c, carry):
        ps = pl.ds(_aligned(c * heads, heads), heads)
        rows = pl.ds(_aligned(c * CHUNK, CHUNK), CHUNK)
        s = s_ref[...]
        sb = s.astype(BF16)
        v_new = u_ref[ps] - _bmm(w_ref[ps], sb)
        vb = v_new.astype(BF16)
        o = _bmm(qd_ref[ps], sb) + _bmm(qk_ref[ps], vb)
        s_ref[...] = s * tail_ref[ps] + _bmm_tn(kt_ref[ps], vb)
        for h in range(heads):
            cols = slice(h * HEAD_DIM, (h + 1) * HEAD_DIM)
            o_ref[rows, cols] = _gated_norm(o[h], nw, g_ref[rows, cols]).astype(o_ref.dtype)
        return carry

    lax.fori_loop(0, n_chunks, phase2, 0, unroll=2)


def _gdn(proj, ab, conv_w, a_log_pad, dt_bias_pad, norm_w, casts, batch, seq, heads, col0, tb=512):
    per_b = seq // tb
    cast_specs, cast_shapes = _cast_specs(casts, batch * per_b, lambda b, j: b * per_b + j)
    width = heads * HEAD_DIM
    cb0 = col0 // heads

    def col(group):
        return pl.BlockSpec((tb, width), lambda b, j, g=group: (b * per_b + j, cb0 + g))

    def wcol(group):
        return pl.BlockSpec((CONV_K, width), lambda b, j, g=group: (0, g))

    small = pl.BlockSpec((1, LANES), lambda b, j: (0, 0))
    n_prob = (tb // CHUNK) * heads
    halo = pltpu.VMEM((HALO, width), F32)
    prob = lambda n, dt: pltpu.VMEM((n_prob, CHUNK, n), dt)
    return pl.pallas_call(
        functools.partial(_gdn_kernel, heads=heads, tb=tb, n_cast=len(casts)),
        grid=(batch, per_b),
        in_specs=[col(0), col(1), col(2), col(3),
                  pl.BlockSpec((tb, LANES), lambda b, j: (b * per_b + j, 0)),
                  wcol(0), wcol(1), wcol(2), small, small, small] + cast_specs,
        out_specs=[pl.BlockSpec((tb, width), lambda b, j: (b * per_b + j, 0))] + cast_specs,
        out_shape=[jax.ShapeDtypeStruct((batch * seq, width), BF16)] + cast_shapes,
        scratch_shapes=[pltpu.VMEM((heads, HEAD_DIM, HEAD_DIM), F32), halo, halo, halo,
                        prob(HEAD_DIM, F32), prob(HEAD_DIM, BF16), prob(HEAD_DIM, BF16),
                        prob(HEAD_DIM, BF16), prob(CHUNK, BF16),
                        pltpu.VMEM((n_prob, 1, HEAD_DIM), F32)],
        compiler_params=_cparams(("arbitrary", "arbitrary")),
        name="gdn",
    )(proj, proj, proj, proj, ab, conv_w, conv_w, conv_w, a_log_pad, dt_bias_pad, norm_w, *casts)


def _rms(y, w):
    return (y * lax.rsqrt(jnp.mean(y * y, axis=-1, keepdims=True) + EPS)) * w


OUT_ROWS = 256


def _outproj_kernel(oh_ref, og_ref, w_ref, x_ref, gt_ref, pw_ref, fw_ref, sc_ref, sh_ref,
                    x1_ref, h2_ref):
    post_scale = gt_ref[0] * pw_ref[...]
    pre_scale = fw_ref[...] * (1.0 + sc_ref[0])
    for r in range(oh_ref.shape[0] // OUT_ROWS):
        rows = pl.ds(r * OUT_ROWS, OUT_ROWS)
        o = jnp.concatenate([oh_ref[rows, :], og_ref[rows, :]], axis=1)
        y = jnp.dot(o, w_ref[...], preferred_element_type=F32)
        x1 = x_ref[rows, :] + _rms(y, post_scale)
        x1_ref[rows, :] = x1
        h2_ref[rows, :] = (_rms(x1, pre_scale) + sh_ref[0]).astype(BF16)


def _outproj(o_hg, o_gdn, w_out, x2, gt, post_w, ffn_w, sc, sh, seq, bm=512):
    m, d = x2.shape
    kh = o_hg.shape[1]
    per_b = seq // bm
    vec = pl.BlockSpec((1, d), lambda i: (0, 0))
    mod = pl.BlockSpec((1, 1, d), lambda i: (i // per_b, 0, 0))
    return pl.pallas_call(
        _outproj_kernel,
        grid=(m // bm,),
        in_specs=[pl.BlockSpec((bm, kh), lambda i: (i, 0)),
                  pl.BlockSpec((bm, kh), lambda i: (i, 0)),
                  pl.BlockSpec((2 * kh, d), lambda i: (0, 0)),
                  pl.BlockSpec((bm, d), lambda i: (i, 0)),
                  mod, vec, vec, mod, mod],
        out_specs=[pl.BlockSpec((bm, d), lambda i: (i, 0)),
                   pl.BlockSpec((bm, d), lambda i: (i, 0))],
        out_shape=[jax.ShapeDtypeStruct((m, d), F32),
                   jax.ShapeDtypeStruct((m, d), BF16)],
        compiler_params=_cparams(("arbitrary",)),
        name="outproj",
    )(o_hg, o_gdn, w_out, x2, gt, post_w, ffn_w, sc, sh)


NORM_ROWS = 128


def _ffn_kernel(h_ref, w1_ref, w2_ref, x1_ref, gt_ref, pw_ref, o_ref):
    f = pl.program_id(1)

    @pl.when(f == 0)
    def _():
        o_ref[...] = jnp.zeros_like(o_ref)

    a = jnp.maximum(jnp.dot(h_ref[...], w1_ref[...], preferred_element_type=F32), 0.0)
    o_ref[...] += jnp.dot((a * a).astype(BF16), w2_ref[...], preferred_element_type=F32)

    @pl.when(f == pl.num_programs(1) - 1)
    def _():
        scale = gt_ref[0] * pw_ref[...]

        def rows_body(r, carry):
            rows = pl.ds(_aligned(r * NORM_ROWS, NORM_ROWS), NORM_ROWS)
            o_ref[rows, :] = x1_ref[rows, :] + _rms(o_ref[rows, :], scale)
            return carry

        lax.fori_loop(0, o_ref.shape[0] // NORM_ROWS, rows_body, 0)


def _ffn(h2, w1, w2, x1, gt, post_w, seq, bm=512, bf=2048):
    m, d = h2.shape
    dff = w1.shape[1]
    per_b = seq // bm
    return pl.pallas_call(
        _ffn_kernel,
        grid=(m // bm, dff // bf),
        in_specs=[pl.BlockSpec((bm, d), lambda i, f: (i, 0)),
                  pl.BlockSpec((d, bf), lambda i, f: (0, f)),
                  pl.BlockSpec((bf, d), lambda i, f: (f, 0)),
                  pl.BlockSpec((bm, d), lambda i, f: (i, 0)),
                  pl.BlockSpec((1, 1, d), lambda i, f: (i // per_b, 0, 0)),
                  pl.BlockSpec((1, d), lambda i, f: (0, 0))],
        out_specs=pl.BlockSpec((bm, d), lambda i, f: (i, 0)),
        out_shape=jax.ShapeDtypeStruct((m, d), F32),
        compiler_params=_cparams(("arbitrary", "arbitrary")),
        name="ffn",
    )(h2, w1, w2, x1, gt, post_w)


def _layer(x, mod, pre_mix_w, post_mix_w, pre_ffn_w, post_ffn_w, w_in, lb_logits, hg_norm_w,
           conv_w, a_log, dt_bias, gdn_norm_w, w_out, w_ff1, w_ff2):
    batch, seq, d = x.shape
    hg_heads = lb_logits.shape[1]
    gdn_heads = a_log.shape[0]
    hg_cols = 4 * hg_heads * HEAD_DIM
    n_main = hg_cols + 4 * gdn_heads * HEAD_DIM

    x2 = x.reshape(batch * seq, d)
    sh_m, sc_m, gt_m, sh_f, sc_f, gt_f = [mod[:batch, None, i * d:(i + 1) * d] for i in range(6)]
    row = lambda v: v.reshape(1, -1)
    pad_lanes = lambda v: jnp.pad(v.reshape(1, -1), ((0, 0), (0, LANES - v.shape[-1])))

    w_in_t = w_in.T
    w_ab_t = jnp.pad(w_in_t[n_main:], ((0, LANES - (w_in_t.shape[0] - n_main)), (0, 0))).astype(BF16)
    h, ab = _prep(x2, row(pre_mix_w), sc_m, sh_m, w_ab_t, seq)
    proj = _inproj(h, w_in_t, n_main)

    o_hg, w_out_b = _hgrn(proj, lb_logits.reshape(lb_logits.shape[0], -1), row(hg_norm_w),
                          [w_out], batch, seq, hg_heads, 0)
    o_gdn, w_ff1_b, w_ff2_b = _gdn(proj, ab, conv_w, pad_lanes(a_log), pad_lanes(dt_bias),
                                   row(gdn_norm_w), [w_ff1, w_ff2], batch, seq, gdn_heads,
                                   hg_cols // HEAD_DIM)

    x1, h2 = _outproj(o_hg, o_gdn, w_out_b, x2, gt_m, row(post_mix_w), row(pre_ffn_w), sc_f, sh_f,
                      seq)
    out = _ffn(h2, w_ff1_b, w_ff2_b, x1, gt_f, row(post_ffn_w), seq)
    return out.reshape(batch, seq, d)


def kernel(x, c, w_ada, b_ada, pre_mix_norm, post_mix_norm, pre_ffn_norm, post_ffn_norm, w_in,
           hg_lb_logits, hg_norm, gdn_conv_w, gdn_a_log, gdn_dt_bias, gdn_norm, w_out, w_ff1, w_ff2):
    depth = w_ada.shape[0]
    assert depth == 1 and hg_lb_logits.shape[0] == 2, "single-layer configuration only"
    batch = c.shape[0]
    c_pad = jnp.pad(c, ((0, 8 - batch), (0, 0)))
    for l in range(depth):
        mod = _ada(c_pad, w_ada[l], b_ada[l][None, :])
        x = _layer(x, mod, pre_mix_norm[l], post_mix_norm[l], pre_ffn_norm[l], post_ffn_norm[l],
                   w_in[l], hg_lb_logits, hg_norm[l], gdn_conv_w[l], gdn_a_log[l], gdn_dt_bias[l],
                   gdn_norm[l], w_out[l], w_ff1[l], w_ff2[l])
    return x
```

```python
import functools

import jax
import jax.numpy as jnp
from jax import lax
from jax.experimental import pallas as pl
from jax.experimental.pallas import tpu as pltpu

F32 = jnp.float32
BF16 = jnp.bfloat16
EPS = 1e-6

LANES = 128
HEAD_DIM = 128
CHUNK = 64
SUB = 16
CONV_K = 4
HALO = 8
VMEM_LIMIT = 60 * 1024 * 1024


def _cparams(sem):
    return pltpu.CompilerParams(dimension_semantics=sem, vmem_limit_bytes=VMEM_LIMIT)


def _dot(a, b):
    return jnp.dot(a.astype(BF16), b.astype(BF16), preferred_element_type=F32)


def _dot_nt(a, b):
    return lax.dot_general(a.astype(BF16), b.astype(BF16), (((1,), (1,)), ((), ())),
                           preferred_element_type=F32)


def _dot_tn(a, b):
    return lax.dot_general(a.astype(BF16), b.astype(BF16), (((0,), (0,)), ((), ())),
                           preferred_element_type=F32)


def _split3(x):
    hi = x.astype(BF16)
    r = x - hi.astype(F32)
    mid = r.astype(BF16)
    lo = (r - mid.astype(F32)).astype(BF16)
    return hi, mid, lo


def _dot_exact_lhs(m_bf16, x):
    hi, mid, lo = _split3(x)
    d = lambda p: jnp.dot(m_bf16, p, preferred_element_type=F32)
    return d(hi) + d(mid) + d(lo)


def _dot_exact_rhs(x, m_bf16):
    hi, mid, lo = _split3(x)
    d = lambda p: jnp.dot(p, m_bf16, preferred_element_type=F32)
    return d(hi) + d(mid) + d(lo)


def _sigmoid(x):
    return jax.nn.sigmoid(x)


def _silu(x):
    return x * _sigmoid(x)


def _aligned(x, m):
    return x if isinstance(x, int) else pl.multiple_of(x, m)


def _iota2(shape, dim):
    return lax.broadcasted_iota(jnp.int32, shape, dim)


def _cast_specs(arrays, n_steps, step_index):
    specs, shapes = [], []
    for a in arrays:
        rows, cols = a.shape
        specs.append(pl.BlockSpec((rows // n_steps, cols), lambda *g: (step_index(*g), 0)))
        shapes.append(jax.ShapeDtypeStruct((rows, cols), BF16))
    return specs, shapes


def _cast_slabs(src_refs, dst_refs):
    for src, dst in zip(src_refs, dst_refs):
        dst[...] = src[...].astype(BF16)


def _ada_kernel(c_ref, w_ref, b_ref, o_ref):
    c = c_ref[...]
    o_ref[...] = _dot(_silu(c), w_ref[...]) + b_ref[...]


def _ada(c_pad, w_ada, b_ada, bn=1024):
    rows, d = c_pad.shape
    n = w_ada.shape[1]
    return pl.pallas_call(
        _ada_kernel,
        grid=(n // bn,),
        in_specs=[pl.BlockSpec((rows, d), lambda j: (0, 0)),
                  pl.BlockSpec((d, bn), lambda j: (0, j)),
                  pl.BlockSpec((1, bn), lambda j: (0, j))],
        out_specs=pl.BlockSpec((rows, bn), lambda j: (0, j)),
        out_shape=jax.ShapeDtypeStruct((rows, n), F32),
        compiler_params=_cparams(("arbitrary",)),
        name="ada",
    )(c_pad, w_ada, b_ada)


def _prep_kernel(x_ref, nw_ref, sc_ref, sh_ref, wab_ref, h_ref, ab_ref):
    x = x_ref[...]
    y = x * lax.rsqrt(jnp.mean(x * x, axis=-1, keepdims=True) + EPS)
    hb = ((y * nw_ref[...]) * (1.0 + sc_ref[0]) + sh_ref[0]).astype(BF16)
    h_ref[...] = hb
    ab_ref[...] = _dot_nt(hb, wab_ref[...])


def _prep(x2, nw, sc, sh, w_ab_t, seq, bm=1024):
    m, d = x2.shape
    per_b = seq // bm
    mod = pl.BlockSpec((1, 1, d), lambda i: (i // per_b, 0, 0))
    return pl.pallas_call(
        _prep_kernel,
        grid=(m // bm,),
        in_specs=[pl.BlockSpec((bm, d), lambda i: (i, 0)),
                  pl.BlockSpec((1, d), lambda i: (0, 0)), mod, mod,
                  pl.BlockSpec((LANES, d), lambda i: (0, 0))],
        out_specs=[pl.BlockSpec((bm, d), lambda i: (i, 0)),
                   pl.BlockSpec((bm, LANES), lambda i: (i, 0))],
        out_shape=[jax.ShapeDtypeStruct((m, d), BF16),
                   jax.ShapeDtypeStruct((m, LANES), F32)],
        compiler_params=_cparams(("arbitrary",)),
        name="prep",
    )(x2, nw, sc, sh, w_ab_t)


def _inproj_kernel(h_ref, w_ref, o_ref):
    o_ref[...] = _dot_nt(h_ref[...], w_ref[...])


def _inproj(h, w_t, n, bm=2048, bn=1024):
    m, d = h.shape
    return pl.pallas_call(
        _inproj_kernel,
        grid=(m // bm, n // bn),
        in_specs=[pl.BlockSpec((bm, d), lambda i, j: (i, 0)),
                  pl.BlockSpec((bn, d), lambda i, j: (j, 0))],
        out_specs=pl.BlockSpec((bm, bn), lambda i, j: (i, j)),
        out_shape=jax.ShapeDtypeStruct((m, n), F32),
        compiler_params=_cparams(("arbitrary", "arbitrary")),
        name="inproj",
    )(h, w_t)


def _gated_norm(o, nw, g):
    y = o * lax.rsqrt(jnp.mean(o * o, axis=-1, keepdims=True) + EPS)
    return (y * nw) * _silu(g)


HGRN_ROWS = 8 * CHUNK


def _hgrn_kernel(lbl_ref, q_ref, f_ref, i_ref, g_ref, nw_ref, *rest, heads, tb, n_cast):
    cast_src, (o_ref, *cast_dst) = rest[:n_cast], rest[n_cast:2 * n_cast + 1]
    st_ref, oi_ref, qd_ref, kv_ref, dec_ref = rest[2 * n_cast + 1:]
    _cast_slabs(cast_src, cast_dst)
    n_chunks = tb // CHUNK
    cpi = HGRN_ROWS // CHUNK

    @pl.when(pl.program_id(1) == 0)
    def _():
        st_ref[...] = jnp.zeros_like(st_ref)

    l0 = lbl_ref[0:1, :]
    l1 = lbl_ref[1:2, :]
    mx = jnp.maximum(l0, l1)
    e0 = jnp.exp(l0 - mx)
    lb = e0 / (e0 + jnp.exp(l1 - mx))

    tri = (_iota2((CHUNK, CHUNK), 1) <= _iota2((CHUNK, CHUNK), 0)).astype(BF16)

    def phase1(i, carry):
        rows = pl.ds(_aligned(i * HGRN_ROWS, HGRN_ROWS), HGRN_ROWS)
        f = lb + (1.0 - lb) * _sigmoid(f_ref[rows, :])
        logf = jnp.log(f)
        bc = jnp.concatenate([_dot_exact_lhs(tri, logf[c * CHUNK:(c + 1) * CHUNK]) for c in range(cpi)],
                             axis=0)
        q3 = _problems(q_ref[rows, :], heads, cpi)
        k3 = _problems(1.0 - f, heads, cpi)
        b3 = _problems(bc, heads, cpi)
        v3 = _problems(i_ref[rows, :], heads, cpi).astype(BF16)

        parts = []
        for blk in range(CHUNK // SUB):
            r0 = blk * SUB
            n = r0 + SUB
            p = b3[:, r0 + SUB // 2:r0 + SUB // 2 + 1, :]
            qt = q3[:, r0:n] * jnp.exp(b3[:, r0:n] - p)
            kt = k3[:, 0:n] * jnp.exp(p - b3[:, 0:n])
            a = _bmm_nt(qt, kt)
            keep = _iota2((SUB, n), 1) <= _iota2((SUB, n), 0) + r0
            parts.append(_bmm(jnp.where(keep, a, 0.0), v3[:, 0:n]))
        b_last = b3[:, CHUNK - 1:CHUNK, :]
        ps = pl.ds(_aligned(i * (cpi * heads), cpi * heads), cpi * heads)
        oi_ref[ps] = jnp.concatenate(parts, axis=1)
        qd_ref[ps] = (q3 * jnp.exp(b3)).astype(BF16)
        kv_ref[ps] = _bmm_tn(v3, k3 * jnp.exp(b_last - b3))
        dec_ref[ps] = jnp.exp(b_last)
        return carry

    lax.fori_loop(0, n_chunks // cpi, phase1, 0)

    nw = nw_ref[...]

    def phase2(c, carry):
        ps = pl.ds(_aligned(c * heads, heads), heads)
        rows = pl.ds(_aligned(c * CHUNK, CHUNK), CHUNK)
        st = st_ref[...]
        o = oi_ref[ps] + _bmm_nt(qd_ref[ps], st)
        st_ref[...] = st * dec_ref[ps] + kv_ref[ps]
        for h in range(heads):
            cols = slice(h * HEAD_DIM, (h + 1) * HEAD_DIM)
            o_ref[rows, cols] = _gated_norm(o[h], nw, g_ref[rows, cols]).astype(o_ref.dtype)
        return carry

    lax.fori_loop(0, n_chunks, phase2, 0, unroll=2)


def _hgrn(proj, lb_logits2, norm_w, casts, batch, seq, heads, col0, tb=512):
    per_b = seq // tb
    cast_specs, cast_shapes = _cast_specs(casts, batch * per_b, lambda b, j: b * per_b + j)
    width = heads * HEAD_DIM
    cb0 = col0 // heads

    def col(group):
        return pl.BlockSpec((tb, width), lambda b, j, g=group: (b * per_b + j, cb0 + g))

    n_prob = (tb // CHUNK) * heads
    return pl.pallas_call(
        functools.partial(_hgrn_kernel, heads=heads, tb=tb, n_cast=len(casts)),
        grid=(batch, per_b),
        in_specs=[pl.BlockSpec(lb_logits2.shape, lambda b, j: (0, 0)),
                  col(0), col(1), col(2), col(3),
                  pl.BlockSpec((1, HEAD_DIM), lambda b, j: (0, 0))] + cast_specs,
        out_specs=[pl.BlockSpec((tb, width), lambda b, j: (b * per_b + j, 0))] + cast_specs,
        out_shape=[jax.ShapeDtypeStruct((batch * seq, width), BF16)] + cast_shapes,
        scratch_shapes=[pltpu.VMEM((heads, HEAD_DIM, HEAD_DIM), F32),
                        pltpu.VMEM((n_prob, CHUNK, HEAD_DIM), F32),
                        pltpu.VMEM((n_prob, CHUNK, HEAD_DIM), BF16),
                        pltpu.VMEM((n_prob, HEAD_DIM, HEAD_DIM), F32),
                        pltpu.VMEM((n_prob, 1, HEAD_DIM), F32)],
        compiler_params=_cparams(("arbitrary", "arbitrary")),
        name="hgrn",
    )(lb_logits2, proj, proj, proj, proj, norm_w, *casts)


def _softplus(x):
    return jnp.maximum(x, 0.0) + jnp.log(1.0 + jnp.exp(-jnp.abs(x)))


def _l2norm(x, scale=1.0):
    return x * (lax.rsqrt(jnp.sum(x * x, axis=-1, keepdims=True) + EPS) * scale)


def _bmm(a, b):
    return lax.dot_general(a.astype(BF16), b.astype(BF16), (((2,), (1,)), ((0,), (0,))),
                           preferred_element_type=F32)


def _bmm_nt(a, b):
    return lax.dot_general(a.astype(BF16), b.astype(BF16), (((2,), (2,)), ((0,), (0,))),
                           preferred_element_type=F32)


def _bmm_tn(a, b):
    return lax.dot_general(a.astype(BF16), b.astype(BF16), (((1,), (1,)), ((0,), (0,))),
                           preferred_element_type=F32)


def _conv_silu_tile(x_ref, halo_ref, w, r0, rb):
    cur = x_ref[pl.ds(r0, rb), :]
    prev = x_ref[pl.ds(_aligned(jnp.maximum(r0 - HALO, 0), HALO), HALO), :]
    prev = jnp.where(r0 == 0, halo_ref[...], prev)
    ext = jnp.concatenate([prev, cur], axis=0)
    acc = w[CONV_K - 1:CONV_K, :] * cur
    for j in range(CONV_K - 1):
        s = HALO - (CONV_K - 1) + j
        acc = acc + w[j:j + 1, :] * ext[s:s + rb, :]
    return _silu(acc)


def _problems(x, heads, chunks):
    return jnp.stack([x[c * CHUNK:(c + 1) * CHUNK, h * HEAD_DIM:(h + 1) * HEAD_DIM]
                      for c in range(chunks) for h in range(heads)], axis=0)


GDN_ROWS = 4 * CHUNK


def _gdn_kernel(q_ref, k_ref, v_ref, g_ref, ab_ref, wq_ref, wk_ref, wv_ref, alog_ref, dtb_ref,
                nw_ref, *rest, heads, tb, n_cast):
    cast_src, (o_ref, *cast_dst) = rest[:n_cast], rest[n_cast:2 * n_cast + 1]
    s_ref, hq_ref, hk_ref, hv_ref, u_ref, wqd_ref, qkt_ref, tail_ref = rest[2 * n_cast + 1:]
    _cast_slabs(cast_src, cast_dst)
    n_chunks = tb // CHUNK
    cpi = GDN_ROWS // CHUNK

    @pl.when(pl.program_id(1) == 0)
    def _():
        s_ref[...] = jnp.zeros_like(s_ref)
        hq_ref[...] = jnp.zeros_like(hq_ref)
        hk_ref[...] = jnp.zeros_like(hk_ref)
        hv_ref[...] = jnp.zeros_like(hv_ref)

    width = heads * HEAD_DIM
    row = _iota2((CHUNK, CHUNK), 0)
    colm = _iota2((CHUNK, CHUNK), 1)
    tri = (colm <= row).astype(BF16)
    incl = colm <= row
    strict = colm < row
    sub_shift = SUB.bit_length() - 1
    same_blk = (row >> sub_shift) == (colm >> sub_shift)
    eye = (row == colm).astype(F32)
    neg_diag_blk = jnp.where(same_blk & strict, -1.0, 0.0)
    off_blk = jnp.where(strict & ~same_blk, 1.0, 0.0)
    src = _iota2((LANES, width), 0)
    dst_head = _iota2((LANES, width), 1) >> (HEAD_DIM.bit_length() - 1)
    sel_a = (src == dst_head).astype(BF16)
    sel_b = (src == dst_head + heads).astype(BF16)

    def phase1(i, carry):
        r0 = _aligned(i * GDN_ROWS, GDN_ROWS)
        qa = _conv_silu_tile(q_ref, hq_ref, wq_ref[...], r0, GDN_ROWS)
        ka = _conv_silu_tile(k_ref, hk_ref, wk_ref[...], r0, GDN_ROWS)
        va = _conv_silu_tile(v_ref, hv_ref, wv_ref[...], r0, GDN_ROWS)
        qn = jnp.concatenate([_l2norm(qa[:, h * HEAD_DIM:(h + 1) * HEAD_DIM], HEAD_DIM ** -0.5)
                              for h in range(heads)], axis=1)
        kn = jnp.concatenate([_l2norm(ka[:, h * HEAD_DIM:(h + 1) * HEAD_DIM]) for h in range(heads)],
                             axis=1)

        ab = ab_ref[pl.ds(r0, GDN_ROWS), :]
        la = -jnp.exp(alog_ref[...]) * _softplus(ab + dtb_ref[...])
        be = _sigmoid(ab)
        gc = jnp.concatenate([_dot_exact_lhs(tri, la[c * CHUNK:(c + 1) * CHUNK]) for c in range(cpi)],
                             axis=0)
        g_rep = _dot_exact_rhs(gc, sel_a)
        be_rep = _dot_exact_rhs(be, sel_b)

        q3 = _problems(qn, heads, cpi)
        k3 = _problems(kn, heads, cpi)
        v3 = _problems(va, heads, cpi)
        g3 = _problems(g_rep, heads, cpi)
        b3 = _problems(be_rep, heads, cpi)

        g_sq = g3[:, :, :CHUNK]
        g_row = jnp.sum(g_sq * eye, axis=1, keepdims=True)
        gamma = jnp.exp(jnp.where(incl, g_sq - g_row, -jnp.inf))
        kb = k3.astype(BF16)
        mm = _bmm_nt(kb, kb) * (b3[:, :, :CHUNK] * gamma)
        nd = mm * neg_diag_blk
        off = mm * off_blk
        dinv = eye + nd
        pw = nd
        for _ in range(3):
            pw = _bmm(pw, pw)
            dinv = dinv + _bmm(dinv, pw)
        fm = -_bmm(dinv, off)
        ipf = eye + fm
        tinv = _bmm(ipf + _bmm(ipf, _bmm(fm, fm)), dinv)

        eg = jnp.exp(g3)
        uw = _bmm(tinv, jnp.concatenate([v3 * b3, k3 * (b3 * eg)], axis=2))
        g_last = g3[:, CHUNK - 1:CHUNK, :]
        ps = pl.ds(_aligned(i * (cpi * heads), cpi * heads), cpi * heads)
        u_ref[ps] = uw[:, :, :HEAD_DIM]
        wqd_ref[ps] = jnp.concatenate([uw[:, :, HEAD_DIM:], q3 * eg], axis=1).astype(BF16)
        kt_t = jnp.swapaxes(k3 * jnp.exp(g_last - g3), 1, 2)
        qkt_ref[ps] = jnp.concatenate([_bmm_nt(q3, kb) * gamma, kt_t], axis=1).astype(BF16)
        tail_ref[ps] = jnp.exp(g_last)
        return carry

    lax.fori_loop(0, n_chunks // cpi, phase1, 0)
    hq_ref[...] = q_ref[tb - HALO:tb, :]
    hk_ref[...] = k_ref[tb - HALO:tb, :]
    hv_ref[...] = v_ref[tb - HALO:tb, :]

    nw = nw_ref[...]

    def phase2(c, carry):
        ps = pl.ds(_aligned(c * heads, heads), heads)
        rows = pl.ds(_aligned(c * CHUNK, CHUNK), CHUNK)
        s = s_ref[...]
        sb = s.astype(BF16)
        ws_qs = _bmm(wqd_ref[ps], sb)
        v_new = u_ref[ps] - ws_qs[:, :CHUNK]
        vb = v_new.astype(BF16)
        ov_ds = _bmm(qkt_ref[ps], vb)
        o = ws_qs[:, CHUNK:] + ov_ds[:, :CHUNK]
        s_ref[...] = s * tail_ref[ps] + ov_ds[:, CHUNK:]
        for h in range(heads):
            cols = slice(h * HEAD_DIM, (h + 1) * HEAD_DIM)
            o_ref[rows, cols] = _gated_norm(o[h], nw, g_ref[rows, cols]).astype(o_ref.dtype)
        return carry

    lax.fori_loop(0, n_chunks, phase2, 0, unroll=2)


def _gdn(proj, ab, conv_w, a_log_pad, dt_bias_pad, norm_w, casts, batch, seq, heads, col0, tb=512):
    per_b = seq // tb
    cast_specs, cast_shapes = _cast_specs(casts, batch * per_b, lambda b, j: b * per_b + j)
    width = heads * HEAD_DIM
    cb0 = col0 // heads

    def col(group):
        return pl.BlockSpec((tb, width), lambda b, j, g=group: (b * per_b + j, cb0 + g))

    def wcol(group):
        return pl.BlockSpec((CONV_K, width), lambda b, j, g=group: (0, g))

    small = pl.BlockSpec((1, LANES), lambda b, j: (0, 0))
    n_prob = (tb // CHUNK) * heads
    halo = pltpu.VMEM((HALO, width), F32)
    prob = lambda n, dt: pltpu.VMEM((n_prob, CHUNK, n), dt)
    return pl.pallas_call(
        functools.partial(_gdn_kernel, heads=heads, tb=tb, n_cast=len(casts)),
        grid=(batch, per_b),
        in_specs=[col(0), col(1), col(2), col(3),
                  pl.BlockSpec((tb, LANES), lambda b, j: (b * per_b + j, 0)),
                  wcol(0), wcol(1), wcol(2), small, small, small] + cast_specs,
        out_specs=[pl.BlockSpec((tb, width), lambda b, j: (b * per_b + j, 0))] + cast_specs,
        out_shape=[jax.ShapeDtypeStruct((batch * seq, width), BF16)] + cast_shapes,
        scratch_shapes=[pltpu.VMEM((heads, HEAD_DIM, HEAD_DIM), F32), halo, halo, halo,
                        prob(HEAD_DIM, F32), pltpu.VMEM((n_prob, 2 * CHUNK, HEAD_DIM), BF16),
                        pltpu.VMEM((n_prob, CHUNK + HEAD_DIM, CHUNK), BF16),
                        pltpu.VMEM((n_prob, 1, HEAD_DIM), F32)],
        compiler_params=_cparams(("arbitrary", "arbitrary")),
        name="gdn",
    )(proj, proj, proj, proj, ab, conv_w, conv_w, conv_w, a_log_pad, dt_bias_pad, norm_w, *casts)


def _rms(y, w):
    return (y * lax.rsqrt(jnp.mean(y * y, axis=-1, keepdims=True) + EPS)) * w


OUT_ROWS = 256


def _outproj_kernel(oh_ref, og_ref, w_ref, x_ref, gt_ref, pw_ref, fw_ref, sc_ref, sh_ref,
                    x1_ref, h2_ref):
    post_scale = gt_ref[0] * pw_ref[...]
    pre_scale = fw_ref[...] * (1.0 + sc_ref[0])
    for r in range(oh_ref.shape[0] // OUT_ROWS):
        rows = pl.ds(r * OUT_ROWS, OUT_ROWS)
        o = jnp.concatenate([oh_ref[rows, :], og_ref[rows, :]], axis=1)
        y = jnp.dot(o, w_ref[...], preferred_element_type=F32)
        x1 = x_ref[rows, :] + _rms(y, post_scale)
        x1_ref[rows, :] = x1
        h2_ref[rows, :] = (_rms(x1, pre_scale) + sh_ref[0]).astype(BF16)


def _outproj(o_hg, o_gdn, w_out, x2, gt, post_w, ffn_w, sc, sh, seq, bm=512):
    m, d = x2.shape
    kh = o_hg.shape[1]
    per_b = seq // bm
    vec = pl.BlockSpec((1, d), lambda i: (0, 0))
    mod = pl.BlockSpec((1, 1, d), lambda i: (i // per_b, 0, 0))
    return pl.pallas_call(
        _outproj_kernel,
        grid=(m // bm,),
        in_specs=[pl.BlockSpec((bm, kh), lambda i: (i, 0)),
                  pl.BlockSpec((bm, kh), lambda i: (i, 0)),
                  pl.BlockSpec((2 * kh, d), lambda i: (0, 0)),
                  pl.BlockSpec((bm, d), lambda i: (i, 0)),
                  mod, vec, vec, mod, mod],
        out_specs=[pl.BlockSpec((bm, d), lambda i: (i, 0)),
                   pl.BlockSpec((bm, d), lambda i: (i, 0))],
        out_shape=[jax.ShapeDtypeStruct((m, d), F32),
                   jax.ShapeDtypeStruct((m, d), BF16)],
        compiler_params=_cparams(("arbitrary",)),
        name="outproj",
    )(o_hg, o_gdn, w_out, x2, gt, post_w, ffn_w, sc, sh)


NORM_ROWS = 128


def _ffn_kernel(h_ref, w1_ref, w2_ref, x1_ref, gt_ref, pw_ref, o_ref):
    f = pl.program_id(1)

    @pl.when(f == 0)
    def _():
        o_ref[...] = jnp.zeros_like(o_ref)

    a = jnp.maximum(jnp.dot(h_ref[...], w1_ref[...], preferred_element_type=F32), 0.0)
    o_ref[...] += jnp.dot((a * a).astype(BF16), w2_ref[...], preferred_element_type=F32)

    @pl.when(f == pl.num_programs(1) - 1)
    def _():
        scale = gt_ref[0] * pw_ref[...]

        def rows_body(r, carry):
            rows = pl.ds(_aligned(r * NORM_ROWS, NORM_ROWS), NORM_ROWS)
            o_ref[rows, :] = x1_ref[rows, :] + _rms(o_ref[rows, :], scale)
            return carry

        lax.fori_loop(0, o_ref.shape[0] // NORM_ROWS, rows_body, 0)


def _ffn(h2, w1, w2, x1, gt, post_w, seq, bm=512, bf=2048):
    m, d = h2.shape
    dff = w1.shape[1]
    per_b = seq // bm
    return pl.pallas_call(
        _ffn_kernel,
        grid=(m // bm, dff // bf),
        in_specs=[pl.BlockSpec((bm, d), lambda i, f: (i, 0)),
                  pl.BlockSpec((d, bf), lambda i, f: (0, f)),
                  pl.BlockSpec((bf, d), lambda i, f: (f, 0)),
                  pl.BlockSpec((bm, d), lambda i, f: (i, 0)),
                  pl.BlockSpec((1, 1, d), lambda i, f: (i // per_b, 0, 0)),
                  pl.BlockSpec((1, d), lambda i, f: (0, 0))],
        out_specs=pl.BlockSpec((bm, d), lambda i, f: (i, 0)),
        out_shape=jax.ShapeDtypeStruct((m, d), F32),
        compiler_params=_cparams(("arbitrary", "arbitrary")),
        name="ffn",
    )(h2, w1, w2, x1, gt, post_w)


def _layer(x, mod, pre_mix_w, post_mix_w, pre_ffn_w, post_ffn_w, w_in, lb_logits, hg_norm_w,
           conv_w, a_log, dt_bias, gdn_norm_w, w_out, w_ff1, w_ff2):
    batch, seq, d = x.shape
    hg_heads = lb_logits.shape[1]
    gdn_heads = a_log.shape[0]
    hg_cols = 4 * hg_heads * HEAD_DIM
    n_main = hg_cols + 4 * gdn_heads * HEAD_DIM

    x2 = x.reshape(batch * seq, d)
    sh_m, sc_m, gt_m, sh_f, sc_f, gt_f = [mod[:batch, None, i * d:(i + 1) * d] for i in range(6)]
    row = lambda v: v.reshape(1, -1)
    pad_lanes = lambda v: jnp.pad(v.reshape(1, -1), ((0, 0), (0, LANES - v.shape[-1])))

    w_in_t = w_in.T
    w_ab_t = jnp.pad(w_in_t[n_main:], ((0, LANES - (w_in_t.shape[0] - n_main)), (0, 0))).astype(BF16)
    h, ab = _prep(x2, row(pre_mix_w), sc_m, sh_m, w_ab_t, seq)
    proj = _inproj(h, w_in_t, n_main)

    o_hg, w_ff2_b, w_out_b = _hgrn(proj, lb_logits.reshape(lb_logits.shape[0], -1), row(hg_norm_w),
                                   [w_ff2, w_out], batch, seq, hg_heads, 0)
    o_gdn, w_ff1_b = _gdn(proj, ab, conv_w, pad_lanes(a_log), pad_lanes(dt_bias), row(gdn_norm_w),
                          [w_ff1], batch, seq, gdn_heads, hg_cols // HEAD_DIM)

    x1, h2 = _outproj(o_hg, o_gdn, w_out_b, x2, gt_m, row(post_mix_w), row(pre_ffn_w), sc_f, sh_f,
                      seq)
    out = _ffn(h2, w_ff1_b, w_ff2_b, x1, gt_f, row(post_ffn_w), seq)
    return out.reshape(batch, seq, d)


def kernel(x, c, w_ada, b_ada, pre_mix_norm, post_mix_norm, pre_ffn_norm, post_ffn_norm, w_in,
           hg_lb_logits, hg_norm, gdn_conv_w, gdn_a_log, gdn_dt_bias, gdn_norm, w_out, w_ff1, w_ff2):
    depth = w_ada.shape[0]
    assert depth == 1 and hg_lb_logits.shape[0] == 2, "single-layer configuration only"
    batch = c.shape[0]
    c_pad = jnp.pad(c, ((0, 8 - batch), (0, 0)))
    for l in range(depth):
        mod = _ada(c_pad, w_ada[l], b_ada[l][None, :])
        x = _layer(x, mod, pre_mix_norm[l], post_mix_norm[l], pre_ffn_norm[l], post_ffn_norm[l],
                   w_in[l], hg_lb_logits, hg_norm[l], gdn_conv_w[l], gdn_a_log[l], gdn_dt_bias[l],
                   gdn_norm[l], w_out[l], w_ff1[l], w_ff2[l])
    return x
```

```python
import functools

import jax
import jax.numpy as jnp
from jax import lax
from jax.experimental import pallas as pl
from jax.experimental.pallas import tpu as pltpu

F32 = jnp.float32
BF16 = jnp.bfloat16
EPS = 1e-6

LANES = 128
HEAD_DIM = 128
CHUNK = 64
SUB = 16
CONV_K = 4
HALO = 8
VMEM_LIMIT = 60 * 1024 * 1024


def _cparams(sem):
    return pltpu.CompilerParams(dimension_semantics=sem, vmem_limit_bytes=VMEM_LIMIT)


def _dot(a, b):
    return jnp.dot(a.astype(BF16), b.astype(BF16), preferred_element_type=F32)


def _dot_nt(a, b):
    return lax.dot_general(a.astype(BF16), b.astype(BF16), (((1,), (1,)), ((), ())),
                           preferred_element_type=F32)


def _dot_tn(a, b):
    return lax.dot_general(a.astype(BF16), b.astype(BF16), (((0,), (0,)), ((), ())),
                           preferred_element_type=F32)


def _split3(x):
    hi = x.astype(BF16)
    r = x - hi.astype(F32)
    mid = r.astype(BF16)
    lo = (r - mid.astype(F32)).astype(BF16)
    return hi, mid, lo


def _dot_exact_lhs(m_bf16, x):
    hi, mid, lo = _split3(x)
    d = lambda p: jnp.dot(m_bf16, p, preferred_element_type=F32)
    return d(hi) + d(mid) + d(lo)


def _dot_exact_rhs(x, m_bf16):
    hi, mid, lo = _split3(x)
    d = lambda p: jnp.dot(p, m_bf16, preferred_element_type=F32)
    return d(hi) + d(mid) + d(lo)


def _sigmoid(x):
    return jax.nn.sigmoid(x)


def _silu(x):
    return x * _sigmoid(x)


def _aligned(x, m):
    return x if isinstance(x, int) else pl.multiple_of(x, m)


def _iota2(shape, dim):
    return lax.broadcasted_iota(jnp.int32, shape, dim)


def _cast_specs(arrays, n_steps, step_index):
    specs, shapes = [], []
    for a in arrays:
        rows, cols = a.shape
        specs.append(pl.BlockSpec((rows // n_steps, cols), lambda *g: (step_index(*g), 0)))
        shapes.append(jax.ShapeDtypeStruct((rows, cols), BF16))
    return specs, shapes


def _cast_slabs(src_refs, dst_refs):
    for src, dst in zip(src_refs, dst_refs):
        dst[...] = src[...].astype(BF16)


def _ada_kernel(c_ref, w_ref, b_ref, o_ref):
    c = c_ref[...]
    o_ref[...] = _dot(_silu(c), w_ref[...]) + b_ref[...]


def _ada(c_pad, w_ada, b_ada, bn=1024):
    rows, d = c_pad.shape
    n = w_ada.shape[1]
    return pl.pallas_call(
        _ada_kernel,
        grid=(n // bn,),
        in_specs=[pl.BlockSpec((rows, d), lambda j: (0, 0)),
                  pl.BlockSpec((d, bn), lambda j: (0, j)),
                  pl.BlockSpec((1, bn), lambda j: (0, j))],
        out_specs=pl.BlockSpec((rows, bn), lambda j: (0, j)),
        out_shape=jax.ShapeDtypeStruct((rows, n), F32),
        compiler_params=_cparams(("arbitrary",)),
        name="ada",
    )(c_pad, w_ada, b_ada)


def _prep_kernel(x_ref, nw_ref, sc_ref, sh_ref, wab_ref, h_ref, ab_ref):
    x = x_ref[...]
    y = x * lax.rsqrt(jnp.mean(x * x, axis=-1, keepdims=True) + EPS)
    hb = ((y * nw_ref[...]) * (1.0 + sc_ref[0]) + sh_ref[0]).astype(BF16)
    h_ref[...] = hb
    ab_ref[...] = _dot_nt(hb, wab_ref[...])


def _prep(x2, nw, sc, sh, w_ab_t, seq, bm=1024):
    m, d = x2.shape
    per_b = seq // bm
    mod = pl.BlockSpec((1, 1, d), lambda i: (i // per_b, 0, 0))
    return pl.pallas_call(
        _prep_kernel,
        grid=(m // bm,),
        in_specs=[pl.BlockSpec((bm, d), lambda i: (i, 0)),
                  pl.BlockSpec((1, d), lambda i: (0, 0)), mod, mod,
                  pl.BlockSpec((LANES, d), lambda i: (0, 0))],
        out_specs=[pl.BlockSpec((bm, d), lambda i: (i, 0)),
                   pl.BlockSpec((bm, LANES), lambda i: (i, 0))],
        out_shape=[jax.ShapeDtypeStruct((m, d), BF16),
                   jax.ShapeDtypeStruct((m, LANES), F32)],
        compiler_params=_cparams(("arbitrary",)),
        name="prep",
    )(x2, nw, sc, sh, w_ab_t)


def _inproj_kernel(h_ref, w_ref, o_ref):
    o_ref[...] = _dot_nt(h_ref[...], w_ref[...])


def _inproj(h, w_t, n, bm=2048, bn=1024):
    m, d = h.shape
    return pl.pallas_call(
        _inproj_kernel,
        grid=(m // bm, n // bn),
        in_specs=[pl.BlockSpec((bm, d), lambda i, j: (i, 0)),
                  pl.BlockSpec((bn, d), lambda i, j: (j, 0))],
        out_specs=pl.BlockSpec((bm, bn), lambda i, j: (i, j)),
        out_shape=jax.ShapeDtypeStruct((m, n), F32),
        compiler_params=_cparams(("arbitrary", "arbitrary")),
        name="inproj",
    )(h, w_t)


def _gated_norm(o, nw, g):
    y = o * lax.rsqrt(jnp.mean(o * o, axis=-1, keepdims=True) + EPS)
    return (y * nw) * _silu(g)


HGRN_ROWS = 8 * CHUNK


def _hgrn_kernel(lbl_ref, q_ref, f_ref, i_ref, g_ref, nw_ref, *rest, heads, tb, n_cast):
    cast_src, (o_ref, *cast_dst) = rest[:n_cast], rest[n_cast:2 * n_cast + 1]
    st_ref, oi_ref, qd_ref, kv_ref, dec_ref = rest[2 * n_cast + 1:]
    _cast_slabs(cast_src, cast_dst)
    n_chunks = tb // CHUNK
    cpi = HGRN_ROWS // CHUNK

    @pl.when(pl.program_id(1) == 0)
    def _():
        st_ref[...] = jnp.zeros_like(st_ref)

    l0 = lbl_ref[0:1, :]
    l1 = lbl_ref[1:2, :]
    mx = jnp.maximum(l0, l1)
    e0 = jnp.exp(l0 - mx)
    lb = e0 / (e0 + jnp.exp(l1 - mx))

    tri = (_iota2((CHUNK, CHUNK), 1) <= _iota2((CHUNK, CHUNK), 0)).astype(BF16)

    def phase1(i, carry):
        rows = pl.ds(_aligned(i * HGRN_ROWS, HGRN_ROWS), HGRN_ROWS)
        f = lb + (1.0 - lb) * _sigmoid(f_ref[rows, :])
        logf = jnp.log(f)
        bc = jnp.concatenate([_dot_exact_lhs(tri, logf[c * CHUNK:(c + 1) * CHUNK]) for c in range(cpi)],
                             axis=0)
        q3 = _problems(q_ref[rows, :], heads, cpi)
        k3 = _problems(1.0 - f, heads, cpi)
        b3 = _problems(bc, heads, cpi)
        v3 = _problems(i_ref[rows, :], heads, cpi).astype(BF16)

        parts = []
        for blk in range(CHUNK // SUB):
            r0 = blk * SUB
            n = r0 + SUB
            p = b3[:, r0 + SUB // 2:r0 + SUB // 2 + 1, :]
            qt = q3[:, r0:n] * jnp.exp(b3[:, r0:n] - p)
            kt = k3[:, 0:n] * jnp.exp(p - b3[:, 0:n])
            a = _bmm_nt(qt, kt)
            keep = _iota2((SUB, n), 1) <= _iota2((SUB, n), 0) + r0
            a = jnp.where(keep, a, 0.0)
            if n < CHUNK:
                a = jnp.concatenate([a, jnp.zeros(a.shape[:2] + (CHUNK - n,), F32)], axis=2)
            parts.append(a)
        b_last = b3[:, CHUNK - 1:CHUNK, :]
        ps = pl.ds(_aligned(i * (cpi * heads), cpi * heads), cpi * heads)
        oi_ref[ps] = _bmm(jnp.concatenate(parts, axis=1), v3)
        qd_ref[ps] = (q3 * jnp.exp(b3)).astype(BF16)
        kv_ref[ps] = _bmm_tn(v3, k3 * jnp.exp(b_last - b3))
        dec_ref[ps] = jnp.exp(b_last)
        return carry

    lax.fori_loop(0, n_chunks // cpi, phase1, 0)

    nw = nw_ref[...]

    def phase2(c, carry):
        ps = pl.ds(_aligned(c * heads, heads), heads)
        rows = pl.ds(_aligned(c * CHUNK, CHUNK), CHUNK)
        st = st_ref[...]
        o = oi_ref[ps] + _bmm_nt(qd_ref[ps], st)
        st_ref[...] = st * dec_ref[ps] + kv_ref[ps]
        for h in range(heads):
            cols = slice(h * HEAD_DIM, (h + 1) * HEAD_DIM)
            o_ref[rows, cols] = _gated_norm(o[h], nw, g_ref[rows, cols]).astype(o_ref.dtype)
        return carry

    lax.fori_loop(0, n_chunks, phase2, 0, unroll=2)


def _hgrn(proj, lb_logits2, norm_w, casts, batch, seq, heads, col0, tb=512):
    per_b = seq // tb
    cast_specs, cast_shapes = _cast_specs(casts, batch * per_b, lambda b, j: b * per_b + j)
    width = heads * HEAD_DIM
    cb0 = col0 // heads

    def col(group):
        return pl.BlockSpec((tb, width), lambda b, j, g=group: (b * per_b + j, cb0 + g))

    n_prob = (tb // CHUNK) * heads
    return pl.pallas_call(
        functools.partial(_hgrn_kernel, heads=heads, tb=tb, n_cast=len(casts)),
        grid=(batch, per_b),
        in_specs=[pl.BlockSpec(lb_logits2.shape, lambda b, j: (0, 0)),
                  col(0), col(1), col(2), col(3),
                  pl.BlockSpec((1, HEAD_DIM), lambda b, j: (0, 0))] + cast_specs,
        out_specs=[pl.BlockSpec((tb, width), lambda b, j: (b * per_b + j, 0))] + cast_specs,
        out_shape=[jax.ShapeDtypeStruct((batch * seq, width), BF16)] + cast_shapes,
        scratch_shapes=[pltpu.VMEM((heads, HEAD_DIM, HEAD_DIM), F32),
                        pltpu.VMEM((n_prob, CHUNK, HEAD_DIM), F32),
                        pltpu.VMEM((n_prob, CHUNK, HEAD_DIM), BF16),
                        pltpu.VMEM((n_prob, HEAD_DIM, HEAD_DIM), F32),
                        pltpu.VMEM((n_prob, 1, HEAD_DIM), F32)],
        compiler_params=_cparams(("arbitrary", "arbitrary")),
        name="hgrn",
    )(lb_logits2, proj, proj, proj, proj, norm_w, *casts)


def _softplus(x):
    return jnp.maximum(x, 0.0) + jnp.log(1.0 + jnp.exp(-jnp.abs(x)))


def _l2norm(x, scale=1.0):
    return x * (lax.rsqrt(jnp.sum(x * x, axis=-1, keepdims=True) + EPS) * scale)


def _bmm(a, b):
    return lax.dot_general(a.astype(BF16), b.astype(BF16), (((2,), (1,)), ((0,), (0,))),
                           preferred_element_type=F32)


def _bmm_nt(a, b):
    return lax.dot_general(a.astype(BF16), b.astype(BF16), (((2,), (2,)), ((0,), (0,))),
                           preferred_element_type=F32)


def _bmm_tn(a, b):
    return lax.dot_general(a.astype(BF16), b.astype(BF16), (((1,), (1,)), ((0,), (0,))),
                           preferred_element_type=F32)


def _conv_silu_tile(x_ref, halo_ref, w, r0, rb):
    cur = x_ref[pl.ds(r0, rb), :]
    prev = x_ref[pl.ds(_aligned(jnp.maximum(r0 - HALO, 0), HALO), HALO), :]
    prev = jnp.where(r0 == 0, halo_ref[...], prev)
    ext = jnp.concatenate([prev, cur], axis=0)
    acc = w[CONV_K - 1:CONV_K, :] * cur
    for j in range(CONV_K - 1):
        s = HALO - (CONV_K - 1) + j
        acc = acc + w[j:j + 1, :] * ext[s:s + rb, :]
    return _silu(acc)


def _problems(x, heads, chunks):
    return jnp.stack([x[c * CHUNK:(c + 1) * CHUNK, h * HEAD_DIM:(h + 1) * HEAD_DIM]
                      for c in range(chunks) for h in range(heads)], axis=0)


GDN_ROWS = 4 * CHUNK


def _gdn_kernel(q_ref, k_ref, v_ref, g_ref, ab_ref, wq_ref, wk_ref, wv_ref, alog_ref, dtb_ref,
                nw_ref, *rest, heads, tb, n_cast):
    cast_src, (o_ref, *cast_dst) = rest[:n_cast], rest[n_cast:2 * n_cast + 1]
    s_ref, hq_ref, hk_ref, hv_ref, u_ref, wqd_ref, qkt_ref, tail_ref = rest[2 * n_cast + 1:]
    _cast_slabs(cast_src, cast_dst)
    n_chunks = tb // CHUNK
    cpi = GDN_ROWS // CHUNK

    @pl.when(pl.program_id(1) == 0)
    def _():
        s_ref[...] = jnp.zeros_like(s_ref)
        hq_ref[...] = jnp.zeros_like(hq_ref)
        hk_ref[...] = jnp.zeros_like(hk_ref)
        hv_ref[...] = jnp.zeros_like(hv_ref)

    width = heads * HEAD_DIM
    row = _iota2((CHUNK, CHUNK), 0)
    colm = _iota2((CHUNK, CHUNK), 1)
    tri = (colm <= row).astype(BF16)
    incl = colm <= row
    strict = colm < row
    sub_shift = SUB.bit_length() - 1
    same_blk = (row >> sub_shift) == (colm >> sub_shift)
    eye = (row == colm).astype(F32)
    neg_diag_blk = jnp.where(same_blk & strict, -1.0, 0.0)
    off_blk = jnp.where(strict & ~same_blk, 1.0, 0.0)
    src = _iota2((LANES, width), 0)
    dst_head = _iota2((LANES, width), 1) >> (HEAD_DIM.bit_length() - 1)
    sel_a = (src == dst_head).astype(BF16)
    sel_b = (src == dst_head + heads).astype(BF16)

    def phase1(i, carry):
        r0 = _aligned(i * GDN_ROWS, GDN_ROWS)
        qa = _conv_silu_tile(q_ref, hq_ref, wq_ref[...], r0, GDN_ROWS)
        ka = _conv_silu_tile(k_ref, hk_ref, wk_ref[...], r0, GDN_ROWS)
        va = _conv_silu_tile(v_ref, hv_ref, wv_ref[...], r0, GDN_ROWS)
        qn = jnp.concatenate([_l2norm(qa[:, h * HEAD_DIM:(h + 1) * HEAD_DIM], HEAD_DIM ** -0.5)
                              for h in range(heads)], axis=1)
        kn = jnp.concatenate([_l2norm(ka[:, h * HEAD_DIM:(h + 1) * HEAD_DIM]) for h in range(heads)],
                             axis=1)

        ab = ab_ref[pl.ds(r0, GDN_ROWS), :]
        la = -jnp.exp(alog_ref[...]) * _softplus(ab + dtb_ref[...])
        be = _sigmoid(ab)
        gc = jnp.concatenate([_dot_exact_lhs(tri, la[c * CHUNK:(c + 1) * CHUNK]) for c in range(cpi)],
                             axis=0)
        g_rep = _dot_exact_rhs(gc, sel_a)
        be_rep = _dot_exact_rhs(be, sel_b)

        q3 = _problems(qn, heads, cpi)
        k3 = _problems(kn, heads, cpi)
        v3 = _problems(va, heads, cpi)
        g3 = _problems(g_rep, heads, cpi)
        b3 = _problems(be_rep, heads, cpi)

        g_sq = g3[:, :, :CHUNK]
        g_row = jnp.sum(g_sq * eye, axis=1, keepdims=True)
        gamma = jnp.exp(jnp.where(incl, g_sq - g_row, -jnp.inf))
        kb = k3.astype(BF16)
        kq = _bmm_nt(jnp.concatenate([kb, q3.astype(BF16)], axis=1), kb)
        mm = kq[:, :CHUNK] * (b3[:, :, :CHUNK] * gamma)
        nd = mm * neg_diag_blk
        off = mm * off_blk
        dinv = eye + nd
        pw = nd
        for _ in range(3):
            pw = _bmm(pw, pw)
            dinv = dinv + _bmm(dinv, pw)
        fm = -_bmm(dinv, off)
        ipf = eye + fm
        tinv = _bmm(ipf + _bmm(ipf, _bmm(fm, fm)), dinv)

        eg = jnp.exp(g3)
        uw = _bmm(tinv, jnp.concatenate([v3 * b3, k3 * (b3 * eg)], axis=2))
        g_last = g3[:, CHUNK - 1:CHUNK, :]
        ps = pl.ds(_aligned(i * (cpi * heads), cpi * heads), cpi * heads)
        u_ref[ps] = uw[:, :, :HEAD_DIM]
        wqd_ref[ps] = jnp.concatenate([uw[:, :, HEAD_DIM:], q3 * eg], axis=1).astype(BF16)
        kt_t = jnp.swapaxes(k3 * jnp.exp(g_last - g3), 1, 2)
        qkt_ref[ps] = jnp.concatenate([kq[:, CHUNK:] * gamma, kt_t], axis=1).astype(BF16)
        tail_ref[ps] = jnp.exp(g_last)
        return carry

    lax.fori_loop(0, n_chunks // cpi, phase1, 0)
    hq_ref[...] = q_ref[tb - HALO:tb, :]
    hk_ref[...] = k_ref[tb - HALO:tb, :]
    hv_ref[...] = v_ref[tb - HALO:tb, :]

    nw = nw_ref[...]

    def phase2(c, carry):
        ps = pl.ds(_aligned(c * heads, heads), heads)
        rows = pl.ds(_aligned(c * CHUNK, CHUNK), CHUNK)
        s = s_ref[...]
        sb = s.astype(BF16)
        ws_qs = _bmm(wqd_ref[ps], sb)
        v_new = u_ref[ps] - ws_qs[:, :CHUNK]
        vb = v_new.astype(BF16)
        ov_ds = _bmm(qkt_ref[ps], vb)
        o = ws_qs[:, CHUNK:] + ov_ds[:, :CHUNK]
        s_ref[...] = s * tail_ref[ps] + ov_ds[:, CHUNK:]
        for h in range(heads):
            cols = slice(h * HEAD_DIM, (h + 1) * HEAD_DIM)
            o_ref[rows, cols] = _gated_norm(o[h], nw, g_ref[rows, cols]).astype(o_ref.dtype)
        return carry

    lax.fori_loop(0, n_chunks, phase2, 0, unroll=2)


def _gdn(proj, ab, conv_w, a_log_pad, dt_bias_pad, norm_w, casts, batch, seq, heads, col0, tb=512):
    per_b = seq // tb
    cast_specs, cast_shapes = _cast_specs(casts, batch * per_b, lambda b, j: b * per_b + j)
    width = heads * HEAD_DIM
    cb0 = col0 // heads

    def col(group):
        return pl.BlockSpec((tb, width), lambda b, j, g=group: (b * per_b + j, cb0 + g))

    def wcol(group):
        return pl.BlockSpec((CONV_K, width), lambda b, j, g=group: (0, g))

    small = pl.BlockSpec((1, LANES), lambda b, j: (0, 0))
    n_prob = (tb // CHUNK) * heads
    halo = pltpu.VMEM((HALO, width), F32)
    prob = lambda n, dt: pltpu.VMEM((n_prob, CHUNK, n), dt)
    return pl.pallas_call(
        functools.partial(_gdn_kernel, heads=heads, tb=tb, n_cast=len(casts)),
        grid=(batch, per_b),
        in_specs=[col(0), col(1), col(2), col(3),
                  pl.BlockSpec((tb, LANES), lambda b, j: (b * per_b + j, 0)),
                  wcol(0), wcol(1), wcol(2), small, small, small] + cast_specs,
        out_specs=[pl.BlockSpec((tb, width), lambda b, j: (b * per_b + j, 0))] + cast_specs,
        out_shape=[jax.ShapeDtypeStruct((batch * seq, width), BF16)] + cast_shapes,
        scratch_shapes=[pltpu.VMEM((heads, HEAD_DIM, HEAD_DIM), F32), halo, halo, halo,
                        prob(HEAD_DIM, F32), pltpu.VMEM((n_prob, 2 * CHUNK, HEAD_DIM), BF16),
                        pltpu.VMEM((n_prob, CHUNK + HEAD_DIM, CHUNK), BF16),
                        pltpu.VMEM((n_prob, 1, HEAD_DIM), F32)],
        compiler_params=_cparams(("arbitrary", "arbitrary")),
        name="gdn",
    )(proj, proj, proj, proj, ab, conv_w, conv_w, conv_w, a_log_pad, dt_bias_pad, norm_w, *casts)


def _rms(y, w):
    return (y * lax.rsqrt(jnp.mean(y * y, axis=-1, keepdims=True) + EPS)) * w


OUT_ROWS = 256


def _outproj_kernel(oh_ref, og_ref, w_ref, x_ref, gt_ref, pw_ref, fw_ref, sc_ref, sh_ref,
                    x1_ref, h2_ref):
    post_scale = gt_ref[0] * pw_ref[...]
    pre_scale = fw_ref[...] * (1.0 + sc_ref[0])
    for r in range(oh_ref.shape[0] // OUT_ROWS):
        rows = pl.ds(r * OUT_ROWS, OUT_ROWS)
        o = jnp.concatenate([oh_ref[rows, :], og_ref[rows, :]], axis=1)
        y = jnp.dot(o, w_ref[...], preferred_element_type=F32)
        x1 = x_ref[rows, :] + _rms(y, post_scale)
        x1_ref[rows, :] = x1
        h2_ref[rows, :] = (_rms(x1, pre_scale) + sh_ref[0]).astype(BF16)


def _outproj(o_hg, o_gdn, w_out, x2, gt, post_w, ffn_w, sc, sh, seq, bm=512):
    m, d = x2.shape
    kh = o_hg.shape[1]
    per_b = seq // bm
    vec = pl.BlockSpec((1, d), lambda i: (0, 0))
    mod = pl.BlockSpec((1, 1, d), lambda i: (i // per_b, 0, 0))
    return pl.pallas_call(
        _outproj_kernel,
        grid=(m // bm,),
        in_specs=[pl.BlockSpec((bm, kh), lambda i: (i, 0)),
                  pl.BlockSpec((bm, kh), lambda i: (i, 0)),
                  pl.BlockSpec((2 * kh, d), lambda i: (0, 0)),
                  pl.BlockSpec((bm, d), lambda i: (i, 0)),
                  mod, vec, vec, mod, mod],
        out_specs=[pl.BlockSpec((bm, d), lambda i: (i, 0)),
                   pl.BlockSpec((bm, d), lambda i: (i, 0))],
        out_shape=[jax.ShapeDtypeStruct((m, d), F32),
                   jax.ShapeDtypeStruct((m, d), BF16)],
        compiler_params=_cparams(("arbitrary",)),
        name="outproj",
    )(o_hg, o_gdn, w_out, x2, gt, post_w, ffn_w, sc, sh)


NORM_ROWS = 128


def _ffn_kernel(h_ref, w1_ref, w2_ref, x1_ref, gt_ref, pw_ref, o_ref):
    f = pl.program_id(1)

    @pl.when(f == 0)
    def _():
        o_ref[...] = jnp.zeros_like(o_ref)

    a = jnp.maximum(jnp.dot(h_ref[...], w1_ref[...], preferred_element_type=F32), 0.0)
    o_ref[...] += jnp.dot((a * a).astype(BF16), w2_ref[...], preferred_element_type=F32)

    @pl.when(f == pl.num_programs(1) - 1)
    def _():
        scale = gt_ref[0] * pw_ref[...]

        def rows_body(r, carry):
            rows = pl.ds(_aligned(r * NORM_ROWS, NORM_ROWS), NORM_ROWS)
            o_ref[rows, :] = x1_ref[rows, :] + _rms(o_ref[rows, :], scale)
            return carry

        lax.fori_loop(0, o_ref.shape[0] // NORM_ROWS, rows_body, 0)


def _ffn(h2, w1, w2, x1, gt, post_w, seq, bm=512, bf=2048):
    m, d = h2.shape
    dff = w1.shape[1]
    per_b = seq // bm
    return pl.pallas_call(
        _ffn_kernel,
        grid=(m // bm, dff // bf),
        in_specs=[pl.BlockSpec((bm, d), lambda i, f: (i, 0)),
                  pl.BlockSpec((d, bf), lambda i, f: (0, f)),
                  pl.BlockSpec((bf, d), lambda i, f: (f, 0)),
                  pl.BlockSpec((bm, d), lambda i, f: (i, 0)),
                  pl.BlockSpec((1, 1, d), lambda i, f: (i // per_b, 0, 0)),
                  pl.BlockSpec((1, d), lambda i, f: (0, 0))],
        out_specs=pl.BlockSpec((bm, d), lambda i, f: (i, 0)),
        out_shape=jax.ShapeDtypeStruct((m, d), F32),
        compiler_params=_cparams(("arbitrary", "arbitrary")),
        name="ffn",
    )(h2, w1, w2, x1, gt, post_w)


def _layer(x, mod, pre_mix_w, post_mix_w, pre_ffn_w, post_ffn_w, w_in, lb_logits, hg_norm_w,
           conv_w, a_log, dt_bias, gdn_norm_w, w_out, w_ff1, w_ff2):
    batch, seq, d = x.shape
    hg_heads = lb_logits.shape[1]
    gdn_heads = a_log.shape[0]
    hg_cols = 4 * hg_heads * HEAD_DIM
    n_main = hg_cols + 4 * gdn_heads * HEAD_DIM

    x2 = x.reshape(batch * seq, d)
    sh_m, sc_m, gt_m, sh_f, sc_f, gt_f = [mod[:batch, None, i * d:(i + 1) * d] for i in range(6)]
    row = lambda v: v.reshape(1, -1)
    pad_lanes = lambda v: jnp.pad(v.reshape(1, -1), ((0, 0), (0, LANES - v.shape[-1])))

    w_in_t = w_in.T
    w_ab_t = jnp.pad(w_in_t[n_main:], ((0, LANES - (w_in_t.shape[0] - n_main)), (0, 0))).astype(BF16)
    h, ab = _prep(x2, row(pre_mix_w), sc_m, sh_m, w_ab_t, seq)
    proj = _inproj(h, w_in_t, n_main)

    o_hg, w_ff2_b, w_out_b = _hgrn(proj, lb_logits.reshape(lb_logits.shape[0], -1), row(hg_norm_w),
                                   [w_ff2, w_out], batch, seq, hg_heads, 0)
    o_gdn, w_ff1_b = _gdn(proj, ab, conv_w, pad_lanes(a_log), pad_lanes(dt_bias), row(gdn_norm_w),
                          [w_ff1], batch, seq, gdn_heads, hg_cols // HEAD_DIM)

    x1, h2 = _outproj(o_hg, o_gdn, w_out_b, x2, gt_m, row(post_mix_w), row(pre_ffn_w), sc_f, sh_f,
                      seq)
    out = _ffn(h2, w_ff1_b, w_ff2_b, x1, gt_f, row(post_ffn_w), seq)
    return out.reshape(batch, seq, d)


def kernel(x, c, w_ada, b_ada, pre_mix_norm, post_mix_norm, pre_ffn_norm, post_ffn_norm, w_in,
           hg_lb_logits, hg_norm, gdn_conv_w, gdn_a_log, gdn_dt_bias, gdn_norm, w_out, w_ff1, w_ff2):
    depth = w_ada.shape[0]
    assert depth == 1 and hg_lb_logits.shape[0] == 2, "single-layer configuration only"
    batch = c.shape[0]
    c_pad = jnp.pad(c, ((0, 8 - batch), (0, 0)))
    for l in range(depth):
        mod = _ada(c_pad, w_ada[l], b_ada[l][None, :])
        x = _layer(x, mod, pre_mix_norm[l], post_mix_norm[l], pre_ffn_norm[l], post_ffn_norm[l],
                   w_in[l], hg_lb_logits, hg_norm[l], gdn_conv_w[l], gdn_a_log[l], gdn_dt_bias[l],
                   gdn_norm[l], w_out[l], w_ff1[l], w_ff2[l])
    return x
```

```python
import functools

import jax
import jax.numpy as jnp
from jax import lax
from jax.experimental import pallas as pl
from jax.experimental.pallas import tpu as pltpu

F32 = jnp.float32
BF16 = jnp.bfloat16
EPS = 1e-6

LANES = 128
HEAD_DIM = 128
CHUNK = 64
SUB = 16
CONV_K = 4
HALO = 8
VMEM_LIMIT = 60 * 1024 * 1024


def _cparams(sem):
    return pltpu.CompilerParams(dimension_semantics=sem, vmem_limit_bytes=VMEM_LIMIT)


def _dot(a, b):
    return jnp.dot(a.astype(BF16), b.astype(BF16), preferred_element_type=F32)


def _dot_nt(a, b):
    return lax.dot_general(a.astype(BF16), b.astype(BF16), (((1,), (1,)), ((), ())),
                           preferred_element_type=F32)


def _dot_tn(a, b):
    return lax.dot_general(a.astype(BF16), b.astype(BF16), (((0,), (0,)), ((), ())),
                           preferred_element_type=F32)


def _split3(x):
    hi = x.astype(BF16)
    r = x - hi.astype(F32)
    mid = r.astype(BF16)
    lo = (r - mid.astype(F32)).astype(BF16)
    return hi, mid, lo


def _dot_exact_lhs(m_bf16, x):
    hi, mid, lo = _split3(x)
    d = lambda p: jnp.dot(m_bf16, p, preferred_element_type=F32)
    return d(hi) + d(mid) + d(lo)


def _dot_exact_rhs(x, m_bf16):
    hi, mid, lo = _split3(x)
    d = lambda p: jnp.dot(p, m_bf16, preferred_element_type=F32)
    return d(hi) + d(mid) + d(lo)


def _sigmoid(x):
    return jax.nn.sigmoid(x)


def _silu(x):
    return x * _sigmoid(x)


def _aligned(x, m):
    return x if isinstance(x, int) else pl.multiple_of(x, m)


def _iota2(shape, dim):
    return lax.broadcasted_iota(jnp.int32, shape, dim)


def _cast_specs(arrays, n_steps, step_index):
    specs, shapes = [], []
    for a in arrays:
        rows, cols = a.shape
        specs.append(pl.BlockSpec((rows // n_steps, cols), lambda *g: (step_index(*g), 0)))
        shapes.append(jax.ShapeDtypeStruct((rows, cols), BF16))
    return specs, shapes


def _cast_slabs(src_refs, dst_refs):
    for src, dst in zip(src_refs, dst_refs):
        dst[...] = src[...].astype(BF16)


def _ada_kernel(c_ref, w_ref, b_ref, o_ref):
    c = c_ref[...]
    o_ref[...] = _dot(_silu(c), w_ref[...]) + b_ref[...]


def _ada(c_pad, w_ada, b_ada, bn=1024):
    rows, d = c_pad.shape
    n = w_ada.shape[1]
    return pl.pallas_call(
        _ada_kernel,
        grid=(n // bn,),
        in_specs=[pl.BlockSpec((rows, d), lambda j: (0, 0)),
                  pl.BlockSpec((d, bn), lambda j: (0, j)),
                  pl.BlockSpec((1, bn), lambda j: (0, j))],
        out_specs=pl.BlockSpec((rows, bn), lambda j: (0, j)),
        out_shape=jax.ShapeDtypeStruct((rows, n), F32),
        compiler_params=_cparams(("arbitrary",)),
        name="ada",
    )(c_pad, w_ada, b_ada)


def _prep_kernel(x_ref, nw_ref, sc_ref, sh_ref, wab_ref, h_ref, ab_ref):
    x = x_ref[...]
    y = x * lax.rsqrt(jnp.mean(x * x, axis=-1, keepdims=True) + EPS)
    hb = ((y * nw_ref[...]) * (1.0 + sc_ref[0]) + sh_ref[0]).astype(BF16)
    h_ref[...] = hb
    ab_ref[...] = _dot_nt(hb, wab_ref[...])


def _prep(x2, nw, sc, sh, w_ab_t, seq, bm=1024):
    m, d = x2.shape
    per_b = seq // bm
    mod = pl.BlockSpec((1, 1, d), lambda i: (i // per_b, 0, 0))
    return pl.pallas_call(
        _prep_kernel,
        grid=(m // bm,),
        in_specs=[pl.BlockSpec((bm, d), lambda i: (i, 0)),
                  pl.BlockSpec((1, d), lambda i: (0, 0)), mod, mod,
                  pl.BlockSpec((LANES, d), lambda i: (0, 0))],
        out_specs=[pl.BlockSpec((bm, d), lambda i: (i, 0)),
                   pl.BlockSpec((bm, LANES), lambda i: (i, 0))],
        out_shape=[jax.ShapeDtypeStruct((m, d), BF16),
                   jax.ShapeDtypeStruct((m, LANES), F32)],
        compiler_params=_cparams(("arbitrary",)),
        name="prep",
    )(x2, nw, sc, sh, w_ab_t)


def _inproj_kernel(h_ref, w_ref, o_ref):
    o_ref[...] = _dot_nt(h_ref[...], w_ref[...])


def _inproj(h, w_t, n, bm=2048, bn=1024):
    m, d = h.shape
    return pl.pallas_call(
        _inproj_kernel,
        grid=(m // bm, n // bn),
        in_specs=[pl.BlockSpec((bm, d), lambda i, j: (i, 0)),
                  pl.BlockSpec((bn, d), lambda i, j: (j, 0))],
        out_specs=pl.BlockSpec((bm, bn), lambda i, j: (i, j)),
        out_shape=jax.ShapeDtypeStruct((m, n), F32),
        compiler_params=_cparams(("arbitrary", "arbitrary")),
        name="inproj",
    )(h, w_t)


def _gated_norm(o, nw, g):
    y = o * lax.rsqrt(jnp.mean(o * o, axis=-1, keepdims=True) + EPS)
    return (y * nw) * _silu(g)


HGRN_ROWS = 8 * CHUNK


def _hgrn_kernel(lbl_ref, q_ref, f_ref, i_ref, g_ref, nw_ref, *rest, heads, tb, n_cast):
    cast_src, (o_ref, *cast_dst) = rest[:n_cast], rest[n_cast:2 * n_cast + 1]
    st_ref, oi_ref, qd_ref, kv_ref, dec_ref = rest[2 * n_cast + 1:]
    _cast_slabs(cast_src, cast_dst)
    n_chunks = tb // CHUNK
    cpi = HGRN_ROWS // CHUNK

    @pl.when(pl.program_id(1) == 0)
    def _():
        st_ref[...] = jnp.zeros_like(st_ref)

    l0 = lbl_ref[0:1, :]
    l1 = lbl_ref[1:2, :]
    mx = jnp.maximum(l0, l1)
    e0 = jnp.exp(l0 - mx)
    lb = e0 / (e0 + jnp.exp(l1 - mx))

    tri = (_iota2((CHUNK, CHUNK), 1) <= _iota2((CHUNK, CHUNK), 0)).astype(BF16)

    def phase1(i, carry):
        rows = pl.ds(_aligned(i * HGRN_ROWS, HGRN_ROWS), HGRN_ROWS)
        f = lb + (1.0 - lb) * _sigmoid(f_ref[rows, :])
        logf = jnp.log(f)
        bc = jnp.concatenate([_dot_exact_lhs(tri, logf[c * CHUNK:(c + 1) * CHUNK]) for c in range(cpi)],
                             axis=0)
        q3 = _problems(q_ref[rows, :], heads, cpi)
        k3 = _problems(1.0 - f, heads, cpi)
        b3 = _problems(bc, heads, cpi)
        v3 = _problems(i_ref[rows, :], heads, cpi).astype(BF16)

        parts = []
        for blk in range(CHUNK // SUB):
            r0 = blk * SUB
            n = r0 + SUB
            p = b3[:, r0 + SUB // 2:r0 + SUB // 2 + 1, :]
            qt = q3[:, r0:n] * jnp.exp(b3[:, r0:n] - p)
            kt = k3[:, 0:n] * jnp.exp(p - b3[:, 0:n])
            a = _bmm_nt(qt, kt)
            keep = _iota2((SUB, n), 1) <= _iota2((SUB, n), 0) + r0
            a = jnp.where(keep, a, 0.0)
            if n < CHUNK:
                a = jnp.concatenate([a, jnp.zeros(a.shape[:2] + (CHUNK - n,), F32)], axis=2)
            parts.append(a)
        b_last = b3[:, CHUNK - 1:CHUNK, :]
        ps = pl.ds(_aligned(i * (cpi * heads), cpi * heads), cpi * heads)
        oi_ref[ps] = _bmm(jnp.concatenate(parts, axis=1), v3)
        qd_ref[ps] = (q3 * jnp.exp(b3)).astype(BF16)
        kv_ref[ps] = _bmm_tn(v3, k3 * jnp.exp(b_last - b3))
        dec_ref[ps] = jnp.exp(b_last)
        return carry

    lax.fori_loop(0, n_chunks // cpi, phase1, 0)

    nw = nw_ref[...]

    def phase2(c, carry):
        ps = pl.ds(_aligned(c * heads, heads), heads)
        rows = pl.ds(_aligned(c * CHUNK, CHUNK), CHUNK)
        st = st_ref[...]
        o = oi_ref[ps] + _bmm_nt(qd_ref[ps], st)
        st_ref[...] = st * dec_ref[ps] + kv_ref[ps]
        for h in range(heads):
            cols = slice(h * HEAD_DIM, (h + 1) * HEAD_DIM)
            o_ref[rows, cols] = _gated_norm(o[h], nw, g_ref[rows, cols]).astype(o_ref.dtype)
        return carry

    lax.fori_loop(0, n_chunks, phase2, 0, unroll=2)


def _hgrn(proj, lb_logits2, norm_w, casts, batch, seq, heads, col0, tb=512):
    per_b = seq // tb
    cast_specs, cast_shapes = _cast_specs(casts, batch * per_b, lambda b, j: b * per_b + j)
    width = heads * HEAD_DIM
    cb0 = col0 // heads

    def col(group):
        return pl.BlockSpec((tb, width), lambda b, j, g=group: (b * per_b + j, cb0 + g))

    n_prob = (tb // CHUNK) * heads
    return pl.pallas_call(
        functools.partial(_hgrn_kernel, heads=heads, tb=tb, n_cast=len(casts)),
        grid=(batch, per_b),
        in_specs=[pl.BlockSpec(lb_logits2.shape, lambda b, j: (0, 0)),
                  col(0), col(1), col(2), col(3),
                  pl.BlockSpec((1, HEAD_DIM), lambda b, j: (0, 0))] + cast_specs,
        out_specs=[pl.BlockSpec((tb, width), lambda b, j: (b * per_b + j, 0))] + cast_specs,
        out_shape=[jax.ShapeDtypeStruct((batch * seq, width), BF16)] + cast_shapes,
        scratch_shapes=[pltpu.VMEM((heads, HEAD_DIM, HEAD_DIM), F32),
                        pltpu.VMEM((n_prob, CHUNK, HEAD_DIM), F32),
                        pltpu.VMEM((n_prob, CHUNK, HEAD_DIM), BF16),
                        pltpu.VMEM((n_prob, HEAD_DIM, HEAD_DIM), F32),
                        pltpu.VMEM((n_prob, 1, HEAD_DIM), F32)],
        compiler_params=_cparams(("arbitrary", "arbitrary")),
        name="hgrn",
    )(lb_logits2, proj, proj, proj, proj, norm_w, *casts)


def _softplus(x):
    return jnp.maximum(x, 0.0) + jnp.log(1.0 + jnp.exp(-jnp.abs(x)))


def _l2norm(x, scale=1.0):
    return x * (lax.rsqrt(jnp.sum(x * x, axis=-1, keepdims=True) + EPS) * scale)


def _bmm(a, b):
    return lax.dot_general(a.astype(BF16), b.astype(BF16), (((2,), (1,)), ((0,), (0,))),
                           preferred_element_type=F32)


def _bmm_nt(a, b):
    return lax.dot_general(a.astype(BF16), b.astype(BF16), (((2,), (2,)), ((0,), (0,))),
                           preferred_element_type=F32)


def _bmm_tn(a, b):
    return lax.dot_general(a.astype(BF16), b.astype(BF16), (((1,), (1,)), ((0,), (0,))),
                           preferred_element_type=F32)


def _conv_silu_tile(x_ref, halo_ref, w, r0, rb):
    cur = x_ref[pl.ds(r0, rb), :]
    prev = x_ref[pl.ds(_aligned(jnp.maximum(r0 - HALO, 0), HALO), HALO), :]
    prev = jnp.where(r0 == 0, halo_ref[...], prev)
    ext = jnp.concatenate([prev, cur], axis=0)
    acc = w[CONV_K - 1:CONV_K, :] * cur
    for j in range(CONV_K - 1):
        s = HALO - (CONV_K - 1) + j
        acc = acc + w[j:j + 1, :] * ext[s:s + rb, :]
    return _silu(acc)


def _problems(x, heads, chunks):
    return jnp.stack([x[c * CHUNK:(c + 1) * CHUNK, h * HEAD_DIM:(h + 1) * HEAD_DIM]
                      for c in range(chunks) for h in range(heads)], axis=0)


GDN_ROWS = 4 * CHUNK


def _gdn_kernel(q_ref, k_ref, v_ref, g_ref, ab_ref, wq_ref, wk_ref, wv_ref, alog_ref, dtb_ref,
                nw_ref, *rest, heads, tb, n_cast):
    cast_src, (o_ref, *cast_dst) = rest[:n_cast], rest[n_cast:2 * n_cast + 1]
    s_ref, hq_ref, hk_ref, hv_ref, u_ref, wqd_ref, qkt_ref, tail_ref = rest[2 * n_cast + 1:]
    _cast_slabs(cast_src, cast_dst)
    n_chunks = tb // CHUNK
    cpi = GDN_ROWS // CHUNK

    @pl.when(pl.program_id(1) == 0)
    def _():
        s_ref[...] = jnp.zeros_like(s_ref)
        hq_ref[...] = jnp.zeros_like(hq_ref)
        hk_ref[...] = jnp.zeros_like(hk_ref)
        hv_ref[...] = jnp.zeros_like(hv_ref)

    width = heads * HEAD_DIM
    row = _iota2((CHUNK, CHUNK), 0)
    colm = _iota2((CHUNK, CHUNK), 1)
    tri = (colm <= row).astype(BF16)
    incl = colm <= row
    strict = colm < row
    sub_shift = SUB.bit_length() - 1
    same_blk = (row >> sub_shift) == (colm >> sub_shift)
    eye = (row == colm).astype(F32)
    neg_diag_blk = jnp.where(same_blk & strict, -1.0, 0.0)
    off_blk = jnp.where(strict & ~same_blk, 1.0, 0.0)
    src = _iota2((LANES, width), 0)
    dst_head = _iota2((LANES, width), 1) >> (HEAD_DIM.bit_length() - 1)
    sel_a = (src == dst_head).astype(BF16)
    sel_b = (src == dst_head + heads).astype(BF16)

    def phase1(i, carry):
        r0 = _aligned(i * GDN_ROWS, GDN_ROWS)
        qa = _conv_silu_tile(q_ref, hq_ref, wq_ref[...], r0, GDN_ROWS)
        ka = _conv_silu_tile(k_ref, hk_ref, wk_ref[...], r0, GDN_ROWS)
        va = _conv_silu_tile(v_ref, hv_ref, wv_ref[...], r0, GDN_ROWS)
        qn = jnp.concatenate([_l2norm(qa[:, h * HEAD_DIM:(h + 1) * HEAD_DIM], HEAD_DIM ** -0.5)
                              for h in range(heads)], axis=1)
        kn = jnp.concatenate([_l2norm(ka[:, h * HEAD_DIM:(h + 1) * HEAD_DIM]) for h in range(heads)],
                             axis=1)

        ab = ab_ref[pl.ds(r0, GDN_ROWS), :]
        la = -jnp.exp(alog_ref[...]) * _softplus(ab + dtb_ref[...])
        be = _sigmoid(ab)
        gc = jnp.concatenate([_dot_exact_lhs(tri, la[c * CHUNK:(c + 1) * CHUNK]) for c in range(cpi)],
                             axis=0)
        g_rep = _dot_exact_rhs(gc, sel_a)
        be_rep = _dot_exact_rhs(be, sel_b)

        q3 = _problems(qn, heads, cpi)
        k3 = _problems(kn, heads, cpi)
        v3 = _problems(va, heads, cpi)
        g3 = _problems(g_rep, heads, cpi)
        b3 = _problems(be_rep, heads, cpi)

        g_sq = g3[:, :, :CHUNK]
        g_row = jnp.sum(g_sq * eye, axis=1, keepdims=True)
        gamma = jnp.exp(jnp.where(incl, g_sq - g_row, -jnp.inf))
        kb = k3.astype(BF16)
        kq = _bmm_nt(jnp.concatenate([kb, q3.astype(BF16)], axis=1), kb)
        mm = kq[:, :CHUNK] * (b3[:, :, :CHUNK] * gamma)
        nd = mm * neg_diag_blk
        off = mm * off_blk
        dinv = eye + nd
        pw = nd
        for _ in range(3):
            pw = _bmm(pw, pw)
            dinv = dinv + _bmm(dinv, pw)
        fm = -_bmm(dinv, off)
        ipf = eye + fm
        tinv = _bmm(ipf + _bmm(ipf, _bmm(fm, fm)), dinv)

        eg = jnp.exp(g3)
        uw = _bmm(tinv, jnp.concatenate([v3 * b3, k3 * (b3 * eg)], axis=2))
        g_last = g3[:, CHUNK - 1:CHUNK, :]
        ps = pl.ds(_aligned(i * (cpi * heads), cpi * heads), cpi * heads)
        u_ref[ps] = uw[:, :, :HEAD_DIM]
        wqd_ref[ps] = jnp.concatenate([uw[:, :, HEAD_DIM:], q3 * eg], axis=1).astype(BF16)
        kt_t = jnp.swapaxes(k3 * jnp.exp(g_last - g3), 1, 2)
        qkt_ref[ps] = jnp.concatenate([kq[:, CHUNK:] * gamma, kt_t], axis=1).astype(BF16)
        tail_ref[ps] = jnp.exp(g_last)
        return carry

    lax.fori_loop(0, n_chunks // cpi, phase1, 0)
    hq_ref[...] = q_ref[tb - HALO:tb, :]
    hk_ref[...] = k_ref[tb - HALO:tb, :]
    hv_ref[...] = v_ref[tb - HALO:tb, :]

    nw = nw_ref[...]

    def phase2(c, carry):
        ps = pl.ds(_aligned(c * heads, heads), heads)
        rows = pl.ds(_aligned(c * CHUNK, CHUNK), CHUNK)
        s = s_ref[...]
        sb = s.astype(BF16)
        ws_qs = _bmm(wqd_ref[ps], sb)
        v_new = u_ref[ps] - ws_qs[:, :CHUNK]
        vb = v_new.astype(BF16)
        ov_ds = _bmm(qkt_ref[ps], vb)
        o = ws_qs[:, CHUNK:] + ov_ds[:, :CHUNK]
        s_ref[...] = s * tail_ref[ps] + ov_ds[:, CHUNK:]
        for h in range(heads):
            cols = slice(h * HEAD_DIM, (h + 1) * HEAD_DIM)
            o_ref[rows, cols] = _gated_norm(o[h], nw, g_ref[rows, cols]).astype(o_ref.dtype)
        return carry

    lax.fori_loop(0, n_chunks, phase2, 0, unroll=2)


def _gdn(proj, ab, conv_w, a_log_pad, dt_bias_pad, norm_w, casts, batch, seq, heads, col0, tb=512):
    per_b = seq // tb
    cast_specs, cast_shapes = _cast_specs(casts, batch * per_b, lambda b, j: b * per_b + j)
    width = heads * HEAD_DIM
    cb0 = col0 // heads

    def col(group):
        return pl.BlockSpec((tb, width), lambda b, j, g=group: (b * per_b + j, cb0 + g))

    def wcol(group):
        return pl.BlockSpec((CONV_K, width), lambda b, j, g=group: (0, g))

    small = pl.BlockSpec((1, LANES), lambda b, j: (0, 0))
    n_prob = (tb // CHUNK) * heads
    halo = pltpu.VMEM((HALO, width), F32)
    prob = lambda n, dt: pltpu.VMEM((n_prob, CHUNK, n), dt)
    return pl.pallas_call(
        functools.partial(_gdn_kernel, heads=heads, tb=tb, n_cast=len(casts)),
        grid=(batch, per_b),
        in_specs=[col(0), col(1), col(2), col(3),
                  pl.BlockSpec((tb, LANES), lambda b, j: (b * per_b + j, 0)),
                  wcol(0), wcol(1), wcol(2), small, small, small] + cast_specs,
        out_specs=[pl.BlockSpec((tb, width), lambda b, j: (b * per_b + j, 0))] + cast_specs,
        out_shape=[jax.ShapeDtypeStruct((batch * seq, width), BF16)] + cast_shapes,
        scratch_shapes=[pltpu.VMEM((heads, HEAD_DIM, HEAD_DIM), F32), halo, halo, halo,
                        prob(HEAD_DIM, F32), pltpu.VMEM((n_prob, 2 * CHUNK, HEAD_DIM), BF16),
                        pltpu.VMEM((n_prob, CHUNK + HEAD_DIM, CHUNK), BF16),
                        pltpu.VMEM((n_prob, 1, HEAD_DIM), F32)],
        compiler_params=_cparams(("arbitrary", "arbitrary")),
        name="gdn",
    )(proj, proj, proj, proj, ab, conv_w, conv_w, conv_w, a_log_pad, dt_bias_pad, norm_w, *casts)


def _rms(y, w):
    return (y * lax.rsqrt(jnp.mean(y * y, axis=-1, keepdims=True) + EPS)) * w


OUT_ROWS = 256


def _outproj_kernel(oh_ref, og_ref, w_ref, x_ref, gt_ref, pw_ref, fw_ref, sc_ref, sh_ref,
                    x1_ref, h2_ref):
    post_scale = gt_ref[0] * pw_ref[...]
    pre_scale = fw_ref[...] * (1.0 + sc_ref[0])
    for r in range(oh_ref.shape[0] // OUT_ROWS):
        rows = pl.ds(r * OUT_ROWS, OUT_ROWS)
        o = jnp.concatenate([oh_ref[rows, :], og_ref[rows, :]], axis=1)
        y = jnp.dot(o, w_ref[...], preferred_element_type=F32)
        x1 = x_ref[rows, :] + _rms(y, post_scale)
        x1_ref[rows, :] = x1
        h2_ref[rows, :] = (_rms(x1, pre_scale) + sh_ref[0]).astype(BF16)


def _outproj(o_hg, o_gdn, w_out, x2, gt, post_w, ffn_w, sc, sh, seq, bm=512):
    m, d = x2.shape
    kh = o_hg.shape[1]
    per_b = seq // bm
    vec = pl.BlockSpec((1, d), lambda i: (0, 0))
    mod = pl.BlockSpec((1, 1, d), lambda i: (i // per_b, 0, 0))
    return pl.pallas_call(
        _outproj_kernel,
        grid=(m // bm,),
        in_specs=[pl.BlockSpec((bm, kh), lambda i: (i, 0)),
                  pl.BlockSpec((bm, kh), lambda i: (i, 0)),
                  pl.BlockSpec((2 * kh, d), lambda i: (0, 0)),
                  pl.BlockSpec((bm, d), lambda i: (i, 0)),
                  mod, vec, vec, mod, mod],
        out_specs=[pl.BlockSpec((bm, d), lambda i: (i, 0)),
                   pl.BlockSpec((bm, d), lambda i: (i, 0))],
        out_shape=[jax.ShapeDtypeStruct((m, d), F32),
                   jax.ShapeDtypeStruct((m, d), BF16)],
        compiler_params=_cparams(("arbitrary",)),
        name="outproj",
    )(o_hg, o_gdn, w_out, x2, gt, post_w, ffn_w, sc, sh)


NORM_ROWS = 128


def _ffn_kernel(h_ref, w1_ref, w2_ref, x1_ref, gt_ref, pw_ref, o_ref):
    f = pl.program_id(1)

    def partial_sum():
        a = jnp.maximum(jnp.dot(h_ref[...], w1_ref[...], preferred_element_type=F32), 0.0)
        return jnp.dot((a * a).astype(BF16), w2_ref[...], preferred_element_type=F32)

    @pl.when(f == 0)
    def _():
        o_ref[...] = partial_sum()

    @pl.when(f > 0)
    def _():
        o_ref[...] += partial_sum()

    @pl.when(f == pl.num_programs(1) - 1)
    def _():
        scale = gt_ref[0] * pw_ref[...]

        def rows_body(r, carry):
            rows = pl.ds(_aligned(r * NORM_ROWS, NORM_ROWS), NORM_ROWS)
            o_ref[rows, :] = x1_ref[rows, :] + _rms(o_ref[rows, :], scale)
            return carry

        lax.fori_loop(0, o_ref.shape[0] // NORM_ROWS, rows_body, 0)


def _ffn(h2, w1, w2, x1, gt, post_w, seq, bm=512, bf=2048):
    m, d = h2.shape
    dff = w1.shape[1]
    per_b = seq // bm
    return pl.pallas_call(
        _ffn_kernel,
        grid=(m // bm, dff // bf),
        in_specs=[pl.BlockSpec((bm, d), lambda i, f: (i, 0)),
                  pl.BlockSpec((d, bf), lambda i, f: (0, f)),
                  pl.BlockSpec((bf, d), lambda i, f: (f, 0)),
                  pl.BlockSpec((bm, d), lambda i, f: (i, 0)),
                  pl.BlockSpec((1, 1, d), lambda i, f: (i // per_b, 0, 0)),
                  pl.BlockSpec((1, d), lambda i, f: (0, 0))],
        out_specs=pl.BlockSpec((bm, d), lambda i, f: (i, 0)),
        out_shape=jax.ShapeDtypeStruct((m, d), F32),
        compiler_params=_cparams(("arbitrary", "arbitrary")),
        name="ffn",
    )(h2, w1, w2, x1, gt, post_w)


def _layer(x, mod, pre_mix_w, post_mix_w, pre_ffn_w, post_ffn_w, w_in, lb_logits, hg_norm_w,
           conv_w, a_log, dt_bias, gdn_norm_w, w_out, w_ff1, w_ff2):
    batch, seq, d = x.shape
    hg_heads = lb_logits.shape[1]
    gdn_heads = a_log.shape[0]
    hg_cols = 4 * hg_heads * HEAD_DIM
    n_main = hg_cols + 4 * gdn_heads * HEAD_DIM

    x2 = x.reshape(batch * seq, d)
    sh_m, sc_m, gt_m, sh_f, sc_f, gt_f = [mod[:batch, None, i * d:(i + 1) * d] for i in range(6)]
    row = lambda v: v.reshape(1, -1)
    pad_lanes = lambda v: jnp.pad(v.reshape(1, -1), ((0, 0), (0, LANES - v.shape[-1])))

    w_in_t = w_in.T
    w_ab_t = jnp.pad(w_in_t[n_main:], ((0, LANES - (w_in_t.shape[0] - n_main)), (0, 0))).astype(BF16)
    h, ab = _prep(x2, row(pre_mix_w), sc_m, sh_m, w_ab_t, seq)
    proj = _inproj(h, w_in_t, n_main)

    o_hg, w_ff2_b, w_out_b = _hgrn(proj, lb_logits.reshape(lb_logits.shape[0], -1), row(hg_norm_w),
                                   [w_ff2, w_out], batch, seq, hg_heads, 0)
    o_gdn, w_ff1_b = _gdn(proj, ab, conv_w, pad_lanes(a_log), pad_lanes(dt_bias), row(gdn_norm_w),
                          [w_ff1], batch, seq, gdn_heads, hg_cols // HEAD_DIM)

    x1, h2 = _outproj(o_hg, o_gdn, w_out_b, x2, gt_m, row(post_mix_w), row(pre_ffn_w), sc_f, sh_f,
                      seq)
    out = _ffn(h2, w_ff1_b, w_ff2_b, x1, gt_f, row(post_ffn_w), seq)
    return out.reshape(batch, seq, d)


def kernel(x, c, w_ada, b_ada, pre_mix_norm, post_mix_norm, pre_ffn_norm, post_ffn_norm, w_in,
           hg_lb_logits, hg_norm, gdn_conv_w, gdn_a_log, gdn_dt_bias, gdn_norm, w_out, w_ff1, w_ff2):
    depth = w_ada.shape[0]
    assert depth == 1 and hg_lb_logits.shape[0] == 2, "single-layer configuration only"
    batch = c.shape[0]
    c_pad = jnp.pad(c, ((0, 8 - batch), (0, 0)))
    for l in range(depth):
        mod = _ada(c_pad, w_ada[l], b_ada[l][None, :])
        x = _layer(x, mod, pre_mix_norm[l], post_mix_norm[l], pre_ffn_norm[l], post_ffn_norm[l],
                   w_in[l], hg_lb_logits, hg_norm[l], gdn_conv_w[l], gdn_a_log[l], gdn_dt_bias[l],
                   gdn_norm[l], w_out[l], w_ff1[l], w_ff2[l])
    return x
```

```python
import functools

import jax
import jax.numpy as jnp
from jax import lax
from jax.experimental import pallas as pl
from jax.experimental.pallas import tpu as pltpu

F32 = jnp.float32
BF16 = jnp.bfloat16
EPS = 1e-6

LANES = 128
HEAD_DIM = 128
CHUNK = 64
SUB = 16
CONV_K = 4
HALO = 8
VMEM_LIMIT = 60 * 1024 * 1024


def _cparams(sem):
    return pltpu.CompilerParams(dimension_semantics=sem, vmem_limit_bytes=VMEM_LIMIT)


def _dot(a, b):
    return jnp.dot(a.astype(BF16), b.astype(BF16), preferred_element_type=F32)


def _dot_nt(a, b):
    return lax.dot_general(a.astype(BF16), b.astype(BF16), (((1,), (1,)), ((), ())),
                           preferred_element_type=F32)


def _split3(x):
    hi = x.astype(BF16)
    r = x - hi.astype(F32)
    mid = r.astype(BF16)
    lo = (r - mid.astype(F32)).astype(BF16)
    return hi, mid, lo


def _dot_exact_lhs(m_bf16, x):
    hi, mid, lo = _split3(x)
    d = lambda p: jnp.dot(m_bf16, p, preferred_element_type=F32)
    return d(hi) + d(mid) + d(lo)


def _dot_exact_rhs(x, m_bf16):
    hi, mid, lo = _split3(x)
    d = lambda p: jnp.dot(p, m_bf16, preferred_element_type=F32)
    return d(hi) + d(mid) + d(lo)


def _sigmoid(x):
    return jax.nn.sigmoid(x)


def _silu(x):
    return x * _sigmoid(x)


def _aligned(x, m):
    return x if isinstance(x, int) else pl.multiple_of(x, m)


def _iota2(shape, dim):
    return lax.broadcasted_iota(jnp.int32, shape, dim)


def _cast_specs(arrays, n_steps, step_index):
    specs, shapes = [], []
    for a in arrays:
        rows, cols = a.shape
        specs.append(pl.BlockSpec((rows // n_steps, cols), lambda *g: (step_index(*g), 0)))
        shapes.append(jax.ShapeDtypeStruct((rows, cols), BF16))
    return specs, shapes


def _cast_slabs(src_refs, dst_refs):
    for src, dst in zip(src_refs, dst_refs):
        dst[...] = src[...].astype(BF16)


def _ada_kernel(c_ref, w_ref, b_ref, o_ref):
    c = c_ref[...]
    o_ref[...] = _dot(_silu(c), w_ref[...]) + b_ref[...]


def _ada(c_pad, w_ada, b_ada, bn=1024):
    rows, d = c_pad.shape
    n = w_ada.shape[1]
    return pl.pallas_call(
        _ada_kernel,
        grid=(n // bn,),
        in_specs=[pl.BlockSpec((rows, d), lambda j: (0, 0)),
                  pl.BlockSpec((d, bn), lambda j: (0, j)),
                  pl.BlockSpec((1, bn), lambda j: (0, j))],
        out_specs=pl.BlockSpec((rows, bn), lambda j: (0, j)),
        out_shape=jax.ShapeDtypeStruct((rows, n), F32),
        compiler_params=_cparams(("arbitrary",)),
        name="ada",
    )(c_pad, w_ada, b_ada)


def _prep_kernel(x_ref, nw_ref, sc_ref, sh_ref, wab_ref, h_ref, ab_ref):
    x = x_ref[...]
    y = x * lax.rsqrt(jnp.mean(x * x, axis=-1, keepdims=True) + EPS)
    hb = ((y * nw_ref[...]) * (1.0 + sc_ref[0]) + sh_ref[0]).astype(BF16)
    h_ref[...] = hb
    ab_ref[...] = _dot_nt(hb, wab_ref[...])


def _prep(x2, nw, sc, sh, w_ab_t, seq, bm=1024):
    m, d = x2.shape
    per_b = seq // bm
    mod = pl.BlockSpec((1, 1, d), lambda i: (i // per_b, 0, 0))
    return pl.pallas_call(
        _prep_kernel,
        grid=(m // bm,),
        in_specs=[pl.BlockSpec((bm, d), lambda i: (i, 0)),
                  pl.BlockSpec((1, d), lambda i: (0, 0)), mod, mod,
                  pl.BlockSpec((LANES, d), lambda i: (0, 0))],
        out_specs=[pl.BlockSpec((bm, d), lambda i: (i, 0)),
                   pl.BlockSpec((bm, LANES), lambda i: (i, 0))],
        out_shape=[jax.ShapeDtypeStruct((m, d), BF16),
                   jax.ShapeDtypeStruct((m, LANES), F32)],
        compiler_params=_cparams(("arbitrary",)),
        name="prep",
    )(x2, nw, sc, sh, w_ab_t)


def _inproj_kernel(h_ref, w_ref, o_ref):
    o_ref[...] = _dot_nt(h_ref[...], w_ref[...])


def _inproj(h, w_t, n, bm=2048, bn=1024):
    m, d = h.shape
    return pl.pallas_call(
        _inproj_kernel,
        grid=(m // bm, n // bn),
        in_specs=[pl.BlockSpec((bm, d), lambda i, j: (i, 0)),
                  pl.BlockSpec((bn, d), lambda i, j: (j, 0))],
        out_specs=pl.BlockSpec((bm, bn), lambda i, j: (i, j)),
        out_shape=jax.ShapeDtypeStruct((m, n), F32),
        compiler_params=_cparams(("arbitrary", "arbitrary")),
        name="inproj",
    )(h, w_t)


def _gated_norm(o, nw, g):
    y = o * lax.rsqrt(jnp.mean(o * o, axis=-1, keepdims=True) + EPS)
    return (y * nw) * _silu(g)


HGRN_ROWS = 8 * CHUNK


def _hgrn_kernel(lbl_ref, q_ref, f_ref, i_ref, g_ref, nw_ref, *rest, heads, tb, n_cast):
    cast_src, (o_ref, *cast_dst) = rest[:n_cast], rest[n_cast:2 * n_cast + 1]
    st_ref, oi_ref, qd_ref, kv_ref, dec_ref = rest[2 * n_cast + 1:]
    _cast_slabs(cast_src, cast_dst)
    n_chunks = tb // CHUNK
    cpi = HGRN_ROWS // CHUNK

    @pl.when(pl.program_id(1) == 0)
    def _():
        st_ref[...] = jnp.zeros_like(st_ref)

    l0 = lbl_ref[0:1, :]
    l1 = lbl_ref[1:2, :]
    mx = jnp.maximum(l0, l1)
    e0 = jnp.exp(l0 - mx)
    lb = e0 / (e0 + jnp.exp(l1 - mx))

    tri = (_iota2((CHUNK, CHUNK), 1) <= _iota2((CHUNK, CHUNK), 0)).astype(BF16)

    def phase1(i, carry):
        rows = pl.ds(_aligned(i * HGRN_ROWS, HGRN_ROWS), HGRN_ROWS)
        f = lb + (1.0 - lb) * _sigmoid(f_ref[rows, :])
        logf = jnp.log(f)
        bc = jnp.concatenate([_dot_exact_lhs(tri, logf[c * CHUNK:(c + 1) * CHUNK]) for c in range(cpi)],
                             axis=0)
        q3 = _problems(q_ref[rows, :], heads, cpi)
        k3 = _problems(1.0 - f, heads, cpi)
        b3 = _problems(bc, heads, cpi)
        v3 = _problems(i_ref[rows, :], heads, cpi).astype(BF16)

        parts = []
        for blk in range(CHUNK // SUB):
            r0 = blk * SUB
            n = r0 + SUB
            p = b3[:, r0 + SUB // 2:r0 + SUB // 2 + 1, :]
            qt = q3[:, r0:n] * jnp.exp(b3[:, r0:n] - p)
            kt = k3[:, 0:n] * jnp.exp(p - b3[:, 0:n])
            a = _bmm_nt(qt, kt)
            keep = _iota2((SUB, n), 1) <= _iota2((SUB, n), 0) + r0
            a = jnp.where(keep, a, 0.0)
            if n < CHUNK:
                a = jnp.concatenate([a, jnp.zeros(a.shape[:2] + (CHUNK - n,), F32)], axis=2)
            parts.append(a)
        b_last = b3[:, CHUNK - 1:CHUNK, :]
        ps = pl.ds(_aligned(i * (cpi * heads), cpi * heads), cpi * heads)
        oi_ref[ps] = _bmm(jnp.concatenate(parts, axis=1), v3)
        qd_ref[ps] = (q3 * jnp.exp(b3)).astype(BF16)
        kv_ref[ps] = _bmm_tn(v3, k3 * jnp.exp(b_last - b3))
        dec_ref[ps] = jnp.exp(b_last)
        return carry

    lax.fori_loop(0, n_chunks // cpi, phase1, 0)

    nw = nw_ref[...]

    def phase2(c, carry):
        ps = pl.ds(_aligned(c * heads, heads), heads)
        rows = pl.ds(_aligned(c * CHUNK, CHUNK), CHUNK)
        st = st_ref[...]
        o = oi_ref[ps] + _bmm_nt(qd_ref[ps], st)
        st_ref[...] = st * dec_ref[ps] + kv_ref[ps]
        for h in range(heads):
            cols = slice(h * HEAD_DIM, (h + 1) * HEAD_DIM)
            o_ref[rows, cols] = _gated_norm(o[h], nw, g_ref[rows, cols]).astype(o_ref.dtype)
        return carry

    lax.fori_loop(0, n_chunks, phase2, 0, unroll=8)


def _hgrn(proj, lb_logits2, norm_w, casts, batch, seq, heads, col0, tb=512):
    per_b = seq // tb
    cast_specs, cast_shapes = _cast_specs(casts, batch * per_b, lambda b, j: b * per_b + j)
    width = heads * HEAD_DIM
    cb0 = col0 // heads

    def col(group):
        return pl.BlockSpec((tb, width), lambda b, j, g=group: (b * per_b + j, cb0 + g))

    n_prob = (tb // CHUNK) * heads
    return pl.pallas_call(
        functools.partial(_hgrn_kernel, heads=heads, tb=tb, n_cast=len(casts)),
        grid=(batch, per_b),
        in_specs=[pl.BlockSpec(lb_logits2.shape, lambda b, j: (0, 0)),
                  col(0), col(1), col(2), col(3),
                  pl.BlockSpec((1, HEAD_DIM), lambda b, j: (0, 0))] + cast_specs,
        out_specs=[pl.BlockSpec((tb, width), lambda b, j: (b * per_b + j, 0))] + cast_specs,
        out_shape=[jax.ShapeDtypeStruct((batch * seq, width), BF16)] + cast_shapes,
        scratch_shapes=[pltpu.VMEM((heads, HEAD_DIM, HEAD_DIM), F32),
                        pltpu.VMEM((n_prob, CHUNK, HEAD_DIM), F32),
                        pltpu.VMEM((n_prob, CHUNK, HEAD_DIM), BF16),
                        pltpu.VMEM((n_prob, HEAD_DIM, HEAD_DIM), F32),
                        pltpu.VMEM((n_prob, 1, HEAD_DIM), F32)],
        compiler_params=_cparams(("arbitrary", "arbitrary")),
        name="hgrn",
    )(lb_logits2, proj, proj, proj, proj, norm_w, *casts)


def _softplus(x):
    return jnp.maximum(x, 0.0) + jnp.log(1.0 + jnp.exp(-jnp.abs(x)))


def _l2norm(x, scale=1.0):
    return x * (lax.rsqrt(jnp.sum(x * x, axis=-1, keepdims=True) + EPS) * scale)


def _bmm(a, b):
    return lax.dot_general(a.astype(BF16), b.astype(BF16), (((2,), (1,)), ((0,), (0,))),
                           preferred_element_type=F32)


def _bmm_nt(a, b):
    return lax.dot_general(a.astype(BF16), b.astype(BF16), (((2,), (2,)), ((0,), (0,))),
                           preferred_element_type=F32)


def _bmm_tn(a, b):
    return lax.dot_general(a.astype(BF16), b.astype(BF16), (((1,), (1,)), ((0,), (0,))),
                           preferred_element_type=F32)


def _conv_silu_tile(x_ref, halo_ref, w, r0, rb):
    cur = x_ref[pl.ds(r0, rb), :]
    prev = x_ref[pl.ds(_aligned(jnp.maximum(r0 - HALO, 0), HALO), HALO), :]
    prev = jnp.where(r0 == 0, halo_ref[...], prev)
    ext = jnp.concatenate([prev, cur], axis=0)
    acc = w[CONV_K - 1:CONV_K, :] * cur
    for j in range(CONV_K - 1):
        s = HALO - (CONV_K - 1) + j
        acc = acc + w[j:j + 1, :] * ext[s:s + rb, :]
    return _silu(acc)


def _problems(x, heads, chunks):
    return jnp.stack([x[c * CHUNK:(c + 1) * CHUNK, h * HEAD_DIM:(h + 1) * HEAD_DIM]
                      for c in range(chunks) for h in range(heads)], axis=0)


GDN_ROWS = 4 * CHUNK


def _gdn_kernel(q_ref, k_ref, v_ref, g_ref, ab_ref, wq_ref, wk_ref, wv_ref, alog_ref, dtb_ref,
                nw_ref, *rest, heads, tb, n_cast):
    cast_src, (o_ref, *cast_dst) = rest[:n_cast], rest[n_cast:2 * n_cast + 1]
    s_ref, hq_ref, hk_ref, hv_ref, u_ref, wqd_ref, qkt_ref, tail_ref = rest[2 * n_cast + 1:]
    _cast_slabs(cast_src, cast_dst)
    n_chunks = tb // CHUNK
    cpi = GDN_ROWS // CHUNK

    @pl.when(pl.program_id(1) == 0)
    def _():
        s_ref[...] = jnp.zeros_like(s_ref)
        hq_ref[...] = jnp.zeros_like(hq_ref)
        hk_ref[...] = jnp.zeros_like(hk_ref)
        hv_ref[...] = jnp.zeros_like(hv_ref)

    width = heads * HEAD_DIM
    row = _iota2((CHUNK, CHUNK), 0)
    colm = _iota2((CHUNK, CHUNK), 1)
    tri = (colm <= row).astype(BF16)
    incl = colm <= row
    strict = colm < row
    sub_shift = SUB.bit_length() - 1
    same_blk = (row >> sub_shift) == (colm >> sub_shift)
    eye = (row == colm).astype(F32)
    neg_diag_blk = jnp.where(same_blk & strict, -1.0, 0.0)
    off_blk = jnp.where(strict & ~same_blk, 1.0, 0.0)
    src = _iota2((LANES, width), 0)
    dst_head = _iota2((LANES, width), 1) >> (HEAD_DIM.bit_length() - 1)
    sel_a = (src == dst_head).astype(BF16)
    sel_b = (src == dst_head + heads).astype(BF16)

    def phase1(i, carry):
        r0 = _aligned(i * GDN_ROWS, GDN_ROWS)
        qa = _conv_silu_tile(q_ref, hq_ref, wq_ref[...], r0, GDN_ROWS)
        ka = _conv_silu_tile(k_ref, hk_ref, wk_ref[...], r0, GDN_ROWS)
        va = _conv_silu_tile(v_ref, hv_ref, wv_ref[...], r0, GDN_ROWS)
        qn = jnp.concatenate([_l2norm(qa[:, h * HEAD_DIM:(h + 1) * HEAD_DIM], HEAD_DIM ** -0.5)
                              for h in range(heads)], axis=1)
        kn = jnp.concatenate([_l2norm(ka[:, h * HEAD_DIM:(h + 1) * HEAD_DIM]) for h in range(heads)],
                             axis=1)

        ab = ab_ref[pl.ds(r0, GDN_ROWS), :]
        la = -jnp.exp(alog_ref[...]) * _softplus(ab + dtb_ref[...])
        be = _sigmoid(ab)
        gc = jnp.concatenate([_dot_exact_lhs(tri, la[c * CHUNK:(c + 1) * CHUNK]) for c in range(cpi)],
                             axis=0)
        g_rep = _dot_exact_rhs(gc, sel_a)
        be_rep = _dot_exact_rhs(be, sel_b)

        q3 = _problems(qn, heads, cpi)
        k3 = _problems(kn, heads, cpi)
        v3 = _problems(va, heads, cpi)
        g3 = _problems(g_rep, heads, cpi)
        b3 = _problems(be_rep, heads, cpi)

        g_sq = g3[:, :, :CHUNK]
        g_row = jnp.sum(g_sq * eye, axis=1, keepdims=True)
        gamma = jnp.exp(jnp.where(incl, g_sq - g_row, -jnp.inf))
        kb = k3.astype(BF16)
        kq = _bmm_nt(jnp.concatenate([kb, q3.astype(BF16)], axis=1), kb)
        mm = kq[:, :CHUNK] * (b3[:, :, :CHUNK] * gamma)
        nd = mm * neg_diag_blk
        off = mm * off_blk
        dinv = eye + nd
        pw = nd
        for _ in range(3):
            pw = _bmm(pw, pw)
            dinv = dinv + _bmm(dinv, pw)
        fm = -_bmm(dinv, off)
        ipf = eye + fm
        tinv = _bmm(ipf + _bmm(ipf, _bmm(fm, fm)), dinv)

        eg = jnp.exp(g3)
        uw = _bmm(tinv, jnp.concatenate([v3 * b3, k3 * (b3 * eg)], axis=2))
        g_last = g3[:, CHUNK - 1:CHUNK, :]
        ps = pl.ds(_aligned(i * (cpi * heads), cpi * heads), cpi * heads)
        u_ref[ps] = uw[:, :, :HEAD_DIM]
        wqd_ref[ps] = jnp.concatenate([uw[:, :, HEAD_DIM:], q3 * eg], axis=1).astype(BF16)
        kt_t = jnp.swapaxes(k3 * jnp.exp(g_last - g3), 1, 2)
        qkt_ref[ps] = jnp.concatenate([kq[:, CHUNK:] * gamma, kt_t], axis=1).astype(BF16)
        tail_ref[ps] = jnp.exp(g_last)
        return carry

    lax.fori_loop(0, n_chunks // cpi, phase1, 0)
    hq_ref[...] = q_ref[tb - HALO:tb, :]
    hk_ref[...] = k_ref[tb - HALO:tb, :]
    hv_ref[...] = v_ref[tb - HALO:tb, :]

    nw = nw_ref[...]

    def phase2(c, carry):
        ps = pl.ds(_aligned(c * heads, heads), heads)
        rows = pl.ds(_aligned(c * CHUNK, CHUNK), CHUNK)
        s = s_ref[...]
        sb = s.astype(BF16)
        ws_qs = _bmm(wqd_ref[ps], sb)
        v_new = u_ref[ps] - ws_qs[:, :CHUNK]
        vb = v_new.astype(BF16)
        ov_ds = _bmm(qkt_ref[ps], vb)
        o = ws_qs[:, CHUNK:] + ov_ds[:, :CHUNK]
        s_ref[...] = s * tail_ref[ps] + ov_ds[:, CHUNK:]
        for h in range(heads):
            cols = slice(h * HEAD_DIM, (h + 1) * HEAD_DIM)
            o_ref[rows, cols] = _gated_norm(o[h], nw, g_ref[rows, cols]).astype(o_ref.dtype)
        return carry

    lax.fori_loop(0, n_chunks, phase2, 0, unroll=8)


def _gdn(proj, ab, conv_w, a_log_pad, dt_bias_pad, norm_w, casts, batch, seq, heads, col0, tb=512):
    per_b = seq // tb
    cast_specs, cast_shapes = _cast_specs(casts, batch * per_b, lambda b, j: b * per_b + j)
    width = heads * HEAD_DIM
    cb0 = col0 // heads

    def col(group):
        return pl.BlockSpec((tb, width), lambda b, j, g=group: (b * per_b + j, cb0 + g))

    def wcol(group):
        return pl.BlockSpec((CONV_K, width), lambda b, j, g=group: (0, g))

    small = pl.BlockSpec((1, LANES), lambda b, j: (0, 0))
    n_prob = (tb // CHUNK) * heads
    halo = pltpu.VMEM((HALO, width), F32)
    prob = lambda n, dt: pltpu.VMEM((n_prob, CHUNK, n), dt)
    return pl.pallas_call(
        functools.partial(_gdn_kernel, heads=heads, tb=tb, n_cast=len(casts)),
        grid=(batch, per_b),
        in_specs=[col(0), col(1), col(2), col(3),
                  pl.BlockSpec((tb, LANES), lambda b, j: (b * per_b + j, 0)),
                  wcol(0), wcol(1), wcol(2), small, small, small] + cast_specs,
        out_specs=[pl.BlockSpec((tb, width), lambda b, j: (b * per_b + j, 0))] + cast_specs,
        out_shape=[jax.ShapeDtypeStruct((batch * seq, width), BF16)] + cast_shapes,
        scratch_shapes=[pltpu.VMEM((heads, HEAD_DIM, HEAD_DIM), F32), halo, halo, halo,
                        prob(HEAD_DIM, F32), pltpu.VMEM((n_prob, 2 * CHUNK, HEAD_DIM), BF16),
                        pltpu.VMEM((n_prob, CHUNK + HEAD_DIM, CHUNK), BF16),
                        pltpu.VMEM((n_prob, 1, HEAD_DIM), F32)],
        compiler_params=_cparams(("arbitrary", "arbitrary")),
        name="gdn",
    )(proj, proj, proj, proj, ab, conv_w, conv_w, conv_w, a_log_pad, dt_bias_pad, norm_w, *casts)


def _rms(y, w):
    return (y * lax.rsqrt(jnp.mean(y * y, axis=-1, keepdims=True) + EPS)) * w


OUT_ROWS = 256


def _outproj_kernel(oh_ref, og_ref, w_ref, x_ref, gt_ref, pw_ref, fw_ref, sc_ref, sh_ref,
                    x1_ref, h2_ref):
    post_scale = gt_ref[0] * pw_ref[...]
    pre_scale = fw_ref[...] * (1.0 + sc_ref[0])
    for r in range(oh_ref.shape[0] // OUT_ROWS):
        rows = pl.ds(r * OUT_ROWS, OUT_ROWS)
        o = jnp.concatenate([oh_ref[rows, :], og_ref[rows, :]], axis=1)
        y = jnp.dot(o, w_ref[...], preferred_element_type=F32)
        x1 = x_ref[rows, :] + _rms(y, post_scale)
        x1_ref[rows, :] = x1
        h2_ref[rows, :] = (_rms(x1, pre_scale) + sh_ref[0]).astype(BF16)


def _outproj(o_hg, o_gdn, w_out, x2, gt, post_w, ffn_w, sc, sh, seq, bm=512):
    m, d = x2.shape
    kh = o_hg.shape[1]
    per_b = seq // bm
    vec = pl.BlockSpec((1, d), lambda i: (0, 0))
    mod = pl.BlockSpec((1, 1, d), lambda i: (i // per_b, 0, 0))
    return pl.pallas_call(
        _outproj_kernel,
        grid=(m // bm,),
        in_specs=[pl.BlockSpec((bm, kh), lambda i: (i, 0)),
                  pl.BlockSpec((bm, kh), lambda i: (i, 0)),
                  pl.BlockSpec((2 * kh, d), lambda i: (0, 0)),
                  pl.BlockSpec((bm, d), lambda i: (i, 0)),
                  mod, vec, vec, mod, mod],
        out_specs=[pl.BlockSpec((bm, d), lambda i: (i, 0)),
                   pl.BlockSpec((bm, d), lambda i: (i, 0))],
        out_shape=[jax.ShapeDtypeStruct((m, d), F32),
                   jax.ShapeDtypeStruct((m, d), BF16)],
        compiler_params=_cparams(("arbitrary",)),
        name="outproj",
    )(o_hg, o_gdn, w_out, x2, gt, post_w, ffn_w, sc, sh)


NORM_ROWS = 128


def _ffn_kernel(h_ref, w1_ref, w2_ref, x1_ref, gt_ref, pw_ref, o_ref):
    f = pl.program_id(1)

    @pl.when(f == 0)
    def _():
        o_ref[...] = jnp.zeros_like(o_ref)

    a = jnp.maximum(jnp.dot(h_ref[...], w1_ref[...], preferred_element_type=F32), 0.0)
    o_ref[...] += jnp.dot((a * a).astype(BF16), w2_ref[...], preferred_element_type=F32)

    @pl.when(f == pl.num_programs(1) - 1)
    def _():
        scale = gt_ref[0] * pw_ref[...]

        def rows_body(r, carry):
            rows = pl.ds(_aligned(r * NORM_ROWS, NORM_ROWS), NORM_ROWS)
            o_ref[rows, :] = x1_ref[rows, :] + _rms(o_ref[rows, :], scale)
            return carry

        lax.fori_loop(0, o_ref.shape[0] // NORM_ROWS, rows_body, 0)


def _ffn(h2, w1, w2, x1, gt, post_w, seq, bm=512, bf=2048):
    m, d = h2.shape
    dff = w1.shape[1]
    per_b = seq // bm
    return pl.pallas_call(
        _ffn_kernel,
        grid=(m // bm, dff // bf),
        in_specs=[pl.BlockSpec((bm, d), lambda i, f: (i, 0)),
                  pl.BlockSpec((d, bf), lambda i, f: (0, f)),
                  pl.BlockSpec((bf, d), lambda i, f: (f, 0)),
                  pl.BlockSpec((bm, d), lambda i, f: (i, 0)),
                  pl.BlockSpec((1, 1, d), lambda i, f: (i // per_b, 0, 0)),
                  pl.BlockSpec((1, d), lambda i, f: (0, 0))],
        out_specs=pl.BlockSpec((bm, d), lambda i, f: (i, 0)),
        out_shape=jax.ShapeDtypeStruct((m, d), F32),
        compiler_params=_cparams(("arbitrary", "arbitrary")),
        name="ffn",
    )(h2, w1, w2, x1, gt, post_w)


def _layer(x, mod, pre_mix_w, post_mix_w, pre_ffn_w, post_ffn_w, w_in, lb_logits, hg_norm_w,
           conv_w, a_log, dt_bias, gdn_norm_w, w_out, w_ff1, w_ff2):
    batch, seq, d = x.shape
    hg_heads = lb_logits.shape[1]
    gdn_heads = a_log.shape[0]
    hg_cols = 4 * hg_heads * HEAD_DIM
    n_main = hg_cols + 4 * gdn_heads * HEAD_DIM

    x2 = x.reshape(batch * seq, d)
    sh_m, sc_m, gt_m, sh_f, sc_f, gt_f = [mod[:batch, None, i * d:(i + 1) * d] for i in range(6)]
    row = lambda v: v.reshape(1, -1)
    pad_lanes = lambda v: jnp.pad(v.reshape(1, -1), ((0, 0), (0, LANES - v.shape[-1])))

    w_in_t = w_in.T
    w_ab_t = jnp.pad(w_in_t[n_main:], ((0, LANES - (w_in_t.shape[0] - n_main)), (0, 0))).astype(BF16)
    h, ab = _prep(x2, row(pre_mix_w), sc_m, sh_m, w_ab_t, seq)
    proj = _inproj(h, w_in_t, n_main)

    o_hg, w_ff2_b, w_out_b = _hgrn(proj, lb_logits.reshape(lb_logits.shape[0], -1), row(hg_norm_w),
                                   [w_ff2, w_out], batch, seq, hg_heads, 0)
    o_gdn, w_ff1_b = _gdn(proj, ab, conv_w, pad_lanes(a_log), pad_lanes(dt_bias), row(gdn_norm_w),
                          [w_ff1], batch, seq, gdn_heads, hg_cols // HEAD_DIM)

    x1, h2 = _outproj(o_hg, o_gdn, w_out_b, x2, gt_m, row(post_mix_w), row(pre_ffn_w), sc_f, sh_f,
                      seq)
    out = _ffn(h2, w_ff1_b, w_ff2_b, x1, gt_f, row(post_ffn_w), seq)
    return out.reshape(batch, seq, d)


def kernel(x, c, w_ada, b_ada, pre_mix_norm, post_mix_norm, pre_ffn_norm, post_ffn_norm, w_in,
           hg_lb_logits, hg_norm, gdn_conv_w, gdn_a_log, gdn_dt_bias, gdn_norm, w_out, w_ff1, w_ff2):
    depth = w_ada.shape[0]
    assert depth == 1 and hg_lb_logits.shape[0] == 2, "single-layer configuration only"
    batch = c.shape[0]
    c_pad = jnp.pad(c, ((0, 8 - batch), (0, 0)))
    for l in range(depth):
        mod = _ada(c_pad, w_ada[l], b_ada[l][None, :])
        x = _layer(x, mod, pre_mix_norm[l], post_mix_norm[l], pre_ffn_norm[l], post_ffn_norm[l],
                   w_in[l], hg_lb_logits, hg_norm[l], gdn_conv_w[l], gdn_a_log[l], gdn_dt_bias[l],
                   gdn_norm[l], w_out[l], w_ff1[l], w_ff2[l])
    return x
```

```python
import functools

import jax
import jax.numpy as jnp
from jax import lax
from jax.experimental import pallas as pl
from jax.experimental.pallas import tpu as pltpu

F32 = jnp.float32
BF16 = jnp.bfloat16
EPS = 1e-6

LANES = 128
HEAD_DIM = 128
CHUNK = 64
SUB = 16
CONV_K = 4
HALO = 8
VMEM_LIMIT = 60 * 1024 * 1024


def _cparams(sem):
    return pltpu.CompilerParams(dimension_semantics=sem, vmem_limit_bytes=VMEM_LIMIT)


def _dot(a, b):
    return jnp.dot(a.astype(BF16), b.astype(BF16), preferred_element_type=F32)


def _dot_nt(a, b):
    return lax.dot_general(a.astype(BF16), b.astype(BF16), (((1,), (1,)), ((), ())),
                           preferred_element_type=F32)


def _split3(x):
    hi = x.astype(BF16)
    r = x - hi.astype(F32)
    mid = r.astype(BF16)
    lo = (r - mid.astype(F32)).astype(BF16)
    return hi, mid, lo


def _dot_exact_lhs(m_bf16, x):
    hi, mid, lo = _split3(x)
    d = lambda p: jnp.dot(m_bf16, p, preferred_element_type=F32)
    return d(hi) + d(mid) + d(lo)


def _dot_exact_rhs(x, m_bf16):
    hi, mid, lo = _split3(x)
    d = lambda p: jnp.dot(p, m_bf16, preferred_element_type=F32)
    return d(hi) + d(mid) + d(lo)


def _sigmoid(x):
    return jax.nn.sigmoid(x)


def _silu(x):
    return x * _sigmoid(x)


def _aligned(x, m):
    return x if isinstance(x, int) else pl.multiple_of(x, m)


def _iota2(shape, dim):
    return lax.broadcasted_iota(jnp.int32, shape, dim)


def _cast_specs(arrays, n_steps, step_index):
    specs, shapes = [], []
    for a in arrays:
        rows, cols = a.shape
        specs.append(pl.BlockSpec((rows // n_steps, cols), lambda *g: (step_index(*g), 0)))
        shapes.append(jax.ShapeDtypeStruct((rows, cols), BF16))
    return specs, shapes


def _cast_slabs(src_refs, dst_refs):
    for src, dst in zip(src_refs, dst_refs):
        dst[...] = src[...].astype(BF16)


def _ada_kernel(c_ref, w_ref, b_ref, o_ref):
    c = c_ref[...]
    o_ref[...] = _dot(_silu(c), w_ref[...]) + b_ref[...]


def _ada(c_pad, w_ada, b_ada, bn=1024):
    rows, d = c_pad.shape
    n = w_ada.shape[1]
    return pl.pallas_call(
        _ada_kernel,
        grid=(n // bn,),
        in_specs=[pl.BlockSpec((rows, d), lambda j: (0, 0)),
                  pl.BlockSpec((d, bn), lambda j: (0, j)),
                  pl.BlockSpec((1, bn), lambda j: (0, j))],
        out_specs=pl.BlockSpec((rows, bn), lambda j: (0, j)),
        out_shape=jax.ShapeDtypeStruct((rows, n), F32),
        compiler_params=_cparams(("arbitrary",)),
        name="ada",
    )(c_pad, w_ada, b_ada)


def _prep_kernel(x_ref, nw_ref, sc_ref, sh_ref, wab_ref, h_ref, ab_ref):
    x = x_ref[...]
    y = x * lax.rsqrt(jnp.mean(x * x, axis=-1, keepdims=True) + EPS)
    hb = ((y * nw_ref[...]) * (1.0 + sc_ref[0]) + sh_ref[0]).astype(BF16)
    h_ref[...] = hb
    ab_ref[...] = _dot_nt(hb, wab_ref[...])


def _prep(x2, nw, sc, sh, w_ab_t, seq, bm=1024):
    m, d = x2.shape
    per_b = seq // bm
    mod = pl.BlockSpec((1, 1, d), lambda i: (i // per_b, 0, 0))
    return pl.pallas_call(
        _prep_kernel,
        grid=(m // bm,),
        in_specs=[pl.BlockSpec((bm, d), lambda i: (i, 0)),
                  pl.BlockSpec((1, d), lambda i: (0, 0)), mod, mod,
                  pl.BlockSpec((LANES, d), lambda i: (0, 0))],
        out_specs=[pl.BlockSpec((bm, d), lambda i: (i, 0)),
                   pl.BlockSpec((bm, LANES), lambda i: (i, 0))],
        out_shape=[jax.ShapeDtypeStruct((m, d), BF16),
                   jax.ShapeDtypeStruct((m, LANES), F32)],
        compiler_params=_cparams(("arbitrary",)),
        name="prep",
    )(x2, nw, sc, sh, w_ab_t)


def _inproj_kernel(h_ref, w_ref, o_ref):
    o_ref[...] = _dot_nt(h_ref[...], w_ref[...])


def _inproj(h, w_t, n, bm=2048, bn=1024):
    m, d = h.shape
    return pl.pallas_call(
        _inproj_kernel,
        grid=(m // bm, n // bn),
        in_specs=[pl.BlockSpec((bm, d), lambda i, j: (i, 0)),
                  pl.BlockSpec((bn, d), lambda i, j: (j, 0))],
        out_specs=pl.BlockSpec((bm, bn), lambda i, j: (i, j)),
        out_shape=jax.ShapeDtypeStruct((m, n), F32),
        compiler_params=_cparams(("arbitrary", "arbitrary")),
        name="inproj",
    )(h, w_t)


def _gated_norm(o, nw, g):
    y = o * lax.rsqrt(jnp.mean(o * o, axis=-1, keepdims=True) + EPS)
    return (y * nw) * _silu(g)


HGRN_ROWS = 8 * CHUNK


def _hgrn_kernel(lbl_ref, q_ref, f_ref, i_ref, g_ref, nw_ref, *rest, heads, tb, n_cast):
    cast_src, (o_ref, *cast_dst) = rest[:n_cast], rest[n_cast:2 * n_cast + 1]
    st_ref, oi_ref, qd_ref, kv_ref, dec_ref = rest[2 * n_cast + 1:]
    _cast_slabs(cast_src, cast_dst)
    n_chunks = tb // CHUNK
    cpi = HGRN_ROWS // CHUNK

    @pl.when(pl.program_id(1) == 0)
    def _():
        st_ref[...] = jnp.zeros_like(st_ref)

    l0 = lbl_ref[0:1, :]
    l1 = lbl_ref[1:2, :]
    mx = jnp.maximum(l0, l1)
    e0 = jnp.exp(l0 - mx)
    lb = e0 / (e0 + jnp.exp(l1 - mx))

    tri = (_iota2((CHUNK, CHUNK), 1) <= _iota2((CHUNK, CHUNK), 0)).astype(BF16)

    def phase1(i, carry):
        rows = pl.ds(_aligned(i * HGRN_ROWS, HGRN_ROWS), HGRN_ROWS)
        f = lb + (1.0 - lb) * _sigmoid(f_ref[rows, :])
        logf = jnp.log(f)
        bc = jnp.concatenate([_dot_exact_lhs(tri, logf[c * CHUNK:(c + 1) * CHUNK]) for c in range(cpi)],
                             axis=0)
        q3 = _problems(q_ref[rows, :], heads, cpi)
        k3 = _problems(1.0 - f, heads, cpi)
        b3 = _problems(bc, heads, cpi)
        v3 = _problems(i_ref[rows, :], heads, cpi).astype(BF16)

        parts = []
        for blk in range(CHUNK // SUB):
            r0 = blk * SUB
            n = r0 + SUB
            p = b3[:, r0 + SUB // 2:r0 + SUB // 2 + 1, :]
            qt = q3[:, r0:n] * jnp.exp(b3[:, r0:n] - p)
            kt = k3[:, 0:n] * jnp.exp(p - b3[:, 0:n])
            a = _bmm_nt(qt, kt)
            keep = _iota2((SUB, n), 1) <= _iota2((SUB, n), 0) + r0
            a = jnp.where(keep, a, 0.0)
            if n < CHUNK:
                a = jnp.concatenate([a, jnp.zeros(a.shape[:2] + (CHUNK - n,), F32)], axis=2)
            parts.append(a)
        b_last = b3[:, CHUNK - 1:CHUNK, :]
        ps = pl.ds(_aligned(i * (cpi * heads), cpi * heads), cpi * heads)
        oi_ref[ps] = _bmm(jnp.concatenate(parts, axis=1), v3)
        qd_ref[ps] = (q3 * jnp.exp(b3)).astype(BF16)
        kv_ref[ps] = _bmm_tn(v3, k3 * jnp.exp(b_last - b3))
        dec_ref[ps] = jnp.exp(b_last)
        return carry

    lax.fori_loop(0, n_chunks // cpi, phase1, 0)

    nw = nw_ref[...]

    def phase2(c, carry):
        ps = pl.ds(_aligned(c * heads, heads), heads)
        rows = pl.ds(_aligned(c * CHUNK, CHUNK), CHUNK)
        st = st_ref[...]
        o = oi_ref[ps] + _bmm_nt(qd_ref[ps], st)
        st_ref[...] = st * dec_ref[ps] + kv_ref[ps]
        for h in range(heads):
            cols = slice(h * HEAD_DIM, (h + 1) * HEAD_DIM)
            o_ref[rows, cols] = _gated_norm(o[h], nw, g_ref[rows, cols]).astype(o_ref.dtype)
        return carry

    lax.fori_loop(0, n_chunks, phase2, 0, unroll=8)


def _hgrn(proj, lb_logits2, norm_w, casts, batch, seq, heads, col0, tb=512):
    per_b = seq // tb
    cast_specs, cast_shapes = _cast_specs(casts, batch * per_b, lambda b, j: b * per_b + j)
    width = heads * HEAD_DIM
    cb0 = col0 // heads

    def col(group):
        return pl.BlockSpec((tb, width), lambda b, j, g=group: (b * per_b + j, cb0 + g))

    n_prob = (tb // CHUNK) * heads
    return pl.pallas_call(
        functools.partial(_hgrn_kernel, heads=heads, tb=tb, n_cast=len(casts)),
        grid=(batch, per_b),
        in_specs=[pl.BlockSpec(lb_logits2.shape, lambda b, j: (0, 0)),
                  col(0), col(1), col(2), col(3),
                  pl.BlockSpec((1, HEAD_DIM), lambda b, j: (0, 0))] + cast_specs,
        out_specs=[pl.BlockSpec((tb, width), lambda b, j: (b * per_b + j, 0))] + cast_specs,
        out_shape=[jax.ShapeDtypeStruct((batch * seq, width), BF16)] + cast_shapes,
        scratch_shapes=[pltpu.VMEM((heads, HEAD_DIM, HEAD_DIM), F32),
                        pltpu.VMEM((n_prob, CHUNK, HEAD_DIM), F32),
                        pltpu.VMEM((n_prob, CHUNK, HEAD_DIM), BF16),
                        pltpu.VMEM((n_prob, HEAD_DIM, HEAD_DIM), F32),
                        pltpu.VMEM((n_prob, 1, HEAD_DIM), F32)],
        compiler_params=_cparams(("arbitrary", "arbitrary")),
        name="hgrn",
    )(lb_logits2, proj, proj, proj, proj, norm_w, *casts)


def _softplus(x):
    return jnp.maximum(x, 0.0) + jnp.log(1.0 + jnp.exp(-jnp.abs(x)))


def _l2norm(x, scale=1.0):
    return x * (lax.rsqrt(jnp.sum(x * x, axis=-1, keepdims=True) + EPS) * scale)


def _bmm(a, b):
    return lax.dot_general(a.astype(BF16), b.astype(BF16), (((2,), (1,)), ((0,), (0,))),
                           preferred_element_type=F32)


def _bmm_nt(a, b):
    return lax.dot_general(a.astype(BF16), b.astype(BF16), (((2,), (2,)), ((0,), (0,))),
                           preferred_element_type=F32)


def _bmm_tn(a, b):
    return lax.dot_general(a.astype(BF16), b.astype(BF16), (((1,), (1,)), ((0,), (0,))),
                           preferred_element_type=F32)


def _conv_silu_tile(x_ref, halo_ref, w, r0, rb):
    cur = x_ref[pl.ds(r0, rb), :]
    prev = x_ref[pl.ds(_aligned(jnp.maximum(r0 - HALO, 0), HALO), HALO), :]
    prev = jnp.where(r0 == 0, halo_ref[...], prev)
    ext = jnp.concatenate([prev, cur], axis=0)
    acc = w[CONV_K - 1:CONV_K, :] * cur
    for j in range(CONV_K - 1):
        s = HALO - (CONV_K - 1) + j
        acc = acc + w[j:j + 1, :] * ext[s:s + rb, :]
    return _silu(acc)


def _problems(x, heads, chunks):
    return jnp.stack([x[c * CHUNK:(c + 1) * CHUNK, h * HEAD_DIM:(h + 1) * HEAD_DIM]
                      for c in range(chunks) for h in range(heads)], axis=0)


GDN_ROWS = 4 * CHUNK


def _gdn_kernel(q_ref, k_ref, v_ref, g_ref, ab_ref, wq_ref, wk_ref, wv_ref, alog_ref, dtb_ref,
                nw_ref, *rest, heads, tb, n_cast):
    cast_src, (o_ref, *cast_dst) = rest[:n_cast], rest[n_cast:2 * n_cast + 1]
    s_ref, hq_ref, hk_ref, hv_ref, u_ref, wqd_ref, qkt_ref, tail_ref = rest[2 * n_cast + 1:]
    _cast_slabs(cast_src, cast_dst)
    n_chunks = tb // CHUNK
    cpi = GDN_ROWS // CHUNK

    @pl.when(pl.program_id(1) == 0)
    def _():
        s_ref[...] = jnp.zeros_like(s_ref)
        hq_ref[...] = jnp.zeros_like(hq_ref)
        hk_ref[...] = jnp.zeros_like(hk_ref)
        hv_ref[...] = jnp.zeros_like(hv_ref)

    width = heads * HEAD_DIM
    row = _iota2((CHUNK, CHUNK), 0)
    colm = _iota2((CHUNK, CHUNK), 1)
    tri = (colm <= row).astype(BF16)
    incl = colm <= row
    strict = colm < row
    sub_shift = SUB.bit_length() - 1
    same_blk = (row >> sub_shift) == (colm >> sub_shift)
    eye = (row == colm).astype(F32)
    neg_diag_blk = jnp.where(same_blk & strict, -1.0, 0.0)
    off_blk = jnp.where(strict & ~same_blk, 1.0, 0.0)
    src = _iota2((LANES, width), 0)
    dst_head = _iota2((LANES, width), 1) >> (HEAD_DIM.bit_length() - 1)
    sel_a = (src == dst_head).astype(BF16)
    sel_b = (src == dst_head + heads).astype(BF16)

    def phase1(i, carry):
        r0 = _aligned(i * GDN_ROWS, GDN_ROWS)
        qa = _conv_silu_tile(q_ref, hq_ref, wq_ref[...], r0, GDN_ROWS)
        ka = _conv_silu_tile(k_ref, hk_ref, wk_ref[...], r0, GDN_ROWS)
        va = _conv_silu_tile(v_ref, hv_ref, wv_ref[...], r0, GDN_ROWS)
        qn = jnp.concatenate([_l2norm(qa[:, h * HEAD_DIM:(h + 1) * HEAD_DIM], HEAD_DIM ** -0.5)
                              for h in range(heads)], axis=1)
        kn = jnp.concatenate([_l2norm(ka[:, h * HEAD_DIM:(h + 1) * HEAD_DIM]) for h in range(heads)],
                             axis=1)

        ab = ab_ref[pl.ds(r0, GDN_ROWS), :]
        la = -jnp.exp(alog_ref[...]) * _softplus(ab + dtb_ref[...])
        be = _sigmoid(ab)
        gc = jnp.concatenate([_dot_exact_lhs(tri, la[c * CHUNK:(c + 1) * CHUNK]) for c in range(cpi)],
                             axis=0)
        g_rep = _dot_exact_rhs(gc, sel_a)
        be_rep = _dot_exact_rhs(be, sel_b)

        q3 = _problems(qn, heads, cpi)
        k3 = _problems(kn, heads, cpi)
        v3 = _problems(va, heads, cpi)
        g3 = _problems(g_rep, heads, cpi)
        b3 = _problems(be_rep, heads, cpi)

        g_sq = g3[:, :, :CHUNK]
        g_row = jnp.sum(g_sq * eye, axis=1, keepdims=True)
        gamma = jnp.exp(jnp.where(incl, g_sq - g_row, -jnp.inf))
        kb = k3.astype(BF16)
        kq = _bmm_nt(jnp.concatenate([kb, q3.astype(BF16)], axis=1), kb)
        mm = kq[:, :CHUNK] * (b3[:, :, :CHUNK] * gamma)
        nd = mm * neg_diag_blk
        off = mm * off_blk
        dinv = eye + nd
        pw = _bmm(nd, nd)
        for _ in range(2):
            both = _bmm(jnp.concatenate([dinv, pw], axis=1), pw)
            dinv = dinv + both[:, :CHUNK]
            pw = both[:, CHUNK:]
        dinv = dinv + _bmm(dinv, pw)
        fm = -_bmm(dinv, off)
        ipf = eye + fm
        tinv = _bmm(ipf + _bmm(ipf, _bmm(fm, fm)), dinv)

        eg = jnp.exp(g3)
        uw = _bmm(tinv, jnp.concatenate([v3 * b3, k3 * (b3 * eg)], axis=2))
        g_last = g3[:, CHUNK - 1:CHUNK, :]
        ps = pl.ds(_aligned(i * (cpi * heads), cpi * heads), cpi * heads)
        u_ref[ps] = uw[:, :, :HEAD_DIM]
        wqd_ref[ps] = jnp.concatenate([uw[:, :, HEAD_DIM:], q3 * eg], axis=1).astype(BF16)
        kt_t = jnp.swapaxes(k3 * jnp.exp(g_last - g3), 1, 2)
        qkt_ref[ps] = jnp.concatenate([kq[:, CHUNK:] * gamma, kt_t], axis=1).astype(BF16)
        tail_ref[ps] = jnp.exp(g_last)
        return carry

    lax.fori_loop(0, n_chunks // cpi, phase1, 0)
    hq_ref[...] = q_ref[tb - HALO:tb, :]
    hk_ref[...] = k_ref[tb - HALO:tb, :]
    hv_ref[...] = v_ref[tb - HALO:tb, :]

    nw = nw_ref[...]

    def phase2(c, carry):
        ps = pl.ds(_aligned(c * heads, heads), heads)
        rows = pl.ds(_aligned(c * CHUNK, CHUNK), CHUNK)
        s = s_ref[...]
        sb = s.astype(BF16)
        ws_qs = _bmm(wqd_ref[ps], sb)
        v_new = u_ref[ps] - ws_qs[:, :CHUNK]
        vb = v_new.astype(BF16)
        ov_ds = _bmm(qkt_ref[ps], vb)
        o = ws_qs[:, CHUNK:] + ov_ds[:, :CHUNK]
        s_ref[...] = s * tail_ref[ps] + ov_ds[:, CHUNK:]
        for h in range(heads):
            cols = slice(h * HEAD_DIM, (h + 1) * HEAD_DIM)
            o_ref[rows, cols] = _gated_norm(o[h], nw, g_ref[rows, cols]).astype(o_ref.dtype)
        return carry

    lax.fori_loop(0, n_chunks, phase2, 0, unroll=8)


def _gdn(proj, ab, conv_w, a_log_pad, dt_bias_pad, norm_w, casts, batch, seq, heads, col0, tb=512):
    per_b = seq // tb
    cast_specs, cast_shapes = _cast_specs(casts, batch * per_b, lambda b, j: b * per_b + j)
    width = heads * HEAD_DIM
    cb0 = col0 // heads

    def col(group):
        return pl.BlockSpec((tb, width), lambda b, j, g=group: (b * per_b + j, cb0 + g))

    def wcol(group):
        return pl.BlockSpec((CONV_K, width), lambda b, j, g=group: (0, g))

    small = pl.BlockSpec((1, LANES), lambda b, j: (0, 0))
    n_prob = (tb // CHUNK) * heads
    halo = pltpu.VMEM((HALO, width), F32)
    prob = lambda n, dt: pltpu.VMEM((n_prob, CHUNK, n), dt)
    return pl.pallas_call(
        functools.partial(_gdn_kernel, heads=heads, tb=tb, n_cast=len(casts)),
        grid=(batch, per_b),
        in_specs=[col(0), col(1), col(2), col(3),
                  pl.BlockSpec((tb, LANES), lambda b, j: (b * per_b + j, 0)),
                  wcol(0), wcol(1), wcol(2), small, small, small] + cast_specs,
        out_specs=[pl.BlockSpec((tb, width), lambda b, j: (b * per_b + j, 0))] + cast_specs,
        out_shape=[jax.ShapeDtypeStruct((batch * seq, width), BF16)] + cast_shapes,
        scratch_shapes=[pltpu.VMEM((heads, HEAD_DIM, HEAD_DIM), F32), halo, halo, halo,
                        prob(HEAD_DIM, F32), pltpu.VMEM((n_prob, 2 * CHUNK, HEAD_DIM), BF16),
                        pltpu.VMEM((n_prob, CHUNK + HEAD_DIM, CHUNK), BF16),
                        pltpu.VMEM((n_prob, 1, HEAD_DIM), F32)],
        compiler_params=_cparams(("arbitrary", "arbitrary")),
        name="gdn",
    )(proj, proj, proj, proj, ab, conv_w, conv_w, conv_w, a_log_pad, dt_bias_pad, norm_w, *casts)


def _rms(y, w):
    return (y * lax.rsqrt(jnp.mean(y * y, axis=-1, keepdims=True) + EPS)) * w


OUT_ROWS = 256


def _outproj_kernel(oh_ref, og_ref, w_ref, x_ref, gt_ref, pw_ref, fw_ref, sc_ref, sh_ref,
                    x1_ref, h2_ref):
    post_scale = gt_ref[0] * pw_ref[...]
    pre_scale = fw_ref[...] * (1.0 + sc_ref[0])
    for r in range(oh_ref.shape[0] // OUT_ROWS):
        rows = pl.ds(r * OUT_ROWS, OUT_ROWS)
        o = jnp.concatenate([oh_ref[rows, :], og_ref[rows, :]], axis=1)
        y = jnp.dot(o, w_ref[...], preferred_element_type=F32)
        x1 = x_ref[rows, :] + _rms(y, post_scale)
        x1_ref[rows, :] = x1
        h2_ref[rows, :] = (_rms(x1, pre_scale) + sh_ref[0]).astype(BF16)


def _outproj(o_hg, o_gdn, w_out, x2, gt, post_w, ffn_w, sc, sh, seq, bm=512):
    m, d = x2.shape
    kh = o_hg.shape[1]
    per_b = seq // bm
    vec = pl.BlockSpec((1, d), lambda i: (0, 0))
    mod = pl.BlockSpec((1, 1, d), lambda i: (i // per_b, 0, 0))
    return pl.pallas_call(
        _outproj_kernel,
        grid=(m // bm,),
        in_specs=[pl.BlockSpec((bm, kh), lambda i: (i, 0)),
                  pl.BlockSpec((bm, kh), lambda i: (i, 0)),
                  pl.BlockSpec((2 * kh, d), lambda i: (0, 0)),
                  pl.BlockSpec((bm, d), lambda i: (i, 0)),
                  mod, vec, vec, mod, mod],
        out_specs=[pl.BlockSpec((bm, d), lambda i: (i, 0)),
                   pl.BlockSpec((bm, d), lambda i: (i, 0))],
        out_shape=[jax.ShapeDtypeStruct((m, d), F32),
                   jax.ShapeDtypeStruct((m, d), BF16)],
        compiler_params=_cparams(("arbitrary",)),
        name="outproj",
    )(o_hg, o_gdn, w_out, x2, gt, post_w, ffn_w, sc, sh)


NORM_ROWS = 128


def _ffn_kernel(h_ref, w1_ref, w2_ref, x1_ref, gt_ref, pw_ref, o_ref):
    f = pl.program_id(1)

    @pl.when(f == 0)
    def _():
        o_ref[...] = jnp.zeros_like(o_ref)

    a = jnp.maximum(jnp.dot(h_ref[...], w1_ref[...], preferred_element_type=F32), 0.0)
    o_ref[...] += jnp.dot((a * a).astype(BF16), w2_ref[...], preferred_element_type=F32)

    @pl.when(f == pl.num_programs(1) - 1)
    def _():
        scale = gt_ref[0] * pw_ref[...]

        def rows_body(r, carry):
            rows = pl.ds(_aligned(r * NORM_ROWS, NORM_ROWS), NORM_ROWS)
            o_ref[rows, :] = x1_ref[rows, :] + _rms(o_ref[rows, :], scale)
            return carry

        lax.fori_loop(0, o_ref.shape[0] // NORM_ROWS, rows_body, 0)


def _ffn(h2, w1, w2, x1, gt, post_w, seq, bm=512, bf=2048):
    m, d = h2.shape
    dff = w1.shape[1]
    per_b = seq // bm
    return pl.pallas_call(
        _ffn_kernel,
        grid=(m // bm, dff // bf),
        in_specs=[pl.BlockSpec((bm, d), lambda i, f: (i, 0)),
                  pl.BlockSpec((d, bf), lambda i, f: (0, f)),
                  pl.BlockSpec((bf, d), lambda i, f: (f, 0)),
                  pl.BlockSpec((bm, d), lambda i, f: (i, 0)),
                  pl.BlockSpec((1, 1, d), lambda i, f: (i // per_b, 0, 0)),
                  pl.BlockSpec((1, d), lambda i, f: (0, 0))],
        out_specs=pl.BlockSpec((bm, d), lambda i, f: (i, 0)),
        out_shape=jax.ShapeDtypeStruct((m, d), F32),
        compiler_params=_cparams(("arbitrary", "arbitrary")),
        name="ffn",
    )(h2, w1, w2, x1, gt, post_w)


def _layer(x, mod, pre_mix_w, post_mix_w, pre_ffn_w, post_ffn_w, w_in, lb_logits, hg_norm_w,
           conv_w, a_log, dt_bias, gdn_norm_w, w_out, w_ff1, w_ff2):
    batch, seq, d = x.shape
    hg_heads = lb_logits.shape[1]
    gdn_heads = a_log.shape[0]
    hg_cols = 4 * hg_heads * HEAD_DIM
    n_main = hg_cols + 4 * gdn_heads * HEAD_DIM

    x2 = x.reshape(batch * seq, d)
    sh_m, sc_m, gt_m, sh_f, sc_f, gt_f = [mod[:batch, None, i * d:(i + 1) * d] for i in range(6)]
    row = lambda v: v.reshape(1, -1)
    pad_lanes = lambda v: jnp.pad(v.reshape(1, -1), ((0, 0), (0, LANES - v.shape[-1])))

    w_in_t = w_in.T
    w_ab_t = jnp.pad(w_in_t[n_main:], ((0, LANES - (w_in_t.shape[0] - n_main)), (0, 0))).astype(BF16)
    h, ab = _prep(x2, row(pre_mix_w), sc_m, sh_m, w_ab_t, seq)
    proj = _inproj(h, w_in_t, n_main)

    o_hg, w_ff2_b, w_out_b = _hgrn(proj, lb_logits.reshape(lb_logits.shape[0], -1), row(hg_norm_w),
                                   [w_ff2, w_out], batch, seq, hg_heads, 0)
    o_gdn, w_ff1_b = _gdn(proj, ab, conv_w, pad_lanes(a_log), pad_lanes(dt_bias), row(gdn_norm_w),
                          [w_ff1], batch, seq, gdn_heads, hg_cols // HEAD_DIM)

    x1, h2 = _outproj(o_hg, o_gdn, w_out_b, x2, gt_m, row(post_mix_w), row(pre_ffn_w), sc_f, sh_f,
                      seq)
    out = _ffn(h2, w_ff1_b, w_ff2_b, x1, gt_f, row(post_ffn_w), seq)
    return out.reshape(batch, seq, d)


def kernel(x, c, w_ada, b_ada, pre_mix_norm, post_mix_norm, pre_ffn_norm, post_ffn_norm, w_in,
           hg_lb_logits, hg_norm, gdn_conv_w, gdn_a_log, gdn_dt_bias, gdn_norm, w_out, w_ff1, w_ff2):
    depth = w_ada.shape[0]
    assert depth == 1 and hg_lb_logits.shape[0] == 2, "single-layer configuration only"
    batch = c.shape[0]
    c_pad = jnp.pad(c, ((0, 8 - batch), (0, 0)))
    for l in range(depth):
        mod = _ada(c_pad, w_ada[l], b_ada[l][None, :])
        x = _layer(x, mod, pre_mix_norm[l], post_mix_norm[l], pre_ffn_norm[l], post_ffn_norm[l],
                   w_in[l], hg_lb_logits, hg_norm[l], gdn_conv_w[l], gdn_a_log[l], gdn_dt_bias[l],
                   gdn_norm[l], w_out[l], w_ff1[l], w_ff2[l])
    return x
```

```python
import functools

import jax
import jax.numpy as jnp
from jax import lax
from jax.experimental import pallas as pl
from jax.experimental.pallas import tpu as pltpu

F32 = jnp.float32
BF16 = jnp.bfloat16
EPS = 1e-6

LANES = 128
HEAD_DIM = 128
CHUNK = 64
SUB = 16
CONV_K = 4
HALO = 8
VMEM_LIMIT = 60 * 1024 * 1024


def _cparams(sem):
    return pltpu.CompilerParams(dimension_semantics=sem, vmem_limit_bytes=VMEM_LIMIT)


def _dot(a, b):
    return jnp.dot(a.astype(BF16), b.astype(BF16), preferred_element_type=F32)


def _dot_nt(a, b):
    return lax.dot_general(a.astype(BF16), b.astype(BF16), (((1,), (1,)), ((), ())),
                           preferred_element_type=F32)


def _split3(x):
    hi = x.astype(BF16)
    r = x - hi.astype(F32)
    mid = r.astype(BF16)
    lo = (r - mid.astype(F32)).astype(BF16)
    return hi, mid, lo


def _dot_exact_lhs(m_bf16, x):
    hi, mid, lo = _split3(x)
    d = lambda p: jnp.dot(m_bf16, p, preferred_element_type=F32)
    return d(hi) + d(mid) + d(lo)


def _dot_exact_rhs(x, m_bf16):
    hi, mid, lo = _split3(x)
    d = lambda p: jnp.dot(p, m_bf16, preferred_element_type=F32)
    return d(hi) + d(mid) + d(lo)


def _sigmoid(x):
    return jax.nn.sigmoid(x)


def _silu(x):
    return x * _sigmoid(x)


def _aligned(x, m):
    return x if isinstance(x, int) else pl.multiple_of(x, m)


def _iota2(shape, dim):
    return lax.broadcasted_iota(jnp.int32, shape, dim)


def _cast_specs(arrays, n_steps, step_index):
    specs, shapes = [], []
    for a in arrays:
        rows, cols = a.shape
        specs.append(pl.BlockSpec((rows // n_steps, cols), lambda *g: (step_index(*g), 0)))
        shapes.append(jax.ShapeDtypeStruct((rows, cols), BF16))
    return specs, shapes


def _cast_slabs(src_refs, dst_refs):
    for src, dst in zip(src_refs, dst_refs):
        dst[...] = src[...].astype(BF16)


def _ada_kernel(c_ref, w_ref, b_ref, o_ref):
    c = c_ref[...]
    o_ref[...] = _dot(_silu(c), w_ref[...]) + b_ref[...]


def _ada(c_pad, w_ada, b_ada, bn=1024):
    rows, d = c_pad.shape
    n = w_ada.shape[1]
    return pl.pallas_call(
        _ada_kernel,
        grid=(n // bn,),
        in_specs=[pl.BlockSpec((rows, d), lambda j: (0, 0)),
                  pl.BlockSpec((d, bn), lambda j: (0, j)),
                  pl.BlockSpec((1, bn), lambda j: (0, j))],
        out_specs=pl.BlockSpec((rows, bn), lambda j: (0, j)),
        out_shape=jax.ShapeDtypeStruct((rows, n), F32),
        compiler_params=_cparams(("arbitrary",)),
        name="ada",
    )(c_pad, w_ada, b_ada)


def _prep_kernel(x_ref, nw_ref, sc_ref, sh_ref, wab_ref, h_ref, ab_ref):
    x = x_ref[...]
    y = x * lax.rsqrt(jnp.mean(x * x, axis=-1, keepdims=True) + EPS)
    hb = ((y * nw_ref[...]) * (1.0 + sc_ref[0]) + sh_ref[0]).astype(BF16)
    h_ref[...] = hb
    ab_ref[...] = _dot_nt(hb, wab_ref[...])


def _prep(x2, nw, sc, sh, w_ab_t, seq, bm=1024):
    m, d = x2.shape
    per_b = seq // bm
    mod = pl.BlockSpec((1, 1, d), lambda i: (i // per_b, 0, 0))
    return pl.pallas_call(
        _prep_kernel,
        grid=(m // bm,),
        in_specs=[pl.BlockSpec((bm, d), lambda i: (i, 0)),
                  pl.BlockSpec((1, d), lambda i: (0, 0)), mod, mod,
                  pl.BlockSpec((LANES, d), lambda i: (0, 0))],
        out_specs=[pl.BlockSpec((bm, d), lambda i: (i, 0)),
                   pl.BlockSpec((bm, LANES), lambda i: (i, 0))],
        out_shape=[jax.ShapeDtypeStruct((m, d), BF16),
                   jax.ShapeDtypeStruct((m, LANES), F32)],
        compiler_params=_cparams(("arbitrary",)),
        name="prep",
    )(x2, nw, sc, sh, w_ab_t)


def _inproj_kernel(h_ref, w_ref, o_ref):
    o_ref[...] = _dot_nt(h_ref[...], w_ref[...])


def _inproj(h, w_t, n, bm=2048, bn=1024):
    m, d = h.shape
    return pl.pallas_call(
        _inproj_kernel,
        grid=(m // bm, n // bn),
        in_specs=[pl.BlockSpec((bm, d), lambda i, j: (i, 0)),
                  pl.BlockSpec((bn, d), lambda i, j: (j, 0))],
        out_specs=pl.BlockSpec((bm, bn), lambda i, j: (i, j)),
        out_shape=jax.ShapeDtypeStruct((m, n), F32),
        compiler_params=_cparams(("arbitrary", "arbitrary")),
        name="inproj",
    )(h, w_t)


def _gated_norm(o, nw, g):
    y = o * lax.rsqrt(jnp.mean(o * o, axis=-1, keepdims=True) + EPS)
    return (y * nw) * _silu(g)


HGRN_ROWS = 8 * CHUNK


def _hgrn_kernel(lbl_ref, q_ref, f_ref, i_ref, g_ref, nw_ref, *rest, heads, tb, n_cast):
    cast_src, (o_ref, *cast_dst) = rest[:n_cast], rest[n_cast:2 * n_cast + 1]
    st_ref, oi_ref, qd_ref, kv_ref, dec_ref = rest[2 * n_cast + 1:]
    _cast_slabs(cast_src, cast_dst)
    n_chunks = tb // CHUNK
    cpi = HGRN_ROWS // CHUNK

    @pl.when(pl.program_id(1) == 0)
    def _():
        st_ref[...] = jnp.zeros_like(st_ref)

    l0 = lbl_ref[0:1, :]
    l1 = lbl_ref[1:2, :]
    mx = jnp.maximum(l0, l1)
    e0 = jnp.exp(l0 - mx)
    lb = e0 / (e0 + jnp.exp(l1 - mx))

    tri = (_iota2((CHUNK, CHUNK), 1) <= _iota2((CHUNK, CHUNK), 0)).astype(BF16)

    def phase1(i, carry):
        rows = pl.ds(_aligned(i * HGRN_ROWS, HGRN_ROWS), HGRN_ROWS)
        f = lb + (1.0 - lb) * _sigmoid(f_ref[rows, :])
        logf = jnp.log(f)
        bc = jnp.concatenate([_dot_exact_lhs(tri, logf[c * CHUNK:(c + 1) * CHUNK]) for c in range(cpi)],
                             axis=0)
        q3 = _problems(q_ref[rows, :], heads, cpi)
        k3 = _problems(1.0 - f, heads, cpi)
        b3 = _problems(bc, heads, cpi)
        v3 = _problems(i_ref[rows, :], heads, cpi).astype(BF16)

        parts = []
        for blk in range(CHUNK // SUB):
            r0 = blk * SUB
            n = r0 + SUB
            p = b3[:, r0 + SUB // 2:r0 + SUB // 2 + 1, :]
            qt = q3[:, r0:n] * jnp.exp(b3[:, r0:n] - p)
            kt = k3[:, 0:n] * jnp.exp(p - b3[:, 0:n])
            a = _bmm_nt(qt, kt)
            keep = _iota2((SUB, n), 1) <= _iota2((SUB, n), 0) + r0
            a = jnp.where(keep, a, 0.0)
            if n < CHUNK:
                a = jnp.concatenate([a, jnp.zeros(a.shape[:2] + (CHUNK - n,), F32)], axis=2)
            parts.append(a)
        b_last = b3[:, CHUNK - 1:CHUNK, :]
        ps = pl.ds(_aligned(i * (cpi * heads), cpi * heads), cpi * heads)
        oi_ref[ps] = _bmm(jnp.concatenate(parts, axis=1), v3)
        qd_ref[ps] = (q3 * jnp.exp(b3)).astype(BF16)
        kv_ref[ps] = _bmm_tn(v3, k3 * jnp.exp(b_last - b3))
        dec_ref[ps] = jnp.exp(b_last)
        return carry

    lax.fori_loop(0, n_chunks // cpi, phase1, 0)

    nw = nw_ref[...]

    def phase2(c, carry):
        ps = pl.ds(_aligned(c * heads, heads), heads)
        rows = pl.ds(_aligned(c * CHUNK, CHUNK), CHUNK)
        st = st_ref[...]
        o = oi_ref[ps] + _bmm_nt(qd_ref[ps], st)
        st_ref[...] = st * dec_ref[ps] + kv_ref[ps]
        for h in range(heads):
            cols = slice(h * HEAD_DIM, (h + 1) * HEAD_DIM)
            o_ref[rows, cols] = _gated_norm(o[h], nw, g_ref[rows, cols]).astype(o_ref.dtype)
        return carry

    lax.fori_loop(0, n_chunks, phase2, 0, unroll=8)


def _hgrn(proj, lb_logits2, norm_w, casts, batch, seq, heads, col0, tb=512):
    per_b = seq // tb
    cast_specs, cast_shapes = _cast_specs(casts, batch * per_b, lambda b, j: b * per_b + j)
    width = heads * HEAD_DIM
    cb0 = col0 // heads

    def col(group):
        return pl.BlockSpec((tb, width), lambda b, j, g=group: (b * per_b + j, cb0 + g))

    n_prob = (tb // CHUNK) * heads
    return pl.pallas_call(
        functools.partial(_hgrn_kernel, heads=heads, tb=tb, n_cast=len(casts)),
        grid=(batch, per_b),
        in_specs=[pl.BlockSpec(lb_logits2.shape, lambda b, j: (0, 0)),
                  col(0), col(1), col(2), col(3),
                  pl.BlockSpec((1, HEAD_DIM), lambda b, j: (0, 0))] + cast_specs,
        out_specs=[pl.BlockSpec((tb, width), lambda b, j: (b * per_b + j, 0))] + cast_specs,
        out_shape=[jax.ShapeDtypeStruct((batch * seq, width), BF16)] + cast_shapes,
        scratch_shapes=[pltpu.VMEM((heads, HEAD_DIM, HEAD_DIM), F32),
                        pltpu.VMEM((n_prob, CHUNK, HEAD_DIM), F32),
                        pltpu.VMEM((n_prob, CHUNK, HEAD_DIM), BF16),
                        pltpu.VMEM((n_prob, HEAD_DIM, HEAD_DIM), F32),
                        pltpu.VMEM((n_prob, 1, HEAD_DIM), F32)],
        compiler_params=_cparams(("arbitrary", "arbitrary")),
        name="hgrn",
    )(lb_logits2, proj, proj, proj, proj, norm_w, *casts)


def _softplus(x):
    return jnp.maximum(x, 0.0) + jnp.log(1.0 + jnp.exp(-jnp.abs(x)))


def _l2norm(x, scale=1.0):
    return x * (lax.rsqrt(jnp.sum(x * x, axis=-1, keepdims=True) + EPS) * scale)


def _bmm(a, b):
    return lax.dot_general(a.astype(BF16), b.astype(BF16), (((2,), (1,)), ((0,), (0,))),
                           preferred_element_type=F32)


def _bmm_nt(a, b):
    return lax.dot_general(a.astype(BF16), b.astype(BF16), (((2,), (2,)), ((0,), (0,))),
                           preferred_element_type=F32)


def _bmm_tn(a, b):
    return lax.dot_general(a.astype(BF16), b.astype(BF16), (((1,), (1,)), ((0,), (0,))),
                           preferred_element_type=F32)


def _conv_silu_tile(x_ref, halo_ref, w, r0, rb):
    cur = x_ref[pl.ds(r0, rb), :]
    prev = x_ref[pl.ds(_aligned(jnp.maximum(r0 - HALO, 0), HALO), HALO), :]
    prev = jnp.where(r0 == 0, halo_ref[...], prev)
    ext = jnp.concatenate([prev, cur], axis=0)
    acc = w[CONV_K - 1:CONV_K, :] * cur
    for j in range(CONV_K - 1):
        s = HALO - (CONV_K - 1) + j
        acc = acc + w[j:j + 1, :] * ext[s:s + rb, :]
    return _silu(acc)


def _problems(x, heads, chunks):
    return jnp.stack([x[c * CHUNK:(c + 1) * CHUNK, h * HEAD_DIM:(h + 1) * HEAD_DIM]
                      for c in range(chunks) for h in range(heads)], axis=0)


GDN_ROWS = 4 * CHUNK


def _gdn_kernel(q_ref, k_ref, v_ref, g_ref, ab_ref, wq_ref, wk_ref, wv_ref, alog_ref, dtb_ref,
                nw_ref, *rest, heads, tb, n_cast):
    cast_src, (o_ref, *cast_dst) = rest[:n_cast], rest[n_cast:2 * n_cast + 1]
    s_ref, hq_ref, hk_ref, hv_ref, u_ref, wqd_ref, qkt_ref, tail_ref = rest[2 * n_cast + 1:]
    _cast_slabs(cast_src, cast_dst)
    n_chunks = tb // CHUNK
    cpi = GDN_ROWS // CHUNK

    @pl.when(pl.program_id(1) == 0)
    def _():
        s_ref[...] = jnp.zeros_like(s_ref)
        hq_ref[...] = jnp.zeros_like(hq_ref)
        hk_ref[...] = jnp.zeros_like(hk_ref)
        hv_ref[...] = jnp.zeros_like(hv_ref)

    width = heads * HEAD_DIM
    row = _iota2((CHUNK, CHUNK), 0)
    colm = _iota2((CHUNK, CHUNK), 1)
    tri = (colm <= row).astype(BF16)
    incl = colm <= row
    eye = (row == colm).astype(F32)
    lower_left = []
    for k in range(CHUNK.bit_length() - 1):
        same_pair = (row >> (k + 1)) == (colm >> (k + 1))
        quarter = (((row >> k) & 1) == 1) & (((colm >> k) & 1) == 0)
        lower_left.append(jnp.where(same_pair & quarter, 1.0, 0.0))
    src = _iota2((LANES, width), 0)
    dst_head = _iota2((LANES, width), 1) >> (HEAD_DIM.bit_length() - 1)
    sel_a = (src == dst_head).astype(BF16)
    sel_b = (src == dst_head + heads).astype(BF16)

    def phase1(i, carry):
        r0 = _aligned(i * GDN_ROWS, GDN_ROWS)
        qa = _conv_silu_tile(q_ref, hq_ref, wq_ref[...], r0, GDN_ROWS)
        ka = _conv_silu_tile(k_ref, hk_ref, wk_ref[...], r0, GDN_ROWS)
        va = _conv_silu_tile(v_ref, hv_ref, wv_ref[...], r0, GDN_ROWS)
        qn = jnp.concatenate([_l2norm(qa[:, h * HEAD_DIM:(h + 1) * HEAD_DIM], HEAD_DIM ** -0.5)
                              for h in range(heads)], axis=1)
        kn = jnp.concatenate([_l2norm(ka[:, h * HEAD_DIM:(h + 1) * HEAD_DIM]) for h in range(heads)],
                             axis=1)

        ab = ab_ref[pl.ds(r0, GDN_ROWS), :]
        la = -jnp.exp(alog_ref[...]) * _softplus(ab + dtb_ref[...])
        be = _sigmoid(ab)
        gc = jnp.concatenate([_dot_exact_lhs(tri, la[c * CHUNK:(c + 1) * CHUNK]) for c in range(cpi)],
                             axis=0)
        g_rep = _dot_exact_rhs(gc, sel_a)
        be_rep = _dot_exact_rhs(be, sel_b)

        q3 = _problems(qn, heads, cpi)
        k3 = _problems(kn, heads, cpi)
        v3 = _problems(va, heads, cpi)
        g3 = _problems(g_rep, heads, cpi)
        b3 = _problems(be_rep, heads, cpi)

        g_sq = g3[:, :, :CHUNK]
        g_row = jnp.sum(g_sq * eye, axis=1, keepdims=True)
        gamma = jnp.exp(jnp.where(incl, g_sq - g_row, -jnp.inf))
        kb = k3.astype(BF16)
        kq = _bmm_nt(jnp.concatenate([kb, q3.astype(BF16)], axis=1), kb)
        mm = kq[:, :CHUNK] * (b3[:, :, :CHUNK] * gamma)
        tinv = eye - mm * lower_left[0]
        for c_mask in lower_left[1:]:
            tinv = tinv - _bmm(_bmm(tinv, mm * c_mask), tinv)

        eg = jnp.exp(g3)
        uw = _bmm(tinv, jnp.concatenate([v3 * b3, k3 * (b3 * eg)], axis=2))
        g_last = g3[:, CHUNK - 1:CHUNK, :]
        ps = pl.ds(_aligned(i * (cpi * heads), cpi * heads), cpi * heads)
        u_ref[ps] = uw[:, :, :HEAD_DIM]
        wqd_ref[ps] = jnp.concatenate([uw[:, :, HEAD_DIM:], q3 * eg], axis=1).astype(BF16)
        kt_t = jnp.swapaxes(k3 * jnp.exp(g_last - g3), 1, 2)
        qkt_ref[ps] = jnp.concatenate([kq[:, CHUNK:] * gamma, kt_t], axis=1).astype(BF16)
        tail_ref[ps] = jnp.exp(g_last)
        return carry

    lax.fori_loop(0, n_chunks // cpi, phase1, 0)
    hq_ref[...] = q_ref[tb - HALO:tb, :]
    hk_ref[...] = k_ref[tb - HALO:tb, :]
    hv_ref[...] = v_ref[tb - HALO:tb, :]

    nw = nw_ref[...]

    def phase2(c, carry):
        ps = pl.ds(_aligned(c * heads, heads), heads)
        rows = pl.ds(_aligned(c * CHUNK, CHUNK), CHUNK)
        s = s_ref[...]
        sb = s.astype(BF16)
        ws_qs = _bmm(wqd_ref[ps], sb)
        v_new = u_ref[ps] - ws_qs[:, :CHUNK]
        vb = v_new.astype(BF16)
        ov_ds = _bmm(qkt_ref[ps], vb)
        o = ws_qs[:, CHUNK:] + ov_ds[:, :CHUNK]
        s_ref[...] = s * tail_ref[ps] + ov_ds[:, CHUNK:]
        for h in range(heads):
            cols = slice(h * HEAD_DIM, (h + 1) * HEAD_DIM)
            o_ref[rows, cols] = _gated_norm(o[h], nw, g_ref[rows, cols]).astype(o_ref.dtype)
        return carry

    lax.fori_loop(0, n_chunks, phase2, 0, unroll=8)


def _gdn(proj, ab, conv_w, a_log_pad, dt_bias_pad, norm_w, casts, batch, seq, heads, col0, tb=512):
    per_b = seq // tb
    cast_specs, cast_shapes = _cast_specs(casts, batch * per_b, lambda b, j: b * per_b + j)
    width = heads * HEAD_DIM
    cb0 = col0 // heads

    def col(group):
        return pl.BlockSpec((tb, width), lambda b, j, g=group: (b * per_b + j, cb0 + g))

    def wcol(group):
        return pl.BlockSpec((CONV_K, width), lambda b, j, g=group: (0, g))

    small = pl.BlockSpec((1, LANES), lambda b, j: (0, 0))
    n_prob = (tb // CHUNK) * heads
    halo = pltpu.VMEM((HALO, width), F32)
    prob = lambda n, dt: pltpu.VMEM((n_prob, CHUNK, n), dt)
    return pl.pallas_call(
        functools.partial(_gdn_kernel, heads=heads, tb=tb, n_cast=len(casts)),
        grid=(batch, per_b),
        in_specs=[col(0), col(1), col(2), col(3),
                  pl.BlockSpec((tb, LANES), lambda b, j: (b * per_b + j, 0)),
                  wcol(0), wcol(1), wcol(2), small, small, small] + cast_specs,
        out_specs=[pl.BlockSpec((tb, width), lambda b, j: (b * per_b + j, 0))] + cast_specs,
        out_shape=[jax.ShapeDtypeStruct((batch * seq, width), BF16)] + cast_shapes,
        scratch_shapes=[pltpu.VMEM((heads, HEAD_DIM, HEAD_DIM), F32), halo, halo, halo,
                        prob(HEAD_DIM, F32), pltpu.VMEM((n_prob, 2 * CHUNK, HEAD_DIM), BF16),
                        pltpu.VMEM((n_prob, CHUNK + HEAD_DIM, CHUNK), BF16),
                        pltpu.VMEM((n_prob, 1, HEAD_DIM), F32)],
        compiler_params=_cparams(("arbitrary", "arbitrary")),
        name="gdn",
    )(proj, proj, proj, proj, ab, conv_w, conv_w, conv_w, a_log_pad, dt_bias_pad, norm_w, *casts)


def _rms(y, w):
    return (y * lax.rsqrt(jnp.mean(y * y, axis=-1, keepdims=True) + EPS)) * w


OUT_ROWS = 256


def _outproj_kernel(oh_ref, og_ref, w_ref, x_ref, gt_ref, pw_ref, fw_ref, sc_ref, sh_ref,
                    x1_ref, h2_ref):
    post_scale = gt_ref[0] * pw_ref[...]
    pre_scale = fw_ref[...] * (1.0 + sc_ref[0])
    for r in range(oh_ref.shape[0] // OUT_ROWS):
        rows = pl.ds(r * OUT_ROWS, OUT_ROWS)
        o = jnp.concatenate([oh_ref[rows, :], og_ref[rows, :]], axis=1)
        y = jnp.dot(o, w_ref[...], preferred_element_type=F32)
        x1 = x_ref[rows, :] + _rms(y, post_scale)
        x1_ref[rows, :] = x1
        h2_ref[rows, :] = (_rms(x1, pre_scale) + sh_ref[0]).astype(BF16)


def _outproj(o_hg, o_gdn, w_out, x2, gt, post_w, ffn_w, sc, sh, seq, bm=512):
    m, d = x2.shape
    kh = o_hg.shape[1]
    per_b = seq // bm
    vec = pl.BlockSpec((1, d), lambda i: (0, 0))
    mod = pl.BlockSpec((1, 1, d), lambda i: (i // per_b, 0, 0))
    return pl.pallas_call(
        _outproj_kernel,
        grid=(m // bm,),
        in_specs=[pl.BlockSpec((bm, kh), lambda i: (i, 0)),
                  pl.BlockSpec((bm, kh), lambda i: (i, 0)),
                  pl.BlockSpec((2 * kh, d), lambda i: (0, 0)),
                  pl.BlockSpec((bm, d), lambda i: (i, 0)),
                  mod, vec, vec, mod, mod],
        out_specs=[pl.BlockSpec((bm, d), lambda i: (i, 0)),
                   pl.BlockSpec((bm, d), lambda i: (i, 0))],
        out_shape=[jax.ShapeDtypeStruct((m, d), F32),
                   jax.ShapeDtypeStruct((m, d), BF16)],
        compiler_params=_cparams(("arbitrary",)),
        name="outproj",
    )(o_hg, o_gdn, w_out, x2, gt, post_w, ffn_w, sc, sh)


NORM_ROWS = 128


def _ffn_kernel(h_ref, w1_ref, w2_ref, x1_ref, gt_ref, pw_ref, o_ref):
    f = pl.program_id(1)

    @pl.when(f == 0)
    def _():
        o_ref[...] = jnp.zeros_like(o_ref)

    a = jnp.maximum(jnp.dot(h_ref[...], w1_ref[...], preferred_element_type=F32), 0.0)
    o_ref[...] += jnp.dot((a * a).astype(BF16), w2_ref[...], preferred_element_type=F32)

    @pl.when(f == pl.num_programs(1) - 1)
    def _():
        scale = gt_ref[0] * pw_ref[...]

        def rows_body(r, carry):
            rows = pl.ds(_aligned(r * NORM_ROWS, NORM_ROWS), NORM_ROWS)
            o_ref[rows, :] = x1_ref[rows, :] + _rms(o_ref[rows, :], scale)
            return carry

        lax.fori_loop(0, o_ref.shape[0] // NORM_ROWS, rows_body, 0)


def _ffn(h2, w1, w2, x1, gt, post_w, seq, bm=512, bf=2048):
    m, d = h2.shape
    dff = w1.shape[1]
    per_b = seq // bm
    return pl.pallas_call(
        _ffn_kernel,
        grid=(m // bm, dff // bf),
        in_specs=[pl.BlockSpec((bm, d), lambda i, f: (i, 0)),
                  pl.BlockSpec((d, bf), lambda i, f: (0, f)),
                  pl.BlockSpec((bf, d), lambda i, f: (f, 0)),
                  pl.BlockSpec((bm, d), lambda i, f: (i, 0)),
                  pl.BlockSpec((1, 1, d), lambda i, f: (i // per_b, 0, 0)),
                  pl.BlockSpec((1, d), lambda i, f: (0, 0))],
        out_specs=pl.BlockSpec((bm, d), lambda i, f: (i, 0)),
        out_shape=jax.ShapeDtypeStruct((m, d), F32),
        compiler_params=_cparams(("arbitrary", "arbitrary")),
        name="ffn",
    )(h2, w1, w2, x1, gt, post_w)


def _layer(x, mod, pre_mix_w, post_mix_w, pre_ffn_w, post_ffn_w, w_in, lb_logits, hg_norm_w,
           conv_w, a_log, dt_bias, gdn_norm_w, w_out, w_ff1, w_ff2):
    batch, seq, d = x.shape
    hg_heads = lb_logits.shape[1]
    gdn_heads = a_log.shape[0]
    hg_cols = 4 * hg_heads * HEAD_DIM
    n_main = hg_cols + 4 * gdn_heads * HEAD_DIM

    x2 = x.reshape(batch * seq, d)
    sh_m, sc_m, gt_m, sh_f, sc_f, gt_f = [mod[:batch, None, i * d:(i + 1) * d] for i in range(6)]
    row = lambda v: v.reshape(1, -1)
    pad_lanes = lambda v: jnp.pad(v.reshape(1, -1), ((0, 0), (0, LANES - v.shape[-1])))

    w_in_t = w_in.T
    w_ab_t = jnp.pad(w_in_t[n_main:], ((0, LANES - (w_in_t.shape[0] - n_main)), (0, 0))).astype(BF16)
    h, ab = _prep(x2, row(pre_mix_w), sc_m, sh_m, w_ab_t, seq)
    proj = _inproj(h, w_in_t, n_main)

    o_hg, w_ff2_b, w_out_b = _hgrn(proj, lb_logits.reshape(lb_logits.shape[0], -1), row(hg_norm_w),
                                   [w_ff2, w_out], batch, seq, hg_heads, 0)
    o_gdn, w_ff1_b = _gdn(proj, ab, conv_w, pad_lanes(a_log), pad_lanes(dt_bias), row(gdn_norm_w),
                          [w_ff1], batch, seq, gdn_heads, hg_cols // HEAD_DIM)

    x1, h2 = _outproj(o_hg, o_gdn, w_out_b, x2, gt_m, row(post_mix_w), row(pre_ffn_w), sc_f, sh_f,
                      seq)
    out = _ffn(h2, w_ff1_b, w_ff2_b, x1, gt_f, row(post_ffn_w), seq)
    return out.reshape(batch, seq, d)


def kernel(x, c, w_ada, b_ada, pre_mix_norm, post_mix_norm, pre_ffn_norm, post_ffn_norm, w_in,
           hg_lb_logits, hg_norm, gdn_conv_w, gdn_a_log, gdn_dt_bias, gdn_norm, w_out, w_ff1, w_ff2):
    depth = w_ada.shape[0]
    assert depth == 1 and hg_lb_logits.shape[0] == 2, "single-layer configuration only"
    batch = c.shape[0]
    c_pad = jnp.pad(c, ((0, 8 - batch), (0, 0)))
    for l in range(depth):
        mod = _ada(c_pad, w_ada[l], b_ada[l][None, :])
        x = _layer(x, mod, pre_mix_norm[l], post_mix_norm[l], pre_ffn_norm[l], post_ffn_norm[l],
                   w_in[l], hg_lb_logits, hg_norm[l], gdn_conv_w[l], gdn_a_log[l], gdn_dt_bias[l],
                   gdn_norm[l], w_out[l], w_ff1[l], w_ff2[l])
    return x
```

```python
import functools

import jax
import jax.numpy as jnp
from jax import lax
from jax.experimental import pallas as pl
from jax.experimental.pallas import tpu as pltpu

F32 = jnp.float32
BF16 = jnp.bfloat16
EPS = 1e-6

LANES = 128
HEAD_DIM = 128
CHUNK = 64
SUB = 16
CONV_K = 4
HALO = 8
VMEM_LIMIT = 60 * 1024 * 1024


def _cparams(sem):
    return pltpu.CompilerParams(dimension_semantics=sem, vmem_limit_bytes=VMEM_LIMIT)


def _dot(a, b):
    return jnp.dot(a.astype(BF16), b.astype(BF16), preferred_element_type=F32)


def _dot_nt(a, b):
    return lax.dot_general(a.astype(BF16), b.astype(BF16), (((1,), (1,)), ((), ())),
                           preferred_element_type=F32)


def _split3(x):
    hi = x.astype(BF16)
    r = x - hi.astype(F32)
    mid = r.astype(BF16)
    lo = (r - mid.astype(F32)).astype(BF16)
    return hi, mid, lo


def _dot_exact_lhs(m_bf16, x):
    hi, mid, lo = _split3(x)
    d = lambda p: jnp.dot(m_bf16, p, preferred_element_type=F32)
    return d(hi) + d(mid) + d(lo)


def _dot_exact_rhs(x, m_bf16):
    hi, mid, lo = _split3(x)
    d = lambda p: jnp.dot(p, m_bf16, preferred_element_type=F32)
    return d(hi) + d(mid) + d(lo)


def _sigmoid(x):
    return jax.nn.sigmoid(x)


def _silu(x):
    return x * _sigmoid(x)


def _aligned(x, m):
    return x if isinstance(x, int) else pl.multiple_of(x, m)


def _iota2(shape, dim):
    return lax.broadcasted_iota(jnp.int32, shape, dim)


def _cast_specs(arrays, n_steps, step_index):
    specs, shapes = [], []
    for a in arrays:
        rows, cols = a.shape
        specs.append(pl.BlockSpec((rows // n_steps, cols), lambda *g: (step_index(*g), 0)))
        shapes.append(jax.ShapeDtypeStruct((rows, cols), BF16))
    return specs, shapes


def _cast_slabs(src_refs, dst_refs):
    for src, dst in zip(src_refs, dst_refs):
        dst[...] = src[...].astype(BF16)


def _ada_kernel(c_ref, w_ref, b_ref, o_ref):
    c = c_ref[...]
    o_ref[...] = _dot(_silu(c), w_ref[...]) + b_ref[...]


def _ada(c_pad, w_ada, b_ada, bn=1024):
    rows, d = c_pad.shape
    n = w_ada.shape[1]
    return pl.pallas_call(
        _ada_kernel,
        grid=(n // bn,),
        in_specs=[pl.BlockSpec((rows, d), lambda j: (0, 0)),
                  pl.BlockSpec((d, bn), lambda j: (0, j)),
                  pl.BlockSpec((1, bn), lambda j: (0, j))],
        out_specs=pl.BlockSpec((rows, bn), lambda j: (0, j)),
        out_shape=jax.ShapeDtypeStruct((rows, n), F32),
        compiler_params=_cparams(("arbitrary",)),
        name="ada",
    )(c_pad, w_ada, b_ada)


def _prep_kernel(x_ref, nw_ref, sc_ref, sh_ref, wab_ref, h_ref, ab_ref):
    x = x_ref[...]
    y = x * lax.rsqrt(jnp.mean(x * x, axis=-1, keepdims=True) + EPS)
    hb = ((y * nw_ref[...]) * (1.0 + sc_ref[0]) + sh_ref[0]).astype(BF16)
    h_ref[...] = hb
    ab_ref[...] = _dot_nt(hb, wab_ref[...])


def _prep(x2, nw, sc, sh, w_ab_t, seq, bm=1024):
    m, d = x2.shape
    per_b = seq // bm
    mod = pl.BlockSpec((1, 1, d), lambda i: (i // per_b, 0, 0))
    return pl.pallas_call(
        _prep_kernel,
        grid=(m // bm,),
        in_specs=[pl.BlockSpec((bm, d), lambda i: (i, 0)),
                  pl.BlockSpec((1, d), lambda i: (0, 0)), mod, mod,
                  pl.BlockSpec((LANES, d), lambda i: (0, 0))],
        out_specs=[pl.BlockSpec((bm, d), lambda i: (i, 0)),
                   pl.BlockSpec((bm, LANES), lambda i: (i, 0))],
        out_shape=[jax.ShapeDtypeStruct((m, d), BF16),
                   jax.ShapeDtypeStruct((m, LANES), F32)],
        compiler_params=_cparams(("arbitrary",)),
        name="prep",
    )(x2, nw, sc, sh, w_ab_t)


def _inproj_kernel(h_ref, w_ref, o_ref):
    o_ref[...] = _dot_nt(h_ref[...], w_ref[...])


def _inproj(h, w_t, n, bm=2048, bn=1024):
    m, d = h.shape
    return pl.pallas_call(
        _inproj_kernel,
        grid=(m // bm, n // bn),
        in_specs=[pl.BlockSpec((bm, d), lambda i, j: (i, 0)),
                  pl.BlockSpec((bn, d), lambda i, j: (j, 0))],
        out_specs=pl.BlockSpec((bm, bn), lambda i, j: (i, j)),
        out_shape=jax.ShapeDtypeStruct((m, n), F32),
        compiler_params=_cparams(("arbitrary", "arbitrary")),
        name="inproj",
    )(h, w_t)


def _gated_norm(o, nw, g):
    y = o * lax.rsqrt(jnp.mean(o * o, axis=-1, keepdims=True) + EPS)
    return (y * nw) * _silu(g)


HGRN_ROWS = 8 * CHUNK


def _hgrn_kernel(lbl_ref, q_ref, f_ref, i_ref, g_ref, nw_ref, *rest, heads, tb, n_cast):
    cast_src, (o_ref, *cast_dst) = rest[:n_cast], rest[n_cast:2 * n_cast + 1]
    st_ref, oi_ref, qd_ref, kv_ref, dec_ref = rest[2 * n_cast + 1:]
    _cast_slabs(cast_src, cast_dst)
    n_chunks = tb // CHUNK
    cpi = HGRN_ROWS // CHUNK

    @pl.when(pl.program_id(1) == 0)
    def _():
        st_ref[...] = jnp.zeros_like(st_ref)

    l0 = lbl_ref[0:1, :]
    l1 = lbl_ref[1:2, :]
    mx = jnp.maximum(l0, l1)
    e0 = jnp.exp(l0 - mx)
    lb = e0 / (e0 + jnp.exp(l1 - mx))

    tri = (_iota2((CHUNK, CHUNK), 1) <= _iota2((CHUNK, CHUNK), 0)).astype(BF16)

    def phase1(i, carry):
        rows = pl.ds(_aligned(i * HGRN_ROWS, HGRN_ROWS), HGRN_ROWS)
        f = lb + (1.0 - lb) * _sigmoid(f_ref[rows, :])
        logf = jnp.log(f)
        bc = jnp.concatenate([_dot_exact_lhs(tri, logf[c * CHUNK:(c + 1) * CHUNK]) for c in range(cpi)],
                             axis=0)
        q3 = _problems(q_ref[rows, :], heads, cpi)
        k3 = _problems(1.0 - f, heads, cpi)
        b3 = _problems(bc, heads, cpi)
        v3 = _problems(i_ref[rows, :], heads, cpi).astype(BF16)

        parts = []
        for blk in range(CHUNK // SUB):
            r0 = blk * SUB
            n = r0 + SUB
            p = b3[:, r0 + SUB // 2:r0 + SUB // 2 + 1, :]
            qt = q3[:, r0:n] * jnp.exp(b3[:, r0:n] - p)
            kt = k3[:, 0:n] * jnp.exp(p - b3[:, 0:n])
            a = _bmm_nt(qt, kt)
            keep = _iota2((SUB, n), 1) <= _iota2((SUB, n), 0) + r0
            a = jnp.where(keep, a, 0.0)
            if n < CHUNK:
                a = jnp.concatenate([a, jnp.zeros(a.shape[:2] + (CHUNK - n,), F32)], axis=2)
            parts.append(a)
        b_last = b3[:, CHUNK - 1:CHUNK, :]
        ps = pl.ds(_aligned(i * (cpi * heads), cpi * heads), cpi * heads)
        oi_ref[ps] = _bmm(jnp.concatenate(parts, axis=1), v3)
        qd_ref[ps] = (q3 * jnp.exp(b3)).astype(BF16)
        kv_ref[ps] = _bmm_tn(v3, k3 * jnp.exp(b_last - b3))
        dec_ref[ps] = jnp.exp(b_last)
        return carry

    lax.fori_loop(0, n_chunks // cpi, phase1, 0)

    nw = nw_ref[...]

    def phase2(c, carry):
        ps = pl.ds(_aligned(c * heads, heads), heads)
        rows = pl.ds(_aligned(c * CHUNK, CHUNK), CHUNK)
        st = st_ref[...]
        o = oi_ref[ps] + _bmm_nt(qd_ref[ps], st)
        st_ref[...] = st * dec_ref[ps] + kv_ref[ps]
        for h in range(heads):
            cols = slice(h * HEAD_DIM, (h + 1) * HEAD_DIM)
            o_ref[rows, cols] = _gated_norm(o[h], nw, g_ref[rows, cols]).astype(o_ref.dtype)
        return carry

    lax.fori_loop(0, n_chunks, phase2, 0, unroll=8)


def _hgrn(proj, lb_logits2, norm_w, casts, batch, seq, heads, col0, tb=512):
    per_b = seq // tb
    cast_specs, cast_shapes = _cast_specs(casts, batch * per_b, lambda b, j: b * per_b + j)
    width = heads * HEAD_DIM
    cb0 = col0 // heads

    def col(group):
        return pl.BlockSpec((tb, width), lambda b, j, g=group: (b * per_b + j, cb0 + g))

    n_prob = (tb // CHUNK) * heads
    return pl.pallas_call(
        functools.partial(_hgrn_kernel, heads=heads, tb=tb, n_cast=len(casts)),
        grid=(batch, per_b),
        in_specs=[pl.BlockSpec(lb_logits2.shape, lambda b, j: (0, 0)),
                  col(0), col(1), col(2), col(3),
                  pl.BlockSpec((1, HEAD_DIM), lambda b, j: (0, 0))] + cast_specs,
        out_specs=[pl.BlockSpec((tb, width), lambda b, j: (b * per_b + j, 0))] + cast_specs,
        out_shape=[jax.ShapeDtypeStruct((batch * seq, width), BF16)] + cast_shapes,
        scratch_shapes=[pltpu.VMEM((heads, HEAD_DIM, HEAD_DIM), F32),
                        pltpu.VMEM((n_prob, CHUNK, HEAD_DIM), F32),
                        pltpu.VMEM((n_prob, CHUNK, HEAD_DIM), BF16),
                        pltpu.VMEM((n_prob, HEAD_DIM, HEAD_DIM), F32),
                        pltpu.VMEM((n_prob, 1, HEAD_DIM), F32)],
        compiler_params=_cparams(("arbitrary", "arbitrary")),
        name="hgrn",
    )(lb_logits2, proj, proj, proj, proj, norm_w, *casts)


def _softplus(x):
    return jnp.maximum(x, 0.0) + jnp.log(1.0 + jnp.exp(-jnp.abs(x)))


def _l2norm(x, scale=1.0):
    return x * (lax.rsqrt(jnp.sum(x * x, axis=-1, keepdims=True) + EPS) * scale)


def _bmm(a, b):
    return lax.dot_general(a.astype(BF16), b.astype(BF16), (((2,), (1,)), ((0,), (0,))),
                           preferred_element_type=F32)


def _bmm_nt(a, b):
    return lax.dot_general(a.astype(BF16), b.astype(BF16), (((2,), (2,)), ((0,), (0,))),
                           preferred_element_type=F32)


def _bmm_tn(a, b):
    return lax.dot_general(a.astype(BF16), b.astype(BF16), (((1,), (1,)), ((0,), (0,))),
                           preferred_element_type=F32)


def _conv_silu_tile(x_ref, halo_ref, w_ref, r0, rb, cols):
    w = w_ref[:, cols]
    cur = x_ref[pl.ds(r0, rb), cols]
    prev = x_ref[pl.ds(_aligned(jnp.maximum(r0 - HALO, 0), HALO), HALO), cols]
    prev = jnp.where(r0 == 0, halo_ref[:, cols], prev)
    ext = jnp.concatenate([prev, cur], axis=0)
    acc = w[CONV_K - 1:CONV_K, :] * cur
    for j in range(CONV_K - 1):
        s = HALO - (CONV_K - 1) + j
        acc = acc + w[j:j + 1, :] * ext[s:s + rb, :]
    return _silu(acc)


def _problems(x, heads, chunks):
    return jnp.stack([x[c * CHUNK:(c + 1) * CHUNK, h * HEAD_DIM:(h + 1) * HEAD_DIM]
                      for c in range(chunks) for h in range(heads)], axis=0)


GDN_ROWS = 4 * CHUNK


def _gdn_kernel(q_ref, k_ref, v_ref, g_ref, ab_ref, wq_ref, wk_ref, wv_ref, alog_ref, dtb_ref,
                nw_ref, *rest, heads, tb, n_cast):
    cast_src, (o_ref, *cast_dst) = rest[:n_cast], rest[n_cast:2 * n_cast + 1]
    s_ref, hq_ref, hk_ref, hv_ref, u_ref, wqd_ref, qkt_ref, tail_ref = rest[2 * n_cast + 1:]
    _cast_slabs(cast_src, cast_dst)
    n_chunks = tb // CHUNK
    cpi = GDN_ROWS // CHUNK

    @pl.when(pl.program_id(1) == 0)
    def _():
        s_ref[...] = jnp.zeros_like(s_ref)
        hq_ref[...] = jnp.zeros_like(hq_ref)
        hk_ref[...] = jnp.zeros_like(hk_ref)
        hv_ref[...] = jnp.zeros_like(hv_ref)

    width = heads * HEAD_DIM
    row = _iota2((CHUNK, CHUNK), 0)
    colm = _iota2((CHUNK, CHUNK), 1)
    tri = (colm <= row).astype(BF16)
    incl = colm <= row
    eye = (row == colm).astype(F32)
    lower_left = []
    for k in range(CHUNK.bit_length() - 1):
        same_pair = (row >> (k + 1)) == (colm >> (k + 1))
        quarter = (((row >> k) & 1) == 1) & (((colm >> k) & 1) == 0)
        lower_left.append(jnp.where(same_pair & quarter, 1.0, 0.0))
    src = _iota2((LANES, width), 0)
    dst_head = _iota2((LANES, width), 1) >> (HEAD_DIM.bit_length() - 1)
    sel_a = (src == dst_head).astype(BF16)
    sel_b = (src == dst_head + heads).astype(BF16)

    def phase1(i, carry):
        r0 = _aligned(i * GDN_ROWS, GDN_ROWS)
        head = lambda h: slice(h * HEAD_DIM, (h + 1) * HEAD_DIM)
        qn = jnp.concatenate([_l2norm(_conv_silu_tile(q_ref, hq_ref, wq_ref, r0, GDN_ROWS, head(h)),
                                      HEAD_DIM ** -0.5) for h in range(heads)], axis=1)
        kn = jnp.concatenate([_l2norm(_conv_silu_tile(k_ref, hk_ref, wk_ref, r0, GDN_ROWS, head(h)))
                              for h in range(heads)], axis=1)
        va = jnp.concatenate([_conv_silu_tile(v_ref, hv_ref, wv_ref, r0, GDN_ROWS, head(h))
                              for h in range(heads)], axis=1)

        ab = ab_ref[pl.ds(r0, GDN_ROWS), :]
        la = -jnp.exp(alog_ref[...]) * _softplus(ab + dtb_ref[...])
        be = _sigmoid(ab)
        gc = jnp.concatenate([_dot_exact_lhs(tri, la[c * CHUNK:(c + 1) * CHUNK]) for c in range(cpi)],
                             axis=0)
        g_rep = _dot_exact_rhs(gc, sel_a)
        be_rep = _dot_exact_rhs(be, sel_b)

        q3 = _problems(qn, heads, cpi)
        k3 = _problems(kn, heads, cpi)
        v3 = _problems(va, heads, cpi)
        g3 = _problems(g_rep, heads, cpi)
        b3 = _problems(be_rep, heads, cpi)

        g_sq = g3[:, :, :CHUNK]
        g_row = jnp.sum(g_sq * eye, axis=1, keepdims=True)
        gamma = jnp.exp(jnp.where(incl, g_sq - g_row, -jnp.inf))
        kb = k3.astype(BF16)
        kq = _bmm_nt(jnp.concatenate([kb, q3.astype(BF16)], axis=1), kb)
        mm = kq[:, :CHUNK] * (b3[:, :, :CHUNK] * gamma)
        tinv = eye - mm * lower_left[0]
        for c_mask in lower_left[1:]:
            tinv = tinv - _bmm(_bmm(tinv, mm * c_mask), tinv)

        eg = jnp.exp(g3)
        uw = _bmm(tinv, jnp.concatenate([v3 * b3, k3 * (b3 * eg)], axis=2))
        g_last = g3[:, CHUNK - 1:CHUNK, :]
        ps = pl.ds(_aligned(i * (cpi * heads), cpi * heads), cpi * heads)
        u_ref[ps] = uw[:, :, :HEAD_DIM]
        wqd_ref[ps] = jnp.concatenate([uw[:, :, HEAD_DIM:], q3 * eg], axis=1).astype(BF16)
        kt_t = jnp.swapaxes(k3 * jnp.exp(g_last - g3), 1, 2)
        qkt_ref[ps] = jnp.concatenate([kq[:, CHUNK:] * gamma, kt_t], axis=1).astype(BF16)
        tail_ref[ps] = jnp.exp(g_last)
        return carry

    lax.fori_loop(0, n_chunks // cpi, phase1, 0)
    hq_ref[...] = q_ref[tb - HALO:tb, :]
    hk_ref[...] = k_ref[tb - HALO:tb, :]
    hv_ref[...] = v_ref[tb - HALO:tb, :]

    nw = nw_ref[...]

    def phase2(c, carry):
        ps = pl.ds(_aligned(c * heads, heads), heads)
        rows = pl.ds(_aligned(c * CHUNK, CHUNK), CHUNK)
        s = s_ref[...]
        sb = s.astype(BF16)
        ws_qs = _bmm(wqd_ref[ps], sb)
        v_new = u_ref[ps] - ws_qs[:, :CHUNK]
        vb = v_new.astype(BF16)
        ov_ds = _bmm(qkt_ref[ps], vb)
        o = ws_qs[:, CHUNK:] + ov_ds[:, :CHUNK]
        s_ref[...] = s * tail_ref[ps] + ov_ds[:, CHUNK:]
        for h in range(heads):
            cols = slice(h * HEAD_DIM, (h + 1) * HEAD_DIM)
            o_ref[rows, cols] = _gated_norm(o[h], nw, g_ref[rows, cols]).astype(o_ref.dtype)
        return carry

    lax.fori_loop(0, n_chunks, phase2, 0, unroll=8)


def _gdn(proj, ab, conv_w, a_log_pad, dt_bias_pad, norm_w, casts, batch, seq, heads, col0, tb=512):
    per_b = seq // tb
    cast_specs, cast_shapes = _cast_specs(casts, batch * per_b, lambda b, j: b * per_b + j)
    width = heads * HEAD_DIM
    cb0 = col0 // heads

    def col(group):
        return pl.BlockSpec((tb, width), lambda b, j, g=group: (b * per_b + j, cb0 + g))

    def wcol(group):
        return pl.BlockSpec((CONV_K, width), lambda b, j, g=group: (0, g))

    small = pl.BlockSpec((1, LANES), lambda b, j: (0, 0))
    n_prob = (tb // CHUNK) * heads
    halo = pltpu.VMEM((HALO, width), F32)
    prob = lambda n, dt: pltpu.VMEM((n_prob, CHUNK, n), dt)
    return pl.pallas_call(
        functools.partial(_gdn_kernel, heads=heads, tb=tb, n_cast=len(casts)),
        grid=(batch, per_b),
        in_specs=[col(0), col(1), col(2), col(3),
                  pl.BlockSpec((tb, LANES), lambda b, j: (b * per_b + j, 0)),
                  wcol(0), wcol(1), wcol(2), small, small, small] + cast_specs,
        out_specs=[pl.BlockSpec((tb, width), lambda b, j: (b * per_b + j, 0))] + cast_specs,
        out_shape=[jax.ShapeDtypeStruct((batch * seq, width), BF16)] + cast_shapes,
        scratch_shapes=[pltpu.VMEM((heads, HEAD_DIM, HEAD_DIM), F32), halo, halo, halo,
                        prob(HEAD_DIM, F32), pltpu.VMEM((n_prob, 2 * CHUNK, HEAD_DIM), BF16),
                        pltpu.VMEM((n_prob, CHUNK + HEAD_DIM, CHUNK), BF16),
                        pltpu.VMEM((n_prob, 1, HEAD_DIM), F32)],
        compiler_params=_cparams(("arbitrary", "arbitrary")),
        name="gdn",
    )(proj, proj, proj, proj, ab, conv_w, conv_w, conv_w, a_log_pad, dt_bias_pad, norm_w, *casts)


def _rms(y, w):
    return (y * lax.rsqrt(jnp.mean(y * y, axis=-1, keepdims=True) + EPS)) * w


OUT_ROWS = 256


def _outproj_kernel(oh_ref, og_ref, w_ref, x_ref, gt_ref, pw_ref, fw_ref, sc_ref, sh_ref,
                    x1_ref, h2_ref):
    post_scale = gt_ref[0] * pw_ref[...]
    pre_scale = fw_ref[...] * (1.0 + sc_ref[0])
    for r in range(oh_ref.shape[0] // OUT_ROWS):
        rows = pl.ds(r * OUT_ROWS, OUT_ROWS)
        o = jnp.concatenate([oh_ref[rows, :], og_ref[rows, :]], axis=1)
        y = jnp.dot(o, w_ref[...], preferred_element_type=F32)
        x1 = x_ref[rows, :] + _rms(y, post_scale)
        x1_ref[rows, :] = x1
        h2_ref[rows, :] = (_rms(x1, pre_scale) + sh_ref[0]).astype(BF16)


def _outproj(o_hg, o_gdn, w_out, x2, gt, post_w, ffn_w, sc, sh, seq, bm=512):
    m, d = x2.shape
    kh = o_hg.shape[1]
    per_b = seq // bm
    vec = pl.BlockSpec((1, d), lambda i: (0, 0))
    mod = pl.BlockSpec((1, 1, d), lambda i: (i // per_b, 0, 0))
    return pl.pallas_call(
        _outproj_kernel,
        grid=(m // bm,),
        in_specs=[pl.BlockSpec((bm, kh), lambda i: (i, 0)),
                  pl.BlockSpec((bm, kh), lambda i: (i, 0)),
                  pl.BlockSpec((2 * kh, d), lambda i: (0, 0)),
                  pl.BlockSpec((bm, d), lambda i: (i, 0)),
                  mod, vec, vec, mod, mod],
        out_specs=[pl.BlockSpec((bm, d), lambda i: (i, 0)),
                   pl.BlockSpec((bm, d), lambda i: (i, 0))],
        out_shape=[jax.ShapeDtypeStruct((m, d), F32),
                   jax.ShapeDtypeStruct((m, d), BF16)],
        compiler_params=_cparams(("arbitrary",)),
        name="outproj",
    )(o_hg, o_gdn, w_out, x2, gt, post_w, ffn_w, sc, sh)


NORM_ROWS = 128


def _ffn_kernel(h_ref, w1_ref, w2_ref, x1_ref, gt_ref, pw_ref, o_ref):
    f = pl.program_id(1)

    @pl.when(f == 0)
    def _():
        o_ref[...] = jnp.zeros_like(o_ref)

    a = jnp.maximum(jnp.dot(h_ref[...], w1_ref[...], preferred_element_type=F32), 0.0)
    o_ref[...] += jnp.dot((a * a).astype(BF16), w2_ref[...], preferred_element_type=F32)

    @pl.when(f == pl.num_programs(1) - 1)
    def _():
        scale = gt_ref[0] * pw_ref[...]

        def rows_body(r, carry):
            rows = pl.ds(_aligned(r * NORM_ROWS, NORM_ROWS), NORM_ROWS)
            o_ref[rows, :] = x1_ref[rows, :] + _rms(o_ref[rows, :], scale)
            return carry

        lax.fori_loop(0, o_ref.shape[0] // NORM_ROWS, rows_body, 0)


def _ffn(h2, w1, w2, x1, gt, post_w, seq, bm=512, bf=2048):
    m, d = h2.shape
    dff = w1.shape[1]
    per_b = seq // bm
    return pl.pallas_call(
        _ffn_kernel,
        grid=(m // bm, dff // bf),
        in_specs=[pl.BlockSpec((bm, d), lambda i, f: (i, 0)),
                  pl.BlockSpec((d, bf), lambda i, f: (0, f)),
                  pl.BlockSpec((bf, d), lambda i, f: (f, 0)),
                  pl.BlockSpec((bm, d), lambda i, f: (i, 0)),
                  pl.BlockSpec((1, 1, d), lambda i, f: (i // per_b, 0, 0)),
                  pl.BlockSpec((1, d), lambda i, f: (0, 0))],
        out_specs=pl.BlockSpec((bm, d), lambda i, f: (i, 0)),
        out_shape=jax.ShapeDtypeStruct((m, d), F32),
        compiler_params=_cparams(("arbitrary", "arbitrary")),
        name="ffn",
    )(h2, w1, w2, x1, gt, post_w)


def _layer(x, mod, pre_mix_w, post_mix_w, pre_ffn_w, post_ffn_w, w_in, lb_logits, hg_norm_w,
           conv_w, a_log, dt_bias, gdn_norm_w, w_out, w_ff1, w_ff2):
    batch, seq, d = x.shape
    hg_heads = lb_logits.shape[1]
    gdn_heads = a_log.shape[0]
    hg_cols = 4 * hg_heads * HEAD_DIM
    n_main = hg_cols + 4 * gdn_heads * HEAD_DIM

    x2 = x.reshape(batch * seq, d)
    sh_m, sc_m, gt_m, sh_f, sc_f, gt_f = [mod[:batch, None, i * d:(i + 1) * d] for i in range(6)]
    row = lambda v: v.reshape(1, -1)
    pad_lanes = lambda v: jnp.pad(v.reshape(1, -1), ((0, 0), (0, LANES - v.shape[-1])))

    w_in_t = w_in.T
    w_ab_t = jnp.pad(w_in_t[n_main:], ((0, LANES - (w_in_t.shape[0] - n_main)), (0, 0))).astype(BF16)
    h, ab = _prep(x2, row(pre_mix_w), sc_m, sh_m, w_ab_t, seq)
    proj = _inproj(h, w_in_t, n_main)

    o_hg, w_ff2_b, w_out_b = _hgrn(proj, lb_logits.reshape(lb_logits.shape[0], -1), row(hg_norm_w),
                                   [w_ff2, w_out], batch, seq, hg_heads, 0)
    o_gdn, w_ff1_b = _gdn(proj, ab, conv_w, pad_lanes(a_log), pad_lanes(dt_bias), row(gdn_norm_w),
                          [w_ff1], batch, seq, gdn_heads, hg_cols // HEAD_DIM)

    x1, h2 = _outproj(o_hg, o_gdn, w_out_b, x2, gt_m, row(post_mix_w), row(pre_ffn_w), sc_f, sh_f,
                      seq)
    out = _ffn(h2, w_ff1_b, w_ff2_b, x1, gt_f, row(post_ffn_w), seq)
    return out.reshape(batch, seq, d)


def kernel(x, c, w_ada, b_ada, pre_mix_norm, post_mix_norm, pre_ffn_norm, post_ffn_norm, w_in,
           hg_lb_logits, hg_norm, gdn_conv_w, gdn_a_log, gdn_dt_bias, gdn_norm, w_out, w_ff1, w_ff2):
    depth = w_ada.shape[0]
    assert depth == 1 and hg_lb_logits.shape[0] == 2, "single-layer configuration only"
    batch = c.shape[0]
    c_pad = jnp.pad(c, ((0, 8 - batch), (0, 0)))
    for l in range(depth):
        mod = _ada(c_pad, w_ada[l], b_ada[l][None, :])
        x = _layer(x, mod, pre_mix_norm[l], post_mix_norm[l], pre_ffn_norm[l], post_ffn_norm[l],
                   w_in[l], hg_lb_logits, hg_norm[l], gdn_conv_w[l], gdn_a_log[l], gdn_dt_bias[l],
                   gdn_norm[l], w_out[l], w_ff1[l], w_ff2[l])
    return x
```
